```python
import numpy as np
import jax
import jax.numpy as jnp
from jax import lax

D_MODEL = 1024
BATCH = 2
SEQ = 8192
DEPTH = 2

HEAD_DIM = 64
D_MIX = D_MODEL
GM_GROUPS = 4
GM_WIDTH = GM_GROUPS * HEAD_DIM
GM_CHUNK = 128
NSA_HEADS = 8
NSA_KV_HEADS = 2
NSA_WIDTH = NSA_HEADS * HEAD_DIM
NSA_KV_WIDTH = NSA_KV_HEADS * HEAD_DIM
CMP_BLOCK = 32
CMP_STRIDE = 16
CMP_HIDDEN = 2 * HEAD_DIM
SLC_BLOCK = 64
N_SLC = 16
WINDOW = 512
Q_BLOCK = 128
N_BRANCH = 3
HG_HEADS = 4
HG_WIDTH = HG_HEADS * HEAD_DIM
HG_CHUNK = 64
N_GROUPS = 4
EXPERTS_PER_GROUP = 4
N_EXPERTS = N_GROUPS * EXPERTS_PER_GROUP
TOP_K = 2
D_EXPERT = 512
IN_SPLITS = (GM_WIDTH, GM_WIDTH, NSA_WIDTH) + (NSA_KV_WIDTH,) * 6 + (N_BRANCH * NSA_HEADS,) + (HG_WIDTH,) * 4
IN_COLS = 2 * GM_WIDTH + NSA_WIDTH + 6 * NSA_KV_WIDTH + N_BRANCH * NSA_HEADS + 4 * HG_WIDTH
DEEPNORM_ALPHA = (2.0 * DEPTH) ** 0.25
DEEPNORM_BETA = (8.0 * DEPTH) ** -0.25
LN_EPS = 1e-5
RMS_EPS = 1e-6
NEG_INF = -1e30
FORCE_SELECT = 1e4

kernel_name = 'hybrid_gmlp_nsa_hgrn2_hmoe_deepnorm'


def layer_norm(x, g, b):
    xf = x.astype(jnp.float32)
    mu = jnp.mean(xf, -1, keepdims=True)
    var = jnp.mean(jnp.square(xf - mu), -1, keepdims=True)
    return ((xf - mu) * lax.rsqrt(var + LN_EPS) * g + b).astype(x.dtype)


def head_rms_norm(x, gain):
    xf = x.astype(jnp.float32)
    y = xf * lax.rsqrt(jnp.mean(xf * xf, -1, keepdims=True) + RMS_EPS)
    return (y * gain.reshape(x.shape[-2], x.shape[-1])).astype(x.dtype)


def gmlp_mixer(u, v, v_gain, v_bias, w_s, b_s):
    B, T, G, dh = v.shape
    nc = T // GM_CHUNK
    v = layer_norm(v, v_gain.reshape(G, dh), v_bias.reshape(G, dh))
    causal = jnp.tril(jnp.ones((GM_CHUNK, GM_CHUNK), dtype=bool))
    w = jnp.where(causal, w_s, 0)
    vc = v.reshape(B, nc, GM_CHUNK, G, dh)
    z = jnp.einsum('gts,bcsgd->bctgd', w, vc) + b_s.T[None, None, :, :, None]
    return u * z.reshape(B, T, G, dh)


def compress_blocks(k, pos, w1, w2):
    B, T, H, dh = k.shape
    nc = (T - CMP_BLOCK) // CMP_STRIDE + 1
    idx = np.arange(nc)[:, None] * CMP_STRIDE + np.arange(CMP_BLOCK)[None, :]
    blk = k[:, idx] + pos[None, None, :, None, :]
    blk = blk.transpose(0, 3, 1, 2, 4).reshape(B, H, nc, CMP_BLOCK * dh)
    return jax.nn.gelu(blk @ w1) @ w2


def nsa_mixer(q, k_c, v_c, k_s, v_s, k_w, v_w, gates, cmp_pos, cmp_w1, cmp_w2):
    B, T, H, dh = q.shape
    Hkv = k_c.shape[2]
    G = H // Hkv
    nb = T // Q_BLOCK
    ns = T // SLC_BLOCK
    n_sel = min(N_SLC, ns)
    scale = dh ** -0.5
    kcmp = compress_blocks(k_c, cmp_pos[0], cmp_w1[0], cmp_w2[0])
    vcmp = compress_blocks(v_c, cmp_pos[1], cmp_w1[1], cmp_w2[1])
    ncmp = kcmp.shape[2]
    cmp_end = (np.arange(ncmp) * CMP_STRIDE + CMP_BLOCK - 1).astype(np.int32)
    ii = np.arange(ncmp)[:, None]
    jj = np.arange(ns)[None, :]
    overlap = jnp.asarray(((ii * CMP_STRIDE < (jj + 1) * SLC_BLOCK) &
                           (ii * CMP_STRIDE + CMP_BLOCK > jj * SLC_BLOCK)).astype(np.float32))
    ksb = k_s.transpose(0, 2, 1, 3).reshape(B, Hkv, ns, SLC_BLOCK, dh)
    vsb = v_s.transpose(0, 2, 1, 3).reshape(B, Hkv, ns, SLC_BLOCK, dh)
    kwp = jnp.pad(k_w.transpose(0, 2, 1, 3), ((0, 0), (0, 0), (WINDOW, 0), (0, 0)))
    vwp = jnp.pad(v_w.transpose(0, 2, 1, 3), ((0, 0), (0, 0), (WINDOW, 0), (0, 0)))
    qb = q.reshape(B, nb, Q_BLOCK, Hkv, G, dh).transpose(1, 0, 3, 4, 2, 5)
    gb = gates.reshape(B, nb, Q_BLOCK, Hkv, G, N_BRANCH).transpose(1, 0, 3, 4, 2, 5)
    b_idx = jnp.arange(B)[:, None, None, None]
    h_idx = jnp.arange(Hkv)[None, :, None, None]
    blk_ids = jnp.arange(ns)
    blk_start = blk_ids * SLC_BLOCK

    def block_fn(args):
        c, qc, gc = args
        t = c * Q_BLOCK + jnp.arange(Q_BLOCK)
        s = jnp.einsum('bhgqd,bhnd->bhgqn', qc, kcmp).astype(jnp.float32) * scale
        valid = cmp_end[None, :] <= t[:, None]
        p = jax.nn.softmax(jnp.where(valid, s, NEG_INF), axis=-1) * valid
        o_cmp = jnp.einsum('bhgqn,bhnd->bhgqd', p.astype(vcmp.dtype), vcmp)
        imp = jnp.einsum('bhgqn,nj->bhqj', p, overlap)
        cur = t // SLC_BLOCK
        forced = (blk_ids[None, :] == 0) | (blk_ids[None, :] == cur[:, None]) | (blk_ids[None, :] == cur[:, None] - 1)
        causal_blk = blk_start[None, :] <= t[:, None]
        imp = jnp.where(causal_blk, imp + jnp.where(forced, FORCE_SELECT, 0.0), NEG_INF)
        top_val, top_idx = lax.top_k(imp, n_sel)
        sel_ok = top_val > 0.5 * NEG_INF
        kg = ksb[b_idx, h_idx, top_idx]
        vg = vsb[b_idx, h_idx, top_idx]
        pos = top_idx[..., None] * SLC_BLOCK + jnp.arange(SLC_BLOCK)
        ok = sel_ok[..., None] & (pos <= t[None, None, :, None, None])
        s = jnp.einsum('bhgqd,bhqnld->bhgqnl', qc, kg).astype(jnp.float32) * scale
        s = jnp.where(ok[:, :, None], s, NEG_INF).reshape(B, Hkv, G, Q_BLOCK, n_sel * SLC_BLOCK)
        p = jax.nn.softmax(s, axis=-1).reshape(B, Hkv, G, Q_BLOCK, n_sel, SLC_BLOCK)
        o_slc = jnp.einsum('bhgqnl,bhqnld->bhgqd', p.astype(vg.dtype), vg)
        kwc = lax.dynamic_slice_in_dim(kwp, c * Q_BLOCK, Q_BLOCK + WINDOW, axis=2)
        vwc = lax.dynamic_slice_in_dim(vwp, c * Q_BLOCK, Q_BLOCK + WINDOW, axis=2)
        kpos = c * Q_BLOCK - WINDOW + jnp.arange(Q_BLOCK + WINDOW)
        okw = (kpos[None, :] <= t[:, None]) & (kpos[None, :] > t[:, None] - WINDOW) & (kpos[None, :] >= 0)
        s = jnp.einsum('bhgqd,bhkd->bhgqk', qc, kwc).astype(jnp.float32) * scale
        p = jax.nn.softmax(jnp.where(okw, s, NEG_INF), axis=-1)
        o_win = jnp.einsum('bhgqk,bhkd->bhgqd', p.astype(vwc.dtype), vwc)
        out = gc[..., 0:1] * o_cmp + gc[..., 1:2] * o_slc + gc[..., 2:3] * o_win
        return out.astype(qc.dtype)

    out = lax.map(block_fn, (jnp.arange(nb), qb, gb))
    return out.transpose(1, 0, 4, 2, 3, 5).reshape(B, T, H, dh)


def hgrn2_mixer(q, f_logit, i, lower_bound):
    B, T, H, dk = q.shape
    dv = i.shape[-1]
    dtype = q.dtype
    lb = lower_bound.reshape(H, dk).astype(jnp.float32)
    fl = f_logit.astype(jnp.float32)
    log_f = jnp.logaddexp(jnp.log(lb), jnp.log1p(-lb) + jax.nn.log_sigmoid(fl))
    k = (1.0 - lb) * jax.nn.sigmoid(-fl)
    nc = T // HG_CHUNK

    def chunks(a):
        return a.astype(jnp.float32).reshape(B, nc, HG_CHUNK, H, -1).transpose(1, 0, 3, 2, 4)

    causal = jnp.tril(jnp.ones((HG_CHUNK, HG_CHUNK), dtype=bool))

    def step(S, inp):
        qc, kc, ic, lfc = inp
        b = jnp.cumsum(lfc, axis=2)
        diff = b[:, :, :, None, :] - b[:, :, None, :, :]
        decay = jnp.exp(jnp.where(causal[:, :, None], diff, -jnp.inf))
        attn = jnp.einsum('bhtd,bhsd,bhtsd->bhts', qc, kc, decay)
        o = jnp.einsum('bhts,bhsv->bhtv', attn, ic) + jnp.einsum('bhtd,bhdv->bhtv', qc * jnp.exp(b), S)
        b_last = b[:, :, -1, :]
        S = jnp.exp(b_last)[..., None] * S + jnp.einsum('bhsd,bhsv->bhdv', kc * jnp.exp(b_last[:, :, None, :] - b), ic)
        return S, o

    S0 = jnp.zeros((B, H, dk, dv), jnp.float32)
    _, o = lax.scan(step, S0, (chunks(q), chunks(k), chunks(i), chunks(log_f)))
    return o.transpose(1, 0, 3, 2, 4).reshape(B, T, H, dv).astype(dtype)


def token_mixer(x, w_in, gm_v_gain, gm_v_bias, gm_w_s, gm_b_s, cmp_pos, cmp_w1, cmp_w2,
                nsa_gate_b, lower_bound, out_gain, w_out):
    B, T, _ = x.shape
    h = x @ w_in
    points = np.cumsum(IN_SPLITS)[:-1].tolist()
    (gu, gv, nq, kc, vc, ks, vs, kw, vw, ng, hq, hf, hi, hg) = jnp.split(h, points, axis=-1)

    def heads(a):
        return a.reshape(B, T, -1, HEAD_DIM)

    y_a = gmlp_mixer(heads(jax.nn.gelu(gu)), heads(jax.nn.gelu(gv)), gm_v_gain, gm_v_bias, gm_w_s, gm_b_s)
    gates = jax.nn.sigmoid(ng + nsa_gate_b).reshape(B, T, NSA_HEADS, N_BRANCH)
    y_b = nsa_mixer(heads(nq), heads(kc), heads(vc), heads(ks), heads(vs), heads(kw), heads(vw),
                    gates, cmp_pos, cmp_w1, cmp_w2)
    y_c = hgrn2_mixer(heads(hq), heads(hf), heads(hi), lower_bound)
    y = head_rms_norm(jnp.concatenate([y_a, y_b, y_c], axis=2), out_gain).reshape(B, T, D_MIX)
    n_ab = GM_WIDTH + NSA_WIDTH
    y = jnp.concatenate([y[..., :n_ab], y[..., n_ab:] * jax.nn.silu(hg)], axis=-1)
    return y @ w_out


def hier_moe(x, wg, bg, we, be, w_gate, w_up, w_down):
    B, T, D = x.shape
    xt = x.reshape(-1, D)
    pg = jax.nn.softmax((xt @ wg + bg).astype(jnp.float32), axis=-1)
    g_val, g_idx = lax.top_k(pg, 1)
    le = (xt @ we + be).astype(jnp.float32).reshape(-1, N_GROUPS, EXPERTS_PER_GROUP)
    le_sel = jnp.take_along_axis(le, g_idx[:, :, None], axis=1)[:, 0]
    pe = jax.nn.softmax(le_sel, axis=-1)
    e_val, e_idx = lax.top_k(pe, TOP_K)
    e_val = e_val / jnp.sum(e_val, -1, keepdims=True)
    within = jnp.einsum('nk,nke->ne', e_val, jax.nn.one_hot(e_idx, EXPERTS_PER_GROUP, dtype=jnp.float32))
    gate = (g_val * jax.nn.one_hot(g_idx[:, 0], N_GROUPS, dtype=jnp.float32))[:, :, None] * within[:, None, :]
    gate = gate.reshape(-1, N_EXPERTS).astype(x.dtype)
    out = jnp.zeros_like(xt)
    for e in range(N_EXPERTS):
        hdn = jax.nn.silu(xt @ w_gate[e]) * (xt @ w_up[e])
        out = out + gate[:, e:e + 1] * (hdn @ w_down[e])
    return out.reshape(B, T, D)


def setup_inputs(seed: int = 0) -> dict:
    key = jax.random.key(seed)
    ks = jax.random.split(key, 26)

    def nrm(k, shape, s):
        return jax.random.normal(k, shape, jnp.float32) * s

    L = DEPTH
    return {
        'x': nrm(ks[0], (BATCH, SEQ, D_MODEL), 1.0),
        'w_in': nrm(ks[1], (L, D_MODEL, IN_COLS), D_MODEL ** -0.5),
        'gm_v_gain': 1.0 + nrm(ks[2], (L, GM_WIDTH), 0.02),
        'gm_v_bias': nrm(ks[3], (L, GM_WIDTH), 0.02),
        'gm_w_s': nrm(ks[4], (L, GM_GROUPS, GM_CHUNK, GM_CHUNK), GM_CHUNK ** -0.5),
        'gm_b_s': 1.0 + nrm(ks[5], (L, GM_GROUPS, GM_CHUNK), 0.02),
        'cmp_pos': nrm(ks[6], (L, 2, CMP_BLOCK, HEAD_DIM), 0.1),
        'cmp_w1': nrm(ks[7], (L, 2, CMP_BLOCK * HEAD_DIM, CMP_HIDDEN), (CMP_BLOCK * HEAD_DIM) ** -0.5),
        'cmp_w2': nrm(ks[8], (L, 2, CMP_HIDDEN, HEAD_DIM), CMP_HIDDEN ** -0.5),
        'nsa_gate_b': nrm(ks[9], (L, N_BRANCH * NSA_HEADS), 0.01),
        'hg_lower': nrm(ks[10], (L, HG_WIDTH), 0.1),
        'out_gain': 1.0 + nrm(ks[11], (L, D_MIX), 0.02),
        'w_out': nrm(ks[12], (L, D_MIX, D_MODEL), D_MIX ** -0.5 * DEEPNORM_BETA),
        'ln1_g': 1.0 + nrm(ks[13], (L, D_MODEL), 0.02),
        'ln1_b': nrm(ks[14], (L, D_MODEL), 0.02),
        'router_group_w': nrm(ks[15], (L, D_MODEL, N_GROUPS), D_MODEL ** -0.5),
        'router_group_b': nrm(ks[16], (L, N_GROUPS), 0.01),
        'router_expert_w': nrm(ks[17], (L, D_MODEL, N_EXPERTS), D_MODEL ** -0.5),
        'router_expert_b': nrm(ks[18], (L, N_EXPERTS), 0.01),
        'exp_w_gate': nrm(ks[19], (L, N_EXPERTS, D_MODEL, D_EXPERT), D_MODEL ** -0.5),
        'exp_w_up': nrm(ks[20], (L, N_EXPERTS, D_MODEL, D_EXPERT), D_MODEL ** -0.5),
        'exp_w_down': nrm(ks[21], (L, N_EXPERTS, D_EXPERT, D_MODEL), D_EXPERT ** -0.5 * DEEPNORM_BETA),
        'ln2_g': 1.0 + nrm(ks[22], (L, D_MODEL), 0.02),
        'ln2_b': nrm(ks[23], (L, D_MODEL), 0.02),
    }


def reference(x, w_in, gm_v_gain, gm_v_bias, gm_w_s, gm_b_s, cmp_pos, cmp_w1, cmp_w2, nsa_gate_b,
              hg_lower, out_gain, w_out, ln1_g, ln1_b, router_group_w, router_group_b,
              router_expert_w, router_expert_b, exp_w_gate, exp_w_up, exp_w_down, ln2_g, ln2_b):
    lb_all = jnp.cumsum(jax.nn.softmax(hg_lower.astype(jnp.float32), axis=0), axis=0)
    lb_all = lb_all - lb_all[0]
    for l in range(DEPTH):
        y = token_mixer(x, w_in[l], gm_v_gain[l], gm_v_bias[l], gm_w_s[l], gm_b_s[l], cmp_pos[l],
                        cmp_w1[l], cmp_w2[l], nsa_gate_b[l], lb_all[l], out_gain[l], w_out[l])
        x = layer_norm(DEEPNORM_ALPHA * x + y, ln1_g[l], ln1_b[l])
        y = hier_moe(x, router_group_w[l], router_group_b[l], router_expert_w[l], router_expert_b[l],
                     exp_w_gate[l], exp_w_up[l], exp_w_down[l])
        x = layer_norm(DEEPNORM_ALPHA * x + y, ln2_g[l], ln2_b[l])
    return x
```

```python
import functools

import numpy as np
import jax
import jax.numpy as jnp
from jax import lax
from jax.experimental import pallas as pl
from jax.experimental.pallas import tpu as pltpu

F32 = jnp.float32
BF16 = jnp.bfloat16

D_MODEL = 1024
HEAD_DIM = 64
GM_GROUPS = 4
GM_WIDTH = GM_GROUPS * HEAD_DIM
GM_CHUNK = 128
NSA_HEADS = 8
NSA_KV_HEADS = 2
NSA_GROUP = NSA_HEADS // NSA_KV_HEADS
NSA_WIDTH = NSA_HEADS * HEAD_DIM
NSA_KV_WIDTH = NSA_KV_HEADS * HEAD_DIM
CMP_BLOCK = 32
CMP_STRIDE = 16
SLC_BLOCK = 64
N_SLC = 16
WINDOW = 512
Q_BLOCK = 128
N_BRANCH = 3
HG_HEADS = 4
HG_WIDTH = HG_HEADS * HEAD_DIM
HG_CHUNK = 64
N_GROUPS = 4
EXPERTS_PER_GROUP = 4
N_EXPERTS = N_GROUPS * EXPERTS_PER_GROUP
D_EXPERT = 512
DEPTH = 2
DEEPNORM_ALPHA = (2.0 * DEPTH) ** 0.25
LN_EPS = 1e-5
RMS_EPS = 1e-6
NEG_INF = -1e30
FORCE_SELECT = 1e4
TAKEN = -3e38
ATTN_SCALE = HEAD_DIM ** -0.5

LANES = 128
GATE_PAD = LANES
ROUTER_PAD = LANES
EXPERT_LANE0 = N_GROUPS
VMEM_LIMIT = 56 * 1024 * 1024

COL_GU = 0
COL_GV = 256
COL_NQ = 512
COL_KC = 1024
COL_KS = 1280
COL_VS = 1408
COL_KW = 1536
COL_VW = 1664
COL_HQ = 1792
COL_HF = 2048
COL_HI = 2304
COL_HG = 2560
COL_NG = 2816
H_COLS = COL_NG + GATE_PAD

ATTN_TK = 512


def _dot(a, b):
    return jnp.dot(a, b, preferred_element_type=F32)


def _dot_nt(a, b):
    return lax.dot_general(a, b, (((1,), (1,)), ((), ())), preferred_element_type=F32)


def _dot_tn(a, b):
    return lax.dot_general(a, b, (((0,), (0,)), ((), ())), preferred_element_type=F32)


def _split2(x):
    hi = x.astype(BF16)
    lo = (x - hi.astype(F32)).astype(BF16)
    return hi, lo


def _split3(x):
    hi = x.astype(BF16)
    r = x - hi.astype(F32)
    mid = r.astype(BF16)
    lo = (r - mid.astype(F32)).astype(BF16)
    return hi, mid, lo


def _dot2(x, m):
    hi, lo = _split2(x)
    return _dot(hi, m) + _dot(lo, m)


def _gelu(x):
    return 0.5 * x * (1.0 + jnp.tanh(0.7978845608028654 * (x + 0.044715 * (x * x * x))))


def _sigmoid(x):
    return 1.0 / (1.0 + jnp.exp(-x))


def _log_sigmoid(x):
    return jnp.minimum(x, 0.0) - jnp.log1p(jnp.exp(-jnp.abs(x)))


def _layer_norm(z, g, b):
    mu = jnp.mean(z, axis=-1, keepdims=True)
    zc = z - mu
    var = jnp.mean(zc * zc, axis=-1, keepdims=True)
    return zc * lax.rsqrt(var + LN_EPS) * g + b


def _params(*sem):
    return pltpu.CompilerParams(dimension_semantics=sem, vmem_limit_bytes=VMEM_LIMIT)


def _inproj_body(x_ref, w_ref, o_ref):
    o_ref[...] = _dot(x_ref[...].astype(BF16), w_ref[...])


def _inproj(x2, w):
    n, d = x2.shape
    c = w.shape[1]
    tm = 256
    return pl.pallas_call(
        _inproj_body,
        grid=(n // tm,),
        in_specs=[pl.BlockSpec((tm, d), lambda i: (i, 0)), pl.BlockSpec((d, c), lambda i: (0, 0))],
        out_specs=pl.BlockSpec((tm, c), lambda i: (i, 0)),
        out_shape=jax.ShapeDtypeStruct((n, c), F32),
        compiler_params=_params("parallel"),
        name="inproj",
    )(x2, w)


def _group_mean_matrix(width):
    g = np.arange(width) // HEAD_DIM
    return jnp.asarray((g[:, None] == g[None, :]).astype(np.float32) / HEAD_DIM, dtype=BF16)


def _gmlp_body(gu_ref, gv_ref, gain_ref, bias_ref, ws_ref, bsx_ref, gm_ref, og_ref, o_ref, *, chunks):
    gm = gm_ref[...]
    row = lax.broadcasted_iota(jnp.int32, (GM_CHUNK, GM_CHUNK), 0)
    col = lax.broadcasted_iota(jnp.int32, (GM_CHUNK, GM_CHUNK), 1)
    tril = row >= col
    lane_grp = lax.shift_right_logical(lax.broadcasted_iota(jnp.int32, (GM_CHUNK, GM_WIDTH), 1), 6)
    ws = [jnp.where(tril, ws_ref[g], 0.0).astype(BF16) for g in range(GM_GROUPS)]
    for c in range(chunks):
        sl = slice(c * GM_CHUNK, (c + 1) * GM_CHUNK)
        u = _gelu(gu_ref[sl, :])
        v = _gelu(gv_ref[sl, :])
        mu = _dot2(v, gm)
        vc = v - mu
        var = _dot2(vc * vc, gm)
        vn = (vc * lax.rsqrt(var + LN_EPS) * gain_ref[...] + bias_ref[...]).astype(BF16)
        z = bsx_ref[...]
        for g in range(GM_GROUPS):
            z = z + jnp.where(lane_grp == g, _dot(ws[g], vn), 0.0)
        y = u * z
        ms = _dot2(y * y, gm)
        o_ref[sl, :] = y * lax.rsqrt(ms + RMS_EPS) * og_ref[...]


def _gmlp(h, gain, bias, w_s, b_s, out_gain_a):
    n = h.shape[0]
    chunks = 4
    tm = chunks * GM_CHUNK
    bsx = jnp.repeat(b_s.T, HEAD_DIM, axis=1)
    row = lambda a: a.reshape(1, -1)
    full = lambda shape: pl.BlockSpec(shape, lambda i: (0,) * len(shape))
    return pl.pallas_call(
        functools.partial(_gmlp_body, chunks=chunks),
        grid=(n // tm,),
        in_specs=[
            pl.BlockSpec((tm, GM_WIDTH), lambda i: (i, COL_GU // GM_WIDTH)),
            pl.BlockSpec((tm, GM_WIDTH), lambda i: (i, COL_GV // GM_WIDTH)),
            full((1, GM_WIDTH)), full((1, GM_WIDTH)),
            full((GM_GROUPS, GM_CHUNK, GM_CHUNK)), full((GM_CHUNK, GM_WIDTH)),
            full((GM_WIDTH, GM_WIDTH)), full((1, GM_WIDTH)),
        ],
        out_specs=pl.BlockSpec((tm, GM_WIDTH), lambda i: (i, 0)),
        out_shape=jax.ShapeDtypeStruct((n, GM_WIDTH), F32),
        compiler_params=_params("parallel"),
        name="gmlp",
    )(h, h, row(gain), row(bias), w_s, bsx, _group_mean_matrix(GM_WIDTH), row(out_gain_a))


N_LEVELS = 6


def _hgrn_constants():
    c = HG_CHUNK
    t = np.arange(c)[:, None]
    u = np.arange(c)[None, :]
    mats, masks = [], []
    m = c // 2
    while m >= 1:
        p = (t // (2 * m)) * (2 * m) + m - 1
        mats.append(np.where(t > p, (u > p) & (u <= t), (u > t) & (u <= p)))
        masks.append(((t // (2 * m)) == (u // (2 * m))) & ((t % (2 * m)) >= m) & ((u % (2 * m)) < m))
        m //= 2
    mats.append(u <= t)
    mats.append(u > t)
    masks.append(np.eye(c, dtype=bool))
    a_all = np.concatenate(mats, 0).astype(np.float32)
    mask2 = np.stack([np.concatenate([mk, mk], 0) for mk in masks]).astype(np.float32)
    return jnp.asarray(a_all, dtype=BF16), jnp.asarray(mask2)


def _hgrn_body(q_ref, f_ref, i_ref, loglb_ref, log1m_ref, oml_ref, a_ref, mk_ref, o_ref, st_ref, *, chunks):
    @pl.when(pl.program_id(1) == 0)
    def _():
        st_ref[...] = jnp.zeros_like(st_ref)

    c = HG_CHUNK
    a_all = a_ref[...]
    lane = lax.broadcasted_iota(jnp.int32, (c, LANES), 1)
    lo_half = lane < HEAD_DIM
    r2 = lax.broadcasted_iota(jnp.int32, (LANES, LANES), 0)
    c2 = lax.broadcasted_iota(jnp.int32, (LANES, LANES), 1)
    same_head = (r2 < HEAD_DIM) == (c2 < HEAD_DIM)
    for ch in range(chunks):
        sl = slice(ch * c, (ch + 1) * c)
        q = q_ref[sl, :]
        fl = f_ref[sl, :]
        iv = i_ref[sl, :]
        la = loglb_ref[...]
        lb2 = log1m_ref[...] + _log_sigmoid(fl)
        lf = jnp.maximum(la, lb2) + jnp.log1p(jnp.exp(-jnp.abs(la - lb2)))
        k = oml_ref[...] * (1.0 / (1.0 + jnp.exp(fl)))
        hi, mid, lo = _split3(lf)
        d_all = _dot(a_all, hi) + _dot(a_all, mid) + _dot(a_all, lo)
        e_all = jnp.exp(d_all)
        for p in range(HG_HEADS // 2):
            ls = slice(p * LANES, (p + 1) * LANES)
            qp, kp, ip = q[:, ls], k[:, ls], iv[:, ls]
            att = jnp.zeros((2 * c, c), F32)
            for lv in range(N_LEVELS + 1):
                if lv < N_LEVELS:
                    el = e_all[lv * c:(lv + 1) * c, ls]
                    ql, kl = qp * el, kp * el
                else:
                    ql, kl = qp, kp
                qstack = jnp.concatenate([jnp.where(lo_half, ql, 0.0), jnp.where(lo_half, 0.0, ql)], axis=0)
                att = att + _dot_nt(qstack.astype(BF16), kl.astype(BF16)) * mk_ref[lv]
            ipb = ip.astype(BF16)
            o_intra = jnp.where(lo_half, _dot(att[:c].astype(BF16), ipb), _dot(att[c:].astype(BF16), ipb))
            eb = e_all[N_LEVELS * c:(N_LEVELS + 1) * c, ls]
            ebl = e_all[(N_LEVELS + 1) * c:(N_LEVELS + 2) * c, ls]
            st = st_ref[p]
            o_inter = _dot_nt((qp * eb).astype(BF16), st.astype(BF16))
            o_ref[sl, ls] = o_intra + o_inter
            upd = _dot_tn(ipb, (kp * ebl).astype(BF16))
            st_ref[p] = st * eb[c - 1:c, :] + jnp.where(same_head, upd, 0.0)


def _hgrn(h, lb, batch):
    n = h.shape[0]
    t = n // batch
    chunks = 4
    tm = chunks * HG_CHUNK
    steps = t // tm
    a_all, mask2 = _hgrn_constants()
    lb = lb.reshape(1, HG_WIDTH).astype(F32)
    full = lambda shape: pl.BlockSpec(shape, lambda b, i: (0,) * len(shape))
    col = lambda off: pl.BlockSpec((tm, HG_WIDTH), lambda b, i: (b * steps + i, off // HG_WIDTH))
    return pl.pallas_call(
        functools.partial(_hgrn_body, chunks=chunks),
        grid=(batch, steps),
        in_specs=[col(COL_HQ), col(COL_HF), col(COL_HI),
                  full((1, HG_WIDTH)), full((1, HG_WIDTH)), full((1, HG_WIDTH)),
                  full(a_all.shape), full(mask2.shape)],
        out_specs=pl.BlockSpec((tm, HG_WIDTH), lambda b, i: (b * steps + i, 0)),
        out_shape=jax.ShapeDtypeStruct((n, HG_WIDTH), F32),
        scratch_shapes=[pltpu.VMEM((HG_HEADS // 2, LANES, LANES), F32)],
        compiler_params=_params("parallel", "arbitrary"),
        name="hgrn2",
    )(h, h, h, jnp.log(lb), jnp.log1p(-lb), 1.0 - lb, a_all, mask2)


def _compress_body(x_ref, w1_ref, w2_ref, pos_ref, o_ref):
    x = x_ref[0, 0, 0].astype(BF16)
    w1 = w1_ref[0].astype(BF16)
    half = w1.shape[0] // 2
    a = _dot(x, w1[:half])
    b = _dot(x, w1[half:])
    pos = jnp.broadcast_to(pos_ref[0], (8, pos_ref.shape[-1])).astype(BF16)
    pw = _dot(pos, w1)[0:1]
    nrow = x.shape[0]
    b_next = jnp.concatenate([b[1:], b[:1]], axis=0)
    hid = _gelu(a + b_next + pw)
    out = _dot(hid.astype(BF16), w2_ref[0].astype(BF16))
    rid = lax.broadcasted_iota(jnp.int32, out.shape, 0)
    o_ref[0, 0, 0] = jnp.where(rid < nrow - 1, out, 0.0)


def _compress(xkv, w1, w2d, pos):
    b, _, hk, nr, wd = xkv.shape
    return pl.pallas_call(
        _compress_body,
        grid=(b, 2, hk),
        in_specs=[
            pl.BlockSpec((1, 1, 1, nr, wd), lambda i, j, k: (i, j, k, 0, 0)),
            pl.BlockSpec((1,) + w1.shape[1:], lambda i, j, k: (j, 0, 0)),
            pl.BlockSpec((1,) + w2d.shape[1:], lambda i, j, k: (j, 0, 0)),
            pl.BlockSpec((1, 1, pos.shape[-1]), lambda i, j, k: (j, 0, 0)),
        ],
        out_specs=pl.BlockSpec((1, 1, 1, nr, w2d.shape[-1]), lambda i, j, k: (i, j, k, 0, 0)),
        out_shape=jax.ShapeDtypeStruct((b, 2, hk, nr, w2d.shape[-1]), F32),
        compiler_params=_params("parallel", "parallel", "parallel"),
        name="nsa_compress",
    )(xkv, w1, w2d, pos)


def _select_body(q_ref, kvc_ref, ovt_ref, eye_ref, ocmp_ref, bias_ref, *, n_sel):
    c = pl.program_id(1)
    ncmp = kvc_ref.shape[-2]
    nblk = ovt_ref.shape[0]
    q = q_ref[...] * ATTN_SCALE
    lane = lax.broadcasted_iota(jnp.int32, (Q_BLOCK, LANES), 1)
    lo_half = lane < HEAD_DIM
    t_col = c * Q_BLOCK + lax.broadcasted_iota(jnp.int32, (Q_BLOCK, 1), 0)
    n_idx = lax.broadcasted_iota(jnp.int32, (1, ncmp), 1)
    valid = (n_idx * CMP_STRIDE + (CMP_BLOCK - 1)) <= t_col
    jb = lax.broadcasted_iota(jnp.int32, (nblk, 1), 0)
    t_row = c * Q_BLOCK + lax.broadcasted_iota(jnp.int32, (1, Q_BLOCK), 1)
    cur = lax.shift_right_logical(t_row, 6)
    forced = (jb == 0) | (jb == cur) | (jb == cur - 1)
    causal = jb * SLC_BLOCK <= t_row
    for hk in range(NSA_KV_HEADS):
        kc = kvc_ref[0, 0, hk].astype(BF16)
        vc = kvc_ref[0, 1, hk].astype(BF16)
        psum = jnp.zeros((Q_BLOCK, ncmp), F32)
        outs = []
        for g in range(NSA_GROUP):
            hq = hk * NSA_GROUP + g
            pair = q[:, (hq // 2) * LANES:(hq // 2 + 1) * LANES]
            qh = jnp.where(lo_half if hq % 2 == 0 else jnp.logical_not(lo_half), pair, 0.0).astype(BF16)
            s = _dot_nt(qh, kc)
            m = jnp.max(jnp.where(valid, s, NEG_INF), axis=-1, keepdims=True)
            e = jnp.where(valid, jnp.exp(s - m), 0.0)
            l = jnp.sum(e, axis=-1, keepdims=True)
            p = jnp.where(l > 0.0, e / l, 0.0)
            outs.append(_dot(p.astype(BF16), vc))
            psum = psum + p
        for j in range(NSA_GROUP // 2):
            col0 = (hk * (NSA_GROUP // 2) + j) * LANES
            ocmp_ref[:, col0:col0 + LANES] = jnp.where(lo_half, outs[2 * j], outs[2 * j + 1])
        ph, plo = _split2(psum)
        imp = _dot_nt(ovt_ref[...], ph) + _dot_nt(ovt_ref[...], plo)
        val = jnp.where(causal, imp + jnp.where(forced, FORCE_SELECT, 0.0), NEG_INF)
        sel = jnp.zeros((nblk, Q_BLOCK), F32)
        for _ in range(n_sel):
            m = jnp.max(val, axis=0, keepdims=True)
            idx = jnp.min(jnp.where(val == m, jb, nblk), axis=0, keepdims=True)
            hit = jb == idx
            sel = jnp.where(hit & (m > 0.5 * NEG_INF), 1.0, sel)
            val = jnp.where(hit, TAKEN, val)
        sel_q = _dot_nt(eye_ref[...], sel.astype(BF16))
        bias_ref[0, hk] = jnp.where(sel_q > 0.5, 0.0, NEG_INF).astype(BF16)


def _select(h, kvcmp, batch):
    n = h.shape[0]
    t = n // batch
    nq = t // Q_BLOCK
    ncmp = kvcmp.shape[-2]
    nblk = t // SLC_BLOCK
    ii = np.arange(ncmp)[None, :]
    jj = np.arange(nblk)[:, None]
    ovt = ((ii * CMP_STRIDE < (jj + 1) * SLC_BLOCK) & (ii * CMP_STRIDE + CMP_BLOCK > jj * SLC_BLOCK)
           & (ii < ncmp - 1))
    ovt = jnp.asarray(ovt.astype(np.float32), dtype=BF16)
    eye = jnp.asarray(np.eye(Q_BLOCK, dtype=np.float32), dtype=BF16)
    full = lambda shape: pl.BlockSpec(shape, lambda b, i: (0,) * len(shape))
    kv_spec = pl.BlockSpec((1,) + kvcmp.shape[1:], lambda b, i: (b, 0, 0, 0, 0))
    return pl.pallas_call(
        functools.partial(_select_body, n_sel=min(N_SLC, nblk)),
        grid=(batch, nq),
        in_specs=[pl.BlockSpec((Q_BLOCK, NSA_WIDTH), lambda b, i: (b * nq + i, COL_NQ // NSA_WIDTH)),
                  kv_spec, full(ovt.shape), full(eye.shape)],
        out_specs=[pl.BlockSpec((Q_BLOCK, NSA_WIDTH), lambda b, i: (b * nq + i, 0)),
                   pl.BlockSpec((1, NSA_KV_HEADS, Q_BLOCK, nblk), lambda b, i: (b, 0, i, 0))],
        out_shape=[jax.ShapeDtypeStruct((n, NSA_WIDTH), F32),
                   jax.ShapeDtypeStruct((batch, NSA_KV_HEADS, t, nblk), BF16)],
        compiler_params=_params("parallel", "parallel"),
        name="nsa_select",
    )(h, kvcmp, ovt, eye)


def _attn_body(q_ref, bias_ref, kaug_ref, vs_ref, kw_ref, vw_ref, oslc_ref, owin_ref,
               qaug_sc, m_sc, l_sc, acc_sc):
    c = pl.program_id(2)
    rows = NSA_GROUP * Q_BLOCK
    q4 = q_ref[...] * ATTN_SCALE
    lane = lax.broadcasted_iota(jnp.int32, (Q_BLOCK, LANES), 1)
    lo_half = lane < HEAD_DIM
    parts = []
    for g in range(NSA_GROUP):
        pair = q4[:, (g // 2) * LANES:(g // 2 + 1) * LANES]
        parts.append(jnp.where(lo_half if g % 2 == 0 else jnp.logical_not(lo_half), pair, 0.0))
    qpart = jnp.concatenate(parts, axis=0).astype(BF16)
    qaug_sc[:, 0:LANES] = qpart
    bias = bias_ref[0, 0]
    for g in range(NSA_GROUP):
        qaug_sc[g * Q_BLOCK:(g + 1) * Q_BLOCK, LANES:] = bias
    m_sc[...] = jnp.full(m_sc.shape, TAKEN, F32)
    l_sc[...] = jnp.zeros(l_sc.shape, F32)
    acc_sc[...] = jnp.zeros(acc_sc.shape, F32)
    t_col = c * Q_BLOCK + (lax.broadcasted_iota(jnp.int32, (rows, 1), 0) & (Q_BLOCK - 1))

    def body(j, carry):
        start = pl.multiple_of(j * ATTN_TK, ATTN_TK)
        k = kaug_ref[0, 0, pl.ds(start, ATTN_TK), :]
        s = _dot_nt(qaug_sc[...], k)
        kpos = start + lax.broadcasted_iota(jnp.int32, (1, ATTN_TK), 1)
        s = jnp.where(kpos <= t_col, s, NEG_INF)
        m_prev = m_sc[...]
        m_new = jnp.maximum(m_prev, jnp.max(s, axis=-1, keepdims=True))
        alpha = jnp.exp(m_prev - m_new)
        p = jnp.exp(s - m_new)
        l_sc[...] = alpha * l_sc[...] + jnp.sum(p, axis=-1, keepdims=True)
        acc_sc[...] = alpha * acc_sc[...] + _dot(p.astype(BF16), vs_ref[0, 0, pl.ds(start, ATTN_TK), :])
        m_sc[...] = m_new
        return carry

    lax.fori_loop(0, (c * Q_BLOCK + Q_BLOCK - 1) // ATTN_TK + 1, body, 0)
    o = acc_sc[...] / l_sc[...]
    for j in range(NSA_GROUP // 2):
        oslc_ref[:, j * LANES:(j + 1) * LANES] = jnp.where(
            lo_half, o[2 * j * Q_BLOCK:(2 * j + 1) * Q_BLOCK], o[(2 * j + 1) * Q_BLOCK:(2 * j + 2) * Q_BLOCK])

    span = WINDOW + Q_BLOCK
    wstart = pl.multiple_of(jnp.maximum(c * Q_BLOCK - WINDOW, 0), Q_BLOCK)
    s = _dot_nt(qpart, kw_ref[0, 0, pl.ds(wstart, span), :])
    kpos = wstart + lax.broadcasted_iota(jnp.int32, (1, span), 1)
    ok = (kpos <= t_col) & (kpos > t_col - WINDOW)
    s = jnp.where(ok, s, NEG_INF)
    m = jnp.max(s, axis=-1, keepdims=True)
    p = jnp.exp(s - m)
    l = jnp.sum(p, axis=-1, keepdims=True)
    ow = _dot(p.astype(BF16), vw_ref[0, 0, pl.ds(wstart, span), :]) / l
    for j in range(NSA_GROUP // 2):
        owin_ref[:, j * LANES:(j + 1) * LANES] = jnp.where(
            lo_half, ow[2 * j * Q_BLOCK:(2 * j + 1) * Q_BLOCK], ow[(2 * j + 1) * Q_BLOCK:(2 * j + 2) * Q_BLOCK])


def _attention(h, bias, kaug, vs, kw, vw, batch):
    n = h.shape[0]
    t = n // batch
    nq = t // Q_BLOCK
    gw = NSA_GROUP * HEAD_DIM
    rows = NSA_GROUP * Q_BLOCK
    res = lambda a: pl.BlockSpec((1, 1) + a.shape[2:], lambda b, k, i: (b, k, 0, 0))
    out_spec = pl.BlockSpec((Q_BLOCK, gw), lambda b, k, i: (b * nq + i, k))
    return pl.pallas_call(
        _attn_body,
        grid=(batch, NSA_KV_HEADS, nq),
        in_specs=[pl.BlockSpec((Q_BLOCK, gw), lambda b, k, i: (b * nq + i, COL_NQ // gw + k)),
                  pl.BlockSpec((1, 1, Q_BLOCK, bias.shape[-1]), lambda b, k, i: (b, k, i, 0)),
                  res(kaug), res(vs), res(kw), res(vw)],
        out_specs=[out_spec, out_spec],
        out_shape=[jax.ShapeDtypeStruct((n, NSA_WIDTH), F32)] * 2,
        scratch_shapes=[pltpu.VMEM((rows, kaug.shape[-1]), BF16),
                        pltpu.VMEM((rows, 1), F32), pltpu.VMEM((rows, 1), F32),
                        pltpu.VMEM((rows, LANES), F32)],
        compiler_params=_params("parallel", "parallel", "arbitrary"),
        name="nsa_attention",
    )(h, bias, kaug, vs, kw, vw)


def _branch_expanders():
    e = np.zeros((N_BRANCH, GATE_PAD, NSA_WIDTH), np.float32)
    for hq in range(NSA_HEADS):
        for br in range(N_BRANCH):
            e[br, hq * N_BRANCH + br, hq * HEAD_DIM:(hq + 1) * HEAD_DIM] = 1.0
    return jnp.asarray(e, dtype=BF16)


def _route(logits):
    lane = lax.broadcasted_iota(jnp.int32, logits.shape, 1)
    is_g = lane < N_GROUPS
    gl = jnp.where(is_g, logits, -jnp.inf)
    gmax = jnp.max(gl, axis=-1, keepdims=True)
    gsum = jnp.sum(jnp.where(is_g, jnp.exp(logits - gmax), 0.0), axis=-1, keepdims=True)
    g_val = 1.0 / gsum
    g_idx = jnp.min(jnp.where(is_g & (logits == gmax), lane, ROUTER_PAD), axis=-1, keepdims=True)
    e_lo = EXPERT_LANE0 + g_idx * EXPERTS_PER_GROUP
    in_sel = (lane >= e_lo) & (lane < e_lo + EXPERTS_PER_GROUP)
    el = jnp.where(in_sel, logits, -jnp.inf)
    m1 = jnp.max(el, axis=-1, keepdims=True)
    i1 = jnp.min(jnp.where(in_sel & (logits == m1), lane, ROUTER_PAD), axis=-1, keepdims=True)
    el2 = jnp.where(lane == i1, -jnp.inf, el)
    m2 = jnp.max(el2, axis=-1, keepdims=True)
    i2 = jnp.min(jnp.where(el2 == m2, lane, ROUTER_PAD), axis=-1, keepdims=True)
    r = jnp.exp(m2 - m1)
    w1 = 1.0 / (1.0 + r)
    w2 = r * w1
    return g_val * (jnp.where(lane == i1, w1, 0.0) + jnp.where(lane == i2, w2, 0.0))


def _out_body(ya_ref, ocmp_ref, oslc_ref, owin_ref, gl_ref, yc_ref, hg_ref, x_ref, gb_ref, ex_ref,
              gmb_ref, gmc_ref, og_ref, w_ref, lg_ref, lb_ref, wrh_ref, wrl_ref, br_ref, x1_ref, gate_ref):
    g = _sigmoid(gl_ref[...] + gb_ref[...])
    yb = (_dot2(g, ex_ref[0]) * ocmp_ref[...] + _dot2(g, ex_ref[1]) * oslc_ref[...]
          + _dot2(g, ex_ref[2]) * owin_ref[...])
    n_ab = GM_WIDTH + NSA_WIDTH
    ybn = yb * lax.rsqrt(_dot2(yb * yb, gmb_ref[...]) + RMS_EPS) * og_ref[:, GM_WIDTH:n_ab]
    yc = yc_ref[...]
    hg = hg_ref[...]
    ycn = yc * lax.rsqrt(_dot2(yc * yc, gmc_ref[...]) + RMS_EPS) * og_ref[:, n_ab:] * (hg * _sigmoid(hg))
    y = (_dot(ya_ref[...].astype(BF16), w_ref[0:GM_WIDTH, :]) + _dot(ybn.astype(BF16), w_ref[GM_WIDTH:n_ab, :])
         + _dot(ycn.astype(BF16), w_ref[n_ab:, :]))
    x1 = _layer_norm(DEEPNORM_ALPHA * x_ref[...] + y, lg_ref[...], lb_ref[...])
    x1_ref[...] = x1
    xh, xl = _split2(x1)
    logits = _dot(xh, wrh_ref[...]) + _dot(xh, wrl_ref[...]) + _dot(xl, wrh_ref[...]) + br_ref[...]
    gate_ref[...] = _route(logits)


def _outproj(ya, ocmp, oslc, owin, h, yc, x2, gate_b, out_gain, w_out, ln_g, ln_b, wr, br):
    n, d = x2.shape
    tm = 256
    row = lambda a: a.reshape(1, -1)
    gb = jnp.pad(gate_b, (0, GATE_PAD - gate_b.shape[0])).reshape(1, GATE_PAD)
    wrh = wr.astype(BF16)
    wrl = (wr - wrh.astype(F32)).astype(BF16)
    ex = _branch_expanders()
    full = lambda shape: pl.BlockSpec(shape, lambda i: (0,) * len(shape))
    tile = lambda w, cb=0: pl.BlockSpec((tm, w), lambda i: (i, cb))
    return pl.pallas_call(
        _out_body,
        grid=(n // tm,),
        in_specs=[tile(GM_WIDTH), tile(NSA_WIDTH), tile(NSA_WIDTH), tile(NSA_WIDTH),
                  tile(GATE_PAD, COL_NG // GATE_PAD), tile(HG_WIDTH), tile(HG_WIDTH, COL_HG // HG_WIDTH), tile(d),
                  full((1, GATE_PAD)), full(ex.shape), full((NSA_WIDTH, NSA_WIDTH)), full((HG_WIDTH, HG_WIDTH)),
                  full((1, d)), full((d, d)), full((1, d)), full((1, d)),
                  full((d, ROUTER_PAD)), full((d, ROUTER_PAD)), full((1, ROUTER_PAD))],
        out_specs=[tile(d), tile(ROUTER_PAD)],
        out_shape=[jax.ShapeDtypeStruct((n, d), F32), jax.ShapeDtypeStruct((n, ROUTER_PAD), F32)],
        compiler_params=_params("parallel"),
        name="outproj_ln_router",
    )(ya, ocmp, oslc, owin, h, yc, h, x2, gb, ex, _group_mean_matrix(NSA_WIDTH), _group_mean_matrix(HG_WIDTH),
      row(out_gain), w_out.astype(BF16), row(ln_g), row(ln_b), wrh, wrl, br)


def _moe_body(x_ref, gate_ref, wg_ref, wu_ref, wd_ref, lg_ref, lb_ref, o_ref, xb_sc, acc_sc):
    e = pl.program_id(1)

    @pl.when(e == 0)
    def _():
        xb_sc[...] = x_ref[...].astype(BF16)
        acc_sc[...] = jnp.zeros_like(acc_sc)

    xb = xb_sc[...]
    hg = _dot(xb, wg_ref[0])
    hu = _dot(xb, wu_ref[0])
    hdn = (hg * _sigmoid(hg) * hu).astype(BF16)
    y = _dot(hdn, wd_ref[0])
    lane = lax.broadcasted_iota(jnp.int32, gate_ref.shape, 1)
    ge = jnp.sum(jnp.where(lane == e + EXPERT_LANE0, gate_ref[...], 0.0), axis=-1, keepdims=True)
    acc_sc[...] += ge * y

    @pl.when(e == pl.num_programs(1) - 1)
    def _():
        o_ref[...] = _layer_norm(DEEPNORM_ALPHA * x_ref[...] + acc_sc[...], lg_ref[...], lb_ref[...])


def _moe(x1, gate, wg, wu, wd, ln_g, ln_b):
    n, d = x1.shape
    ne, _, de = wg.shape
    tm = 512
    row = lambda a: a.reshape(1, -1)
    return pl.pallas_call(
        _moe_body,
        grid=(n // tm, ne),
        in_specs=[pl.BlockSpec((tm, d), lambda i, e: (i, 0)),
                  pl.BlockSpec((tm, ROUTER_PAD), lambda i, e: (i, 0)),
                  pl.BlockSpec((1, d, de), lambda i, e: (e, 0, 0)),
                  pl.BlockSpec((1, d, de), lambda i, e: (e, 0, 0)),
                  pl.BlockSpec((1, de, d), lambda i, e: (e, 0, 0)),
                  pl.BlockSpec((1, d), lambda i, e: (0, 0)),
                  pl.BlockSpec((1, d), lambda i, e: (0, 0))],
        out_specs=pl.BlockSpec((tm, d), lambda i, e: (i, 0)),
        out_shape=jax.ShapeDtypeStruct((n, d), F32),
        scratch_shapes=[pltpu.VMEM((tm, d), BF16), pltpu.VMEM((tm, d), F32)],
        compiler_params=_params("parallel", "arbitrary"),
        name="moe_ln",
    )(x1, gate, wg, wu, wd, row(ln_g), row(ln_b))


def _reorder_w_in(w_in):
    pts = np.cumsum((GM_WIDTH, GM_WIDTH, NSA_WIDTH) + (NSA_KV_WIDTH,) * 6 + (N_BRANCH * NSA_HEADS,)
                    + (HG_WIDTH,) * 4)
    ng0, ng1 = int(pts[8]), int(pts[9])
    pad = jnp.zeros((w_in.shape[0], GATE_PAD - (ng1 - ng0)), w_in.dtype)
    return jnp.concatenate([w_in[:, :ng0], w_in[:, ng1:], w_in[:, ng0:ng1], pad], axis=1).astype(BF16)


def _kv_heads(h, col, batch):
    t = h.shape[0] // batch
    a = h[:, col:col + NSA_KV_WIDTH].reshape(batch, t, NSA_KV_HEADS, HEAD_DIM)
    return a.transpose(0, 2, 1, 3)


def _layer(x2, batch, w_in, gm_v_gain, gm_v_bias, gm_w_s, gm_b_s, cmp_pos, cmp_w1, cmp_w2, nsa_gate_b, lb,
           out_gain, w_out, ln1_g, ln1_b, wr, br, wg, wu, wd, ln2_g, ln2_b):
    n = x2.shape[0]
    t = n // batch
    h = _inproj(x2, _reorder_w_in(w_in))

    ya = _gmlp(h, gm_v_gain, gm_v_bias, gm_w_s, gm_b_s, out_gain[:GM_WIDTH])
    yc = _hgrn(h, lb, batch)

    kvc = h[:, COL_KC:COL_KC + 2 * NSA_KV_WIDTH].reshape(batch, t, 2, NSA_KV_HEADS, HEAD_DIM)
    kvc = kvc.transpose(0, 2, 3, 1, 4).reshape(batch, 2, NSA_KV_HEADS, t // CMP_STRIDE, CMP_STRIDE * HEAD_DIM)
    w2d = jnp.concatenate([cmp_w2, cmp_w2], axis=-1)
    kvcmp = _compress(kvc, cmp_w1, w2d, cmp_pos.reshape(2, 1, CMP_BLOCK * HEAD_DIM))
    ocmp, bias = _select(h, kvcmp, batch)

    dup = lambda a: jnp.concatenate([a, a], axis=-1).astype(BF16)
    ks = _kv_heads(h, COL_KS, batch)
    blk_onehot = jnp.asarray((np.arange(t)[:, None] // SLC_BLOCK == np.arange(t // SLC_BLOCK)[None, :])
                             .astype(np.float32), dtype=BF16)
    kaug = jnp.concatenate([dup(ks), jnp.broadcast_to(blk_onehot, ks.shape[:2] + blk_onehot.shape)], axis=-1)
    oslc, owin = _attention(h, bias, kaug, dup(_kv_heads(h, COL_VS, batch)), dup(_kv_heads(h, COL_KW, batch)),
                            dup(_kv_heads(h, COL_VW, batch)), batch)

    x1, gate = _outproj(ya, ocmp, oslc, owin, h, yc, x2, nsa_gate_b, out_gain, w_out, ln1_g, ln1_b, wr, br)
    return _moe(x1, gate, wg, wu, wd, ln2_g, ln2_b)


def kernel(x, w_in, gm_v_gain, gm_v_bias, gm_w_s, gm_b_s, cmp_pos, cmp_w1, cmp_w2, nsa_gate_b, hg_lower, out_gain, w_out, ln1_g, ln1_b, router_group_w, router_group_b, router_expert_w, router_expert_b, exp_w_gate, exp_w_up, exp_w_down, ln2_g, ln2_b):
    batch, t, d = x.shape
    depth = w_in.shape[0]
    lb_all = jnp.cumsum(jax.nn.softmax(hg_lower.astype(F32), axis=0), axis=0)
    lb_all = lb_all - lb_all[0]
    x2 = x.reshape(batch * t, d)
    for l in range(depth):
        pad = ROUTER_PAD - N_GROUPS - N_EXPERTS
        wr = jnp.concatenate([router_group_w[l], router_expert_w[l], jnp.zeros((d, pad), F32)], axis=1)
        br = jnp.concatenate([router_group_b[l], router_expert_b[l], jnp.zeros((pad,), F32)]).reshape(1, ROUTER_PAD)
        x2 = _layer(x2, batch, w_in[l], gm_v_gain[l], gm_v_bias[l], gm_w_s[l], gm_b_s[l], cmp_pos[l], cmp_w1[l],
                    cmp_w2[l], nsa_gate_b[l], lb_all[l], out_gain[l], w_out[l], ln1_g[l], ln1_b[l], wr, br,
                    exp_w_gate[l].astype(BF16), exp_w_up[l].astype(BF16), exp_w_down[l].astype(BF16),
                    ln2_g[l], ln2_b[l])
    return x2.reshape(batch, t, d)
```

```python
import functools

import numpy as np
import jax
import jax.numpy as jnp
from jax import lax
from jax.experimental import pallas as pl
from jax.experimental.pallas import tpu as pltpu

F32 = jnp.float32
BF16 = jnp.bfloat16

D_MODEL = 1024
HEAD_DIM = 64
GM_GROUPS = 4
GM_WIDTH = GM_GROUPS * HEAD_DIM
GM_CHUNK = 128
NSA_HEADS = 8
NSA_KV_HEADS = 2
NSA_GROUP = NSA_HEADS // NSA_KV_HEADS
NSA_WIDTH = NSA_HEADS * HEAD_DIM
NSA_KV_WIDTH = NSA_KV_HEADS * HEAD_DIM
CMP_BLOCK = 32
CMP_STRIDE = 16
SLC_BLOCK = 64
N_SLC = 16
WINDOW = 512
Q_BLOCK = 128
N_BRANCH = 3
HG_HEADS = 4
HG_WIDTH = HG_HEADS * HEAD_DIM
HG_CHUNK = 64
N_GROUPS = 4
EXPERTS_PER_GROUP = 4
N_EXPERTS = N_GROUPS * EXPERTS_PER_GROUP
D_EXPERT = 512
DEPTH = 2
DEEPNORM_ALPHA = (2.0 * DEPTH) ** 0.25
LN_EPS = 1e-5
RMS_EPS = 1e-6
NEG_INF = -1e30
FORCE_SELECT = 1e4
TAKEN = -3e38
ATTN_SCALE = HEAD_DIM ** -0.5

LANES = 128
GATE_PAD = LANES
ROUTER_PAD = LANES
EXPERT_LANE0 = N_GROUPS
VMEM_LIMIT = 56 * 1024 * 1024

COL_GU = 0
COL_GV = 256
COL_NQ = 512
COL_KC = 1024
COL_KS = 1280
COL_VS = 1408
COL_KW = 1536
COL_VW = 1664
COL_HQ = 1792
COL_HF = 2048
COL_HI = 2304
COL_HG = 2560
COL_NG = 2816
H_COLS = COL_NG + GATE_PAD

ATTN_TK = 512
V_ROWS = HEAD_DIM + 16
LOG2E = 1.4426950408889634


def _dot(a, b):
    return jnp.dot(a, b, preferred_element_type=F32)


def _dot_nt(a, b):
    return lax.dot_general(a, b, (((1,), (1,)), ((), ())), preferred_element_type=F32)


def _dot_tn(a, b):
    return lax.dot_general(a, b, (((0,), (0,)), ((), ())), preferred_element_type=F32)


def _split2(x):
    hi = x.astype(BF16)
    lo = (x - hi.astype(F32)).astype(BF16)
    return hi, lo


def _split3(x):
    hi = x.astype(BF16)
    r = x - hi.astype(F32)
    mid = r.astype(BF16)
    lo = (r - mid.astype(F32)).astype(BF16)
    return hi, mid, lo


def _dot2(x, m):
    hi, lo = _split2(x)
    return _dot(hi, m) + _dot(lo, m)


def _gelu(x):
    return 0.5 * x * (1.0 + jnp.tanh(0.7978845608028654 * (x + 0.044715 * (x * x * x))))


def _sigmoid(x):
    return 1.0 / (1.0 + jnp.exp(-x))


def _log_sigmoid(x):
    return jnp.minimum(x, 0.0) - jnp.log1p(jnp.exp(-jnp.abs(x)))


def _layer_norm(z, g, b):
    mu = jnp.mean(z, axis=-1, keepdims=True)
    zc = z - mu
    var = jnp.mean(zc * zc, axis=-1, keepdims=True)
    return zc * lax.rsqrt(var + LN_EPS) * g + b


def _params(*sem):
    return pltpu.CompilerParams(dimension_semantics=sem, vmem_limit_bytes=VMEM_LIMIT)


def _inproj_body(x_ref, w_ref, o_ref):
    o_ref[...] = _dot(x_ref[...].astype(BF16), w_ref[...])


def _inproj(x2, w):
    n, d = x2.shape
    c = w.shape[1]
    tm = 256
    return pl.pallas_call(
        _inproj_body,
        grid=(n // tm,),
        in_specs=[pl.BlockSpec((tm, d), lambda i: (i, 0)), pl.BlockSpec((d, c), lambda i: (0, 0))],
        out_specs=pl.BlockSpec((tm, c), lambda i: (i, 0)),
        out_shape=jax.ShapeDtypeStruct((n, c), F32),
        compiler_params=_params("parallel"),
        name="inproj",
    )(x2, w)


def _group_mean_matrix(width):
    g = np.arange(width) // HEAD_DIM
    return jnp.asarray((g[:, None] == g[None, :]).astype(np.float32) / HEAD_DIM, dtype=BF16)


def _gmlp_body(gu_ref, gv_ref, gain_ref, bias_ref, ws_ref, bsx_ref, gm_ref, og_ref, o_ref, *, chunks):
    gm = gm_ref[...]
    row = lax.broadcasted_iota(jnp.int32, (GM_CHUNK, GM_CHUNK), 0)
    col = lax.broadcasted_iota(jnp.int32, (GM_CHUNK, GM_CHUNK), 1)
    tril = row >= col
    lane_grp = lax.shift_right_logical(lax.broadcasted_iota(jnp.int32, (GM_CHUNK, GM_WIDTH), 1), 6)
    ws = [jnp.where(tril, ws_ref[g], 0.0).astype(BF16) for g in range(GM_GROUPS)]
    for c in range(chunks):
        sl = slice(c * GM_CHUNK, (c + 1) * GM_CHUNK)
        u = _gelu(gu_ref[sl, :])
        v = _gelu(gv_ref[sl, :])
        mu = _dot2(v, gm)
        vc = v - mu
        var = _dot2(vc * vc, gm)
        vn = (vc * lax.rsqrt(var + LN_EPS) * gain_ref[...] + bias_ref[...]).astype(BF16)
        z = bsx_ref[...]
        for g in range(GM_GROUPS):
            z = z + jnp.where(lane_grp == g, _dot(ws[g], vn), 0.0)
        y = u * z
        ms = _dot2(y * y, gm)
        o_ref[sl, :] = y * lax.rsqrt(ms + RMS_EPS) * og_ref[...]


def _gmlp(h, gain, bias, w_s, b_s, out_gain_a):
    n = h.shape[0]
    chunks = 4
    tm = chunks * GM_CHUNK
    bsx = jnp.repeat(b_s.T, HEAD_DIM, axis=1)
    row = lambda a: a.reshape(1, -1)
    full = lambda shape: pl.BlockSpec(shape, lambda i: (0,) * len(shape))
    return pl.pallas_call(
        functools.partial(_gmlp_body, chunks=chunks),
        grid=(n // tm,),
        in_specs=[
            pl.BlockSpec((tm, GM_WIDTH), lambda i: (i, COL_GU // GM_WIDTH)),
            pl.BlockSpec((tm, GM_WIDTH), lambda i: (i, COL_GV // GM_WIDTH)),
            full((1, GM_WIDTH)), full((1, GM_WIDTH)),
            full((GM_GROUPS, GM_CHUNK, GM_CHUNK)), full((GM_CHUNK, GM_WIDTH)),
            full((GM_WIDTH, GM_WIDTH)), full((1, GM_WIDTH)),
        ],
        out_specs=pl.BlockSpec((tm, GM_WIDTH), lambda i: (i, 0)),
        out_shape=jax.ShapeDtypeStruct((n, GM_WIDTH), F32),
        compiler_params=_params("parallel"),
        name="gmlp",
    )(h, h, row(gain), row(bias), w_s, bsx, _group_mean_matrix(GM_WIDTH), row(out_gain_a))


N_LEVELS = 6


def _hgrn_constants():
    c = HG_CHUNK
    t = np.arange(c)[:, None]
    u = np.arange(c)[None, :]
    mats, masks = [], []
    m = c // 2
    while m >= 1:
        p = (t // (2 * m)) * (2 * m) + m - 1
        mats.append(np.where(t > p, (u > p) & (u <= t), (u > t) & (u <= p)))
        masks.append(((t // (2 * m)) == (u // (2 * m))) & ((t % (2 * m)) >= m) & ((u % (2 * m)) < m))
        m //= 2
    mats.append(u <= t)
    mats.append(u > t)
    masks.append(np.eye(c, dtype=bool))
    a_all = np.concatenate(mats, 0).astype(np.float32)
    mask2 = np.stack([np.concatenate([mk, mk], 0) for mk in masks]).astype(np.float32)
    return jnp.asarray(a_all, dtype=BF16), jnp.asarray(mask2)


def _hgrn_body(q_ref, f_ref, i_ref, loglb_ref, log1m_ref, oml_ref, a_ref, mk_ref, o_ref, st_ref, *, chunks):
    @pl.when(pl.program_id(1) == 0)
    def _():
        st_ref[...] = jnp.zeros_like(st_ref)

    c = HG_CHUNK
    a_all = a_ref[...]
    lane = lax.broadcasted_iota(jnp.int32, (c, LANES), 1)
    lo_half = lane < HEAD_DIM
    r2 = lax.broadcasted_iota(jnp.int32, (LANES, LANES), 0)
    c2 = lax.broadcasted_iota(jnp.int32, (LANES, LANES), 1)
    same_head = (r2 < HEAD_DIM) == (c2 < HEAD_DIM)
    for ch in range(chunks):
        sl = slice(ch * c, (ch + 1) * c)
        q = q_ref[sl, :]
        fl = f_ref[sl, :]
        iv = i_ref[sl, :]
        la = loglb_ref[...]
        lb2 = log1m_ref[...] + _log_sigmoid(fl)
        lf = jnp.maximum(la, lb2) + jnp.log1p(jnp.exp(-jnp.abs(la - lb2)))
        k = oml_ref[...] * (1.0 / (1.0 + jnp.exp(fl)))
        hi, mid, lo = _split3(lf)
        d_all = _dot(a_all, hi) + _dot(a_all, mid) + _dot(a_all, lo)
        e_all = jnp.exp(d_all)
        for p in range(HG_HEADS // 2):
            ls = slice(p * LANES, (p + 1) * LANES)
            qp, kp, ip = q[:, ls], k[:, ls], iv[:, ls]
            att = jnp.zeros((2 * c, c), F32)
            for lv in range(N_LEVELS + 1):
                if lv < N_LEVELS:
                    el = e_all[lv * c:(lv + 1) * c, ls]
                    ql, kl = qp * el, kp * el
                else:
                    ql, kl = qp, kp
                qstack = jnp.concatenate([jnp.where(lo_half, ql, 0.0), jnp.where(lo_half, 0.0, ql)], axis=0)
                att = att + _dot_nt(qstack.astype(BF16), kl.astype(BF16)) * mk_ref[lv]
            ipb = ip.astype(BF16)
            o_intra = jnp.where(lo_half, _dot(att[:c].astype(BF16), ipb), _dot(att[c:].astype(BF16), ipb))
            eb = e_all[N_LEVELS * c:(N_LEVELS + 1) * c, ls]
            ebl = e_all[(N_LEVELS + 1) * c:(N_LEVELS + 2) * c, ls]
            st = st_ref[p]
            o_inter = _dot_nt((qp * eb).astype(BF16), st.astype(BF16))
            o_ref[sl, ls] = o_intra + o_inter
            upd = _dot_tn(ipb, (kp * ebl).astype(BF16))
            st_ref[p] = st * eb[c - 1:c, :] + jnp.where(same_head, upd, 0.0)


def _hgrn(h, lb, batch):
    n = h.shape[0]
    t = n // batch
    chunks = 4
    tm = chunks * HG_CHUNK
    steps = t // tm
    a_all, mask2 = _hgrn_constants()
    lb = lb.reshape(1, HG_WIDTH).astype(F32)
    full = lambda shape: pl.BlockSpec(shape, lambda b, i: (0,) * len(shape))
    col = lambda off: pl.BlockSpec((tm, HG_WIDTH), lambda b, i: (b * steps + i, off // HG_WIDTH))
    return pl.pallas_call(
        functools.partial(_hgrn_body, chunks=chunks),
        grid=(batch, steps),
        in_specs=[col(COL_HQ), col(COL_HF), col(COL_HI),
                  full((1, HG_WIDTH)), full((1, HG_WIDTH)), full((1, HG_WIDTH)),
                  full(a_all.shape), full(mask2.shape)],
        out_specs=pl.BlockSpec((tm, HG_WIDTH), lambda b, i: (b * steps + i, 0)),
        out_shape=jax.ShapeDtypeStruct((n, HG_WIDTH), F32),
        scratch_shapes=[pltpu.VMEM((HG_HEADS // 2, LANES, LANES), F32)],
        compiler_params=_params("parallel", "arbitrary"),
        name="hgrn2",
    )(h, h, h, jnp.log(lb), jnp.log1p(-lb), 1.0 - lb, a_all, mask2)


def _compress_body(x_ref, w1_ref, w2_ref, pos_ref, o_ref):
    x = x_ref[0, 0, 0].astype(BF16)
    w1 = w1_ref[0].astype(BF16)
    half = w1.shape[0] // 2
    a = _dot(x, w1[:half])
    b = _dot(x, w1[half:])
    pos = jnp.broadcast_to(pos_ref[0], (8, pos_ref.shape[-1])).astype(BF16)
    pw = _dot(pos, w1)[0:1]
    nrow = x.shape[0]
    b_next = jnp.concatenate([b[1:], b[:1]], axis=0)
    hid = _gelu(a + b_next + pw)
    out = _dot(hid.astype(BF16), w2_ref[0].astype(BF16))
    rid = lax.broadcasted_iota(jnp.int32, out.shape, 0)
    o_ref[0, 0, 0] = jnp.where(rid < nrow - 1, out, 0.0)


def _compress(xkv, w1, w2d, pos):
    b, _, hk, nr, wd = xkv.shape
    return pl.pallas_call(
        _compress_body,
        grid=(b, 2, hk),
        in_specs=[
            pl.BlockSpec((1, 1, 1, nr, wd), lambda i, j, k: (i, j, k, 0, 0)),
            pl.BlockSpec((1,) + w1.shape[1:], lambda i, j, k: (j, 0, 0)),
            pl.BlockSpec((1,) + w2d.shape[1:], lambda i, j, k: (j, 0, 0)),
            pl.BlockSpec((1, 1, pos.shape[-1]), lambda i, j, k: (j, 0, 0)),
        ],
        out_specs=pl.BlockSpec((1, 1, 1, nr, w2d.shape[-1]), lambda i, j, k: (i, j, k, 0, 0)),
        out_shape=jax.ShapeDtypeStruct((b, 2, hk, nr, w2d.shape[-1]), F32),
        compiler_params=_params("parallel", "parallel", "parallel"),
        name="nsa_compress",
    )(xkv, w1, w2d, pos)


def _select_body(q_ref, kvc_ref, ovt_ref, eye_ref, ocmp_ref, bias_ref, *, n_sel):
    c = pl.program_id(1)
    ncmp = kvc_ref.shape[-2]
    nblk = ovt_ref.shape[0]
    q = q_ref[...] * ATTN_SCALE
    lane = lax.broadcasted_iota(jnp.int32, (Q_BLOCK, LANES), 1)
    lo_half = lane < HEAD_DIM
    t_col = c * Q_BLOCK + lax.broadcasted_iota(jnp.int32, (Q_BLOCK, 1), 0)
    n_idx = lax.broadcasted_iota(jnp.int32, (1, ncmp), 1)
    valid = (n_idx * CMP_STRIDE + (CMP_BLOCK - 1)) <= t_col
    jb = lax.broadcasted_iota(jnp.int32, (nblk, 1), 0)
    t_row = c * Q_BLOCK + lax.broadcasted_iota(jnp.int32, (1, Q_BLOCK), 1)
    cur = lax.shift_right_logical(t_row, 6)
    forced = (jb == 0) | (jb == cur) | (jb == cur - 1)
    causal = jb * SLC_BLOCK <= t_row
    for hk in range(NSA_KV_HEADS):
        kc = kvc_ref[0, 0, hk].astype(BF16)
        vc = kvc_ref[0, 1, hk].astype(BF16)
        psum = jnp.zeros((Q_BLOCK, ncmp), F32)
        outs = []
        for g in range(NSA_GROUP):
            hq = hk * NSA_GROUP + g
            pair = q[:, (hq // 2) * LANES:(hq // 2 + 1) * LANES]
            qh = jnp.where(lo_half if hq % 2 == 0 else jnp.logical_not(lo_half), pair, 0.0).astype(BF16)
            s = _dot_nt(qh, kc)
            m = jnp.max(jnp.where(valid, s, NEG_INF), axis=-1, keepdims=True)
            e = jnp.where(valid, jnp.exp(s - m), 0.0)
            l = jnp.sum(e, axis=-1, keepdims=True)
            p = jnp.where(l > 0.0, e / l, 0.0)
            outs.append(_dot(p.astype(BF16), vc))
            psum = psum + p
        for j in range(NSA_GROUP // 2):
            col0 = (hk * (NSA_GROUP // 2) + j) * LANES
            ocmp_ref[:, col0:col0 + LANES] = jnp.where(lo_half, outs[2 * j], outs[2 * j + 1])
        ph, plo = _split2(psum)
        imp = _dot_nt(ovt_ref[...], ph) + _dot_nt(ovt_ref[...], plo)
        val = jnp.where(causal, imp + jnp.where(forced, FORCE_SELECT, 0.0), NEG_INF)
        sel = jnp.zeros((nblk, Q_BLOCK), F32)
        for _ in range(n_sel):
            m = jnp.max(val, axis=0, keepdims=True)
            idx = jnp.min(jnp.where(val == m, jb, nblk), axis=0, keepdims=True)
            hit = jb == idx
            sel = jnp.where(hit & (m > 0.5 * NEG_INF), 1.0, sel)
            val = jnp.where(hit, TAKEN, val)
        sel_q = _dot_nt(eye_ref[...], sel.astype(BF16))
        bias_ref[0, hk] = jnp.where(sel_q > 0.5, 0.0, NEG_INF).astype(BF16)


def _select(h, kvcmp, batch):
    n = h.shape[0]
    t = n // batch
    nq = t // Q_BLOCK
    ncmp = kvcmp.shape[-2]
    nblk = t // SLC_BLOCK
    ii = np.arange(ncmp)[None, :]
    jj = np.arange(nblk)[:, None]
    ovt = ((ii * CMP_STRIDE < (jj + 1) * SLC_BLOCK) & (ii * CMP_STRIDE + CMP_BLOCK > jj * SLC_BLOCK)
           & (ii < ncmp - 1))
    ovt = jnp.asarray(ovt.astype(np.float32), dtype=BF16)
    eye = jnp.asarray(np.eye(Q_BLOCK, dtype=np.float32), dtype=BF16)
    full = lambda shape: pl.BlockSpec(shape, lambda b, i: (0,) * len(shape))
    kv_spec = pl.BlockSpec((1,) + kvcmp.shape[1:], lambda b, i: (b, 0, 0, 0, 0))
    return pl.pallas_call(
        functools.partial(_select_body, n_sel=min(N_SLC, nblk)),
        grid=(batch, nq),
        in_specs=[pl.BlockSpec((Q_BLOCK, NSA_WIDTH), lambda b, i: (b * nq + i, COL_NQ // NSA_WIDTH)),
                  kv_spec, full(ovt.shape), full(eye.shape)],
        out_specs=[pl.BlockSpec((Q_BLOCK, NSA_WIDTH), lambda b, i: (b * nq + i, 0)),
                   pl.BlockSpec((1, NSA_KV_HEADS, Q_BLOCK, nblk), lambda b, i: (b, 0, i, 0))],
        out_shape=[jax.ShapeDtypeStruct((n, NSA_WIDTH), F32),
                   jax.ShapeDtypeStruct((batch, NSA_KV_HEADS, t, nblk), BF16)],
        compiler_params=_params("parallel", "parallel"),
        name="nsa_select",
    )(h, kvcmp, ovt, eye)


def _softmax_tile_t(s, m_prev):
    m_new = jnp.maximum(m_prev, jnp.max(s, axis=0, keepdims=True))
    return m_new, jnp.exp2(s - m_new).astype(BF16)


def _emit_heads(acc, pe_ref, po_ref, out_ref):
    o_t = acc[0:HEAD_DIM] / acc[HEAD_DIM:HEAD_DIM + 1]
    hi, lo = _split2(o_t)
    for j in range(NSA_GROUP // 2):
        a = slice(2 * j * Q_BLOCK, (2 * j + 1) * Q_BLOCK)
        b = slice((2 * j + 1) * Q_BLOCK, (2 * j + 2) * Q_BLOCK)
        out_ref[:, j * LANES:(j + 1) * LANES] = (
            _dot_tn(hi[:, a], pe_ref[...]) + _dot_tn(lo[:, a], pe_ref[...])
            + _dot_tn(hi[:, b], po_ref[...]) + _dot_tn(lo[:, b], po_ref[...]))


def _attn_body(q_ref, bias_ref, kaug_ref, vst_ref, kw_ref, vwt_ref, pe_ref, po_ref, oslc_ref, owin_ref,
               qaug_sc, acc_sc, sa_sc, sb_sc):
    c = pl.program_id(2)
    rows = NSA_GROUP * Q_BLOCK
    q4 = q_ref[...] * (ATTN_SCALE * LOG2E)
    lane = lax.broadcasted_iota(jnp.int32, (Q_BLOCK, LANES), 1)
    lo_half = lane < HEAD_DIM
    parts = []
    for g in range(NSA_GROUP):
        pair = q4[:, (g // 2) * LANES:(g // 2 + 1) * LANES]
        parts.append(jnp.where(lo_half if g % 2 == 0 else jnp.logical_not(lo_half), pair, 0.0))
    qpart = jnp.concatenate(parts, axis=0).astype(BF16)
    qaug_sc[:, 0:LANES] = qpart
    bias = bias_ref[0, 0]
    for g in range(NSA_GROUP):
        qaug_sc[g * Q_BLOCK:(g + 1) * Q_BLOCK, LANES:] = bias
    t_row = c * Q_BLOCK + (lax.broadcasted_iota(jnp.int32, (1, rows), 1) & (Q_BLOCK - 1))

    acc_sc[...] = jnp.zeros(acc_sc.shape, F32)
    last_tile = kaug_ref.shape[2] // ATTN_TK - 1

    def scores(tile, s_ref):
        start = pl.multiple_of(jnp.minimum(tile, last_tile) * ATTN_TK, ATTN_TK)
        s = _dot_nt(kaug_ref[0, 0, pl.ds(start, ATTN_TK), :], qaug_sc[...])
        kpos = tile * ATTN_TK + lax.broadcasted_iota(jnp.int32, (ATTN_TK, 1), 0)
        s = jnp.where(kpos <= t_row, s, NEG_INF)
        s_ref[...] = s
        return jnp.max(s, axis=0, keepdims=True)

    def accumulate(tile, m_prev, m_tile, s_ref):
        m_new = jnp.maximum(m_prev, m_tile)
        p = jnp.exp2(s_ref[...] - m_new).astype(BF16)
        acc_sc[...] = (jnp.exp2(m_prev - m_new) * acc_sc[...]
                       + _dot(vst_ref[0, 0, jnp.minimum(tile, last_tile)], p))
        return m_new

    def body(i, carry):
        m, mt_a = carry
        mt_b = scores(2 * i + 1, sb_sc)
        m = accumulate(2 * i, m, mt_a, sa_sc)
        mt_a = scores(2 * i + 2, sa_sc)
        m = accumulate(2 * i + 1, m, mt_b, sb_sc)
        return m, mt_a

    n_full = (c * Q_BLOCK) // ATTN_TK
    carry = (jnp.full((1, rows), TAKEN, F32), scores(0, sa_sc))
    m, mt_a = lax.fori_loop(0, (n_full + 1) // 2, body, carry)
    accumulate(2 * ((n_full + 1) // 2), m, mt_a, sa_sc)
    _emit_heads(acc_sc[...], pe_ref, po_ref, oslc_ref)

    span = WINDOW + Q_BLOCK
    wblk = jnp.maximum(c - WINDOW // Q_BLOCK, 0)
    wstart = pl.multiple_of(wblk * Q_BLOCK, Q_BLOCK)
    s = _dot_nt(kw_ref[0, 0, pl.ds(wstart, span), :], qpart)
    kpos = wstart + lax.broadcasted_iota(jnp.int32, (span, 1), 0)
    ok = (kpos <= t_row) & (kpos > t_row - WINDOW)
    _, p = _softmax_tile_t(jnp.where(ok, s, NEG_INF), jnp.full((1, rows), TAKEN, F32))
    acc = _dot(vwt_ref[0, 0, wblk], p[0:Q_BLOCK])
    for i in range(1, span // Q_BLOCK):
        acc = acc + _dot(vwt_ref[0, 0, wblk + i], p[i * Q_BLOCK:(i + 1) * Q_BLOCK])
    _emit_heads(acc, pe_ref, po_ref, owin_ref)


def _values_t(v, tile):
    b, hk, t, dh = v.shape
    vt = jnp.concatenate([v.transpose(0, 1, 3, 2), jnp.ones((b, hk, 1, t), v.dtype),
                          jnp.zeros((b, hk, V_ROWS - dh - 1, t), v.dtype)], axis=2)
    return vt.reshape(b, hk, V_ROWS, t // tile, tile).transpose(0, 1, 3, 2, 4).astype(BF16)


def _attention(h, bias, kaug, vs, kw, vw, batch):
    n = h.shape[0]
    t = n // batch
    nq = t // Q_BLOCK
    gw = NSA_GROUP * HEAD_DIM
    rows = NSA_GROUP * Q_BLOCK
    vst = _values_t(vs, ATTN_TK)
    vwt = _values_t(vw, Q_BLOCK)
    eye = np.eye(HEAD_DIM, dtype=np.float32)
    zero = np.zeros_like(eye)
    pe = jnp.asarray(np.concatenate([eye, zero], axis=1), dtype=BF16)
    po = jnp.asarray(np.concatenate([zero, eye], axis=1), dtype=BF16)
    res = lambda a: pl.BlockSpec((1, 1) + a.shape[2:], lambda b, k, i: (b, k) + (0,) * (a.ndim - 2))
    full = lambda a: pl.BlockSpec(a.shape, lambda b, k, i: (0,) * a.ndim)
    out_spec = pl.BlockSpec((Q_BLOCK, gw), lambda b, k, i: (b * nq + i, k))
    return pl.pallas_call(
        _attn_body,
        grid=(batch, NSA_KV_HEADS, nq),
        in_specs=[pl.BlockSpec((Q_BLOCK, gw), lambda b, k, i: (b * nq + i, COL_NQ // gw + k)),
                  pl.BlockSpec((1, 1, Q_BLOCK, bias.shape[-1]), lambda b, k, i: (b, k, i, 0)),
                  res(kaug), res(vst), res(kw), res(vwt), full(pe), full(po)],
        out_specs=[out_spec, out_spec],
        out_shape=[jax.ShapeDtypeStruct((n, NSA_WIDTH), F32)] * 2,
        scratch_shapes=[pltpu.VMEM((rows, kaug.shape[-1]), BF16), pltpu.VMEM((V_ROWS, rows), F32),
                        pltpu.VMEM((ATTN_TK, rows), F32), pltpu.VMEM((ATTN_TK, rows), F32)],
        compiler_params=_params("parallel", "parallel", "arbitrary"),
        name="nsa_attention",
    )(h, bias, kaug, vst, kw, vwt, pe, po)


def _branch_expanders():
    e = np.zeros((N_BRANCH, GATE_PAD, NSA_WIDTH), np.float32)
    for hq in range(NSA_HEADS):
        for br in range(N_BRANCH):
            e[br, hq * N_BRANCH + br, hq * HEAD_DIM:(hq + 1) * HEAD_DIM] = 1.0
    return jnp.asarray(e, dtype=BF16)


def _route(logits):
    lane = lax.broadcasted_iota(jnp.int32, logits.shape, 1)
    is_g = lane < N_GROUPS
    gl = jnp.where(is_g, logits, -jnp.inf)
    gmax = jnp.max(gl, axis=-1, keepdims=True)
    gsum = jnp.sum(jnp.where(is_g, jnp.exp(logits - gmax), 0.0), axis=-1, keepdims=True)
    g_val = 1.0 / gsum
    g_idx = jnp.min(jnp.where(is_g & (logits == gmax), lane, ROUTER_PAD), axis=-1, keepdims=True)
    e_lo = EXPERT_LANE0 + g_idx * EXPERTS_PER_GROUP
    in_sel = (lane >= e_lo) & (lane < e_lo + EXPERTS_PER_GROUP)
    el = jnp.where(in_sel, logits, -jnp.inf)
    m1 = jnp.max(el, axis=-1, keepdims=True)
    i1 = jnp.min(jnp.where(in_sel & (logits == m1), lane, ROUTER_PAD), axis=-1, keepdims=True)
    el2 = jnp.where(lane == i1, -jnp.inf, el)
    m2 = jnp.max(el2, axis=-1, keepdims=True)
    i2 = jnp.min(jnp.where(el2 == m2, lane, ROUTER_PAD), axis=-1, keepdims=True)
    r = jnp.exp(m2 - m1)
    w1 = 1.0 / (1.0 + r)
    w2 = r * w1
    return g_val * (jnp.where(lane == i1, w1, 0.0) + jnp.where(lane == i2, w2, 0.0))


def _out_body(ya_ref, ocmp_ref, oslc_ref, owin_ref, gl_ref, yc_ref, hg_ref, x_ref, gb_ref, ex_ref,
              gmb_ref, gmc_ref, og_ref, w_ref, lg_ref, lb_ref, wrh_ref, wrl_ref, br_ref, x1_ref, gate_ref):
    g = _sigmoid(gl_ref[...] + gb_ref[...])
    yb = (_dot2(g, ex_ref[0]) * ocmp_ref[...] + _dot2(g, ex_ref[1]) * oslc_ref[...]
          + _dot2(g, ex_ref[2]) * owin_ref[...])
    n_ab = GM_WIDTH + NSA_WIDTH
    ybn = yb * lax.rsqrt(_dot2(yb * yb, gmb_ref[...]) + RMS_EPS) * og_ref[:, GM_WIDTH:n_ab]
    yc = yc_ref[...]
    hg = hg_ref[...]
    ycn = yc * lax.rsqrt(_dot2(yc * yc, gmc_ref[...]) + RMS_EPS) * og_ref[:, n_ab:] * (hg * _sigmoid(hg))
    y = (_dot(ya_ref[...].astype(BF16), w_ref[0:GM_WIDTH, :]) + _dot(ybn.astype(BF16), w_ref[GM_WIDTH:n_ab, :])
         + _dot(ycn.astype(BF16), w_ref[n_ab:, :]))
    x1 = _layer_norm(DEEPNORM_ALPHA * x_ref[...] + y, lg_ref[...], lb_ref[...])
    x1_ref[...] = x1
    xh, xl = _split2(x1)
    logits = _dot(xh, wrh_ref[...]) + _dot(xh, wrl_ref[...]) + _dot(xl, wrh_ref[...]) + br_ref[...]
    gate_ref[...] = _route(logits)


def _outproj(ya, ocmp, oslc, owin, h, yc, x2, gate_b, out_gain, w_out, ln_g, ln_b, wr, br):
    n, d = x2.shape
    tm = 256
    row = lambda a: a.reshape(1, -1)
    gb = jnp.pad(gate_b, (0, GATE_PAD - gate_b.shape[0])).reshape(1, GATE_PAD)
    wrh = wr.astype(BF16)
    wrl = (wr - wrh.astype(F32)).astype(BF16)
    ex = _branch_expanders()
    full = lambda shape: pl.BlockSpec(shape, lambda i: (0,) * len(shape))
    tile = lambda w, cb=0: pl.BlockSpec((tm, w), lambda i: (i, cb))
    return pl.pallas_call(
        _out_body,
        grid=(n // tm,),
        in_specs=[tile(GM_WIDTH), tile(NSA_WIDTH), tile(NSA_WIDTH), tile(NSA_WIDTH),
                  tile(GATE_PAD, COL_NG // GATE_PAD), tile(HG_WIDTH), tile(HG_WIDTH, COL_HG // HG_WIDTH), tile(d),
                  full((1, GATE_PAD)), full(ex.shape), full((NSA_WIDTH, NSA_WIDTH)), full((HG_WIDTH, HG_WIDTH)),
                  full((1, d)), full((d, d)), full((1, d)), full((1, d)),
                  full((d, ROUTER_PAD)), full((d, ROUTER_PAD)), full((1, ROUTER_PAD))],
        out_specs=[tile(d), tile(ROUTER_PAD)],
        out_shape=[jax.ShapeDtypeStruct((n, d), F32), jax.ShapeDtypeStruct((n, ROUTER_PAD), F32)],
        compiler_params=_params("parallel"),
        name="outproj_ln_router",
    )(ya, ocmp, oslc, owin, h, yc, h, x2, gb, ex, _group_mean_matrix(NSA_WIDTH), _group_mean_matrix(HG_WIDTH),
      row(out_gain), w_out.astype(BF16), row(ln_g), row(ln_b), wrh, wrl, br)


def _moe_body(x_ref, gate_ref, wg_ref, wu_ref, wd_ref, lg_ref, lb_ref, o_ref, xb_sc, acc_sc):
    e = pl.program_id(1)

    @pl.when(e == 0)
    def _():
        xb_sc[...] = x_ref[...].astype(BF16)
        acc_sc[...] = jnp.zeros_like(acc_sc)

    xb = xb_sc[...]
    hg = _dot(xb, wg_ref[0])
    hu = _dot(xb, wu_ref[0])
    hdn = (hg * _sigmoid(hg) * hu).astype(BF16)
    y = _dot(hdn, wd_ref[0])
    lane = lax.broadcasted_iota(jnp.int32, gate_ref.shape, 1)
    ge = jnp.sum(jnp.where(lane == e + EXPERT_LANE0, gate_ref[...], 0.0), axis=-1, keepdims=True)
    acc_sc[...] += ge * y

    @pl.when(e == pl.num_programs(1) - 1)
    def _():
        o_ref[...] = _layer_norm(DEEPNORM_ALPHA * x_ref[...] + acc_sc[...], lg_ref[...], lb_ref[...])


def _moe(x1, gate, wg, wu, wd, ln_g, ln_b):
    n, d = x1.shape
    ne, _, de = wg.shape
    tm = 512
    row = lambda a: a.reshape(1, -1)
    return pl.pallas_call(
        _moe_body,
        grid=(n // tm, ne),
        in_specs=[pl.BlockSpec((tm, d), lambda i, e: (i, 0)),
                  pl.BlockSpec((tm, ROUTER_PAD), lambda i, e: (i, 0)),
                  pl.BlockSpec((1, d, de), lambda i, e: (e, 0, 0)),
                  pl.BlockSpec((1, d, de), lambda i, e: (e, 0, 0)),
                  pl.BlockSpec((1, de, d), lambda i, e: (e, 0, 0)),
                  pl.BlockSpec((1, d), lambda i, e: (0, 0)),
                  pl.BlockSpec((1, d), lambda i, e: (0, 0))],
        out_specs=pl.BlockSpec((tm, d), lambda i, e: (i, 0)),
        out_shape=jax.ShapeDtypeStruct((n, d), F32),
        scratch_shapes=[pltpu.VMEM((tm, d), BF16), pltpu.VMEM((tm, d), F32)],
        compiler_params=_params("parallel", "arbitrary"),
        name="moe_ln",
    )(x1, gate, wg, wu, wd, row(ln_g), row(ln_b))


def _reorder_w_in(w_in):
    pts = np.cumsum((GM_WIDTH, GM_WIDTH, NSA_WIDTH) + (NSA_KV_WIDTH,) * 6 + (N_BRANCH * NSA_HEADS,)
                    + (HG_WIDTH,) * 4)
    ng0, ng1 = int(pts[8]), int(pts[9])
    pad = jnp.zeros((w_in.shape[0], GATE_PAD - (ng1 - ng0)), w_in.dtype)
    return jnp.concatenate([w_in[:, :ng0], w_in[:, ng1:], w_in[:, ng0:ng1], pad], axis=1).astype(BF16)


def _kv_heads(h, col, batch):
    t = h.shape[0] // batch
    a = h[:, col:col + NSA_KV_WIDTH].reshape(batch, t, NSA_KV_HEADS, HEAD_DIM)
    return a.transpose(0, 2, 1, 3)


def _layer(x2, batch, w_in, gm_v_gain, gm_v_bias, gm_w_s, gm_b_s, cmp_pos, cmp_w1, cmp_w2, nsa_gate_b, lb,
           out_gain, w_out, ln1_g, ln1_b, wr, br, wg, wu, wd, ln2_g, ln2_b):
    n = x2.shape[0]
    t = n // batch
    h = _inproj(x2, _reorder_w_in(w_in))

    ya = _gmlp(h, gm_v_gain, gm_v_bias, gm_w_s, gm_b_s, out_gain[:GM_WIDTH])
    yc = _hgrn(h, lb, batch)

    kvc = h[:, COL_KC:COL_KC + 2 * NSA_KV_WIDTH].reshape(batch, t, 2, NSA_KV_HEADS, HEAD_DIM)
    kvc = kvc.transpose(0, 2, 3, 1, 4).reshape(batch, 2, NSA_KV_HEADS, t // CMP_STRIDE, CMP_STRIDE * HEAD_DIM)
    w2d = jnp.concatenate([cmp_w2, cmp_w2], axis=-1)
    kvcmp = _compress(kvc, cmp_w1, w2d, cmp_pos.reshape(2, 1, CMP_BLOCK * HEAD_DIM))
    ocmp, bias = _select(h, kvcmp, batch)

    dup = lambda a: jnp.concatenate([a, a], axis=-1).astype(BF16)
    ks = _kv_heads(h, COL_KS, batch)
    blk_onehot = jnp.asarray((np.arange(t)[:, None] // SLC_BLOCK == np.arange(t // SLC_BLOCK)[None, :])
                             .astype(np.float32), dtype=BF16)
    kaug = jnp.concatenate([dup(ks), jnp.broadcast_to(blk_onehot, ks.shape[:2] + blk_onehot.shape)], axis=-1)
    oslc, owin = _attention(h, bias, kaug, _kv_heads(h, COL_VS, batch), dup(_kv_heads(h, COL_KW, batch)),
                            _kv_heads(h, COL_VW, batch), batch)

    x1, gate = _outproj(ya, ocmp, oslc, owin, h, yc, x2, nsa_gate_b, out_gain, w_out, ln1_g, ln1_b, wr, br)
    return _moe(x1, gate, wg, wu, wd, ln2_g, ln2_b)


def kernel(x, w_in, gm_v_gain, gm_v_bias, gm_w_s, gm_b_s, cmp_pos, cmp_w1, cmp_w2, nsa_gate_b, hg_lower, out_gain, w_out, ln1_g, ln1_b, router_group_w, router_group_b, router_expert_w, router_expert_b, exp_w_gate, exp_w_up, exp_w_down, ln2_g, ln2_b):
    batch, t, d = x.shape
    depth = w_in.shape[0]
    lb_all = jnp.cumsum(jax.nn.softmax(hg_lower.astype(F32), axis=0), axis=0)
    lb_all = lb_all - lb_all[0]
    x2 = x.reshape(batch * t, d)
    for l in range(depth):
        pad = ROUTER_PAD - N_GROUPS - N_EXPERTS
        wr = jnp.concatenate([router_group_w[l], router_expert_w[l], jnp.zeros((d, pad), F32)], axis=1)
        br = jnp.concatenate([router_group_b[l], router_expert_b[l], jnp.zeros((pad,), F32)]).reshape(1, ROUTER_PAD)
        x2 = _layer(x2, batch, w_in[l], gm_v_gain[l], gm_v_bias[l], gm_w_s[l], gm_b_s[l], cmp_pos[l], cmp_w1[l],
                    cmp_w2[l], nsa_gate_b[l], lb_all[l], out_gain[l], w_out[l], ln1_g[l], ln1_b[l], wr, br,
                    exp_w_gate[l].astype(BF16), exp_w_up[l].astype(BF16), exp_w_down[l].astype(BF16),
                    ln2_g[l], ln2_b[l])
    return x2.reshape(batch, t, d)
```

```python
import functools

import numpy as np
import jax
import jax.numpy as jnp
from jax import lax
from jax.experimental import pallas as pl
from jax.experimental.pallas import tpu as pltpu

F32 = jnp.float32
BF16 = jnp.bfloat16

D_MODEL = 1024
HEAD_DIM = 64
GM_GROUPS = 4
GM_WIDTH = GM_GROUPS * HEAD_DIM
GM_CHUNK = 128
NSA_HEADS = 8
NSA_KV_HEADS = 2
NSA_GROUP = NSA_HEADS // NSA_KV_HEADS
NSA_WIDTH = NSA_HEADS * HEAD_DIM
NSA_KV_WIDTH = NSA_KV_HEADS * HEAD_DIM
CMP_BLOCK = 32
CMP_STRIDE = 16
SLC_BLOCK = 64
N_SLC = 16
WINDOW = 512
Q_BLOCK = 128
N_BRANCH = 3
HG_HEADS = 4
HG_WIDTH = HG_HEADS * HEAD_DIM
HG_CHUNK = 64
N_GROUPS = 4
EXPERTS_PER_GROUP = 4
N_EXPERTS = N_GROUPS * EXPERTS_PER_GROUP
D_EXPERT = 512
DEPTH = 2
DEEPNORM_ALPHA = (2.0 * DEPTH) ** 0.25
LN_EPS = 1e-5
RMS_EPS = 1e-6
NEG_INF = -1e30
FORCE_SELECT = 1e4
TAKEN = -3e38
ATTN_SCALE = HEAD_DIM ** -0.5

LANES = 128
GATE_PAD = LANES
ROUTER_PAD = LANES
EXPERT_LANE0 = N_GROUPS
VMEM_LIMIT = 56 * 1024 * 1024

COL_GU = 0
COL_GV = 256
COL_NQ = 512
COL_KC = 1024
COL_KS = 1280
COL_VS = 1408
COL_KW = 1536
COL_VW = 1664
COL_HQ = 1792
COL_HF = 2048
COL_HI = 2304
COL_HG = 2560
COL_NG = 2816
H_COLS = COL_NG + GATE_PAD

ATTN_TK = 512
V_ROWS = HEAD_DIM + 16
LOG2E = 1.4426950408889634
MOE_TM = 1024
MOE_CHUNK = 160


def _dot(a, b):
    return jnp.dot(a, b, preferred_element_type=F32)


def _dot_nt(a, b):
    return lax.dot_general(a, b, (((1,), (1,)), ((), ())), preferred_element_type=F32)


def _dot_tn(a, b):
    return lax.dot_general(a, b, (((0,), (0,)), ((), ())), preferred_element_type=F32)


def _split2(x):
    hi = x.astype(BF16)
    lo = (x - hi.astype(F32)).astype(BF16)
    return hi, lo


def _split3(x):
    hi = x.astype(BF16)
    r = x - hi.astype(F32)
    mid = r.astype(BF16)
    lo = (r - mid.astype(F32)).astype(BF16)
    return hi, mid, lo


def _dot2(x, m):
    hi, lo = _split2(x)
    return _dot(hi, m) + _dot(lo, m)


def _gelu(x):
    return 0.5 * x * (1.0 + jnp.tanh(0.7978845608028654 * (x + 0.044715 * (x * x * x))))


def _sigmoid(x):
    return 1.0 / (1.0 + jnp.exp(-x))


def _log_sigmoid(x):
    return jnp.minimum(x, 0.0) - jnp.log1p(jnp.exp(-jnp.abs(x)))


def _layer_norm(z, g, b):
    mu = jnp.mean(z, axis=-1, keepdims=True)
    zc = z - mu
    var = jnp.mean(zc * zc, axis=-1, keepdims=True)
    return zc * lax.rsqrt(var + LN_EPS) * g + b


def _params(*sem):
    return pltpu.CompilerParams(dimension_semantics=sem, vmem_limit_bytes=VMEM_LIMIT)


def _inproj_body(x_ref, w_ref, o_ref):
    o_ref[...] = _dot(x_ref[...].astype(BF16), w_ref[...])


def _inproj(x2, w):
    n, d = x2.shape
    c = w.shape[1]
    tm = 256
    return pl.pallas_call(
        _inproj_body,
        grid=(n // tm,),
        in_specs=[pl.BlockSpec((tm, d), lambda i: (i, 0)), pl.BlockSpec((d, c), lambda i: (0, 0))],
        out_specs=pl.BlockSpec((tm, c), lambda i: (i, 0)),
        out_shape=jax.ShapeDtypeStruct((n, c), F32),
        compiler_params=_params("parallel"),
        name="inproj",
    )(x2, w)


def _group_mean_matrix(width):
    g = np.arange(width) // HEAD_DIM
    return jnp.asarray((g[:, None] == g[None, :]).astype(np.float32) / HEAD_DIM, dtype=BF16)


def _gmlp_body(gu_ref, gv_ref, gain_ref, bias_ref, ws_ref, bsx_ref, gm_ref, og_ref, o_ref, *, chunks):
    gm = gm_ref[...]
    row = lax.broadcasted_iota(jnp.int32, (GM_CHUNK, GM_CHUNK), 0)
    col = lax.broadcasted_iota(jnp.int32, (GM_CHUNK, GM_CHUNK), 1)
    tril = row >= col
    lane_grp = lax.shift_right_logical(lax.broadcasted_iota(jnp.int32, (GM_CHUNK, GM_WIDTH), 1), 6)
    ws = [jnp.where(tril, ws_ref[g], 0.0).astype(BF16) for g in range(GM_GROUPS)]
    for c in range(chunks):
        sl = slice(c * GM_CHUNK, (c + 1) * GM_CHUNK)
        u = _gelu(gu_ref[sl, :])
        v = _gelu(gv_ref[sl, :])
        mu = _dot2(v, gm)
        vc = v - mu
        var = _dot2(vc * vc, gm)
        vn = (vc * lax.rsqrt(var + LN_EPS) * gain_ref[...] + bias_ref[...]).astype(BF16)
        z = bsx_ref[...]
        for g in range(GM_GROUPS):
            z = z + jnp.where(lane_grp == g, _dot(ws[g], vn), 0.0)
        y = u * z
        ms = _dot2(y * y, gm)
        o_ref[sl, :] = y * lax.rsqrt(ms + RMS_EPS) * og_ref[...]


def _gmlp(h, gain, bias, w_s, b_s, out_gain_a):
    n = h.shape[0]
    chunks = 4
    tm = chunks * GM_CHUNK
    bsx = jnp.repeat(b_s.T, HEAD_DIM, axis=1)
    row = lambda a: a.reshape(1, -1)
    full = lambda shape: pl.BlockSpec(shape, lambda i: (0,) * len(shape))
    return pl.pallas_call(
        functools.partial(_gmlp_body, chunks=chunks),
        grid=(n // tm,),
        in_specs=[
            pl.BlockSpec((tm, GM_WIDTH), lambda i: (i, COL_GU // GM_WIDTH)),
            pl.BlockSpec((tm, GM_WIDTH), lambda i: (i, COL_GV // GM_WIDTH)),
            full((1, GM_WIDTH)), full((1, GM_WIDTH)),
            full((GM_GROUPS, GM_CHUNK, GM_CHUNK)), full((GM_CHUNK, GM_WIDTH)),
            full((GM_WIDTH, GM_WIDTH)), full((1, GM_WIDTH)),
        ],
        out_specs=pl.BlockSpec((tm, GM_WIDTH), lambda i: (i, 0)),
        out_shape=jax.ShapeDtypeStruct((n, GM_WIDTH), F32),
        compiler_params=_params("parallel"),
        name="gmlp",
    )(h, h, row(gain), row(bias), w_s, bsx, _group_mean_matrix(GM_WIDTH), row(out_gain_a))


N_LEVELS = 6


def _hgrn_constants():
    c = HG_CHUNK
    t = np.arange(c)[:, None]
    u = np.arange(c)[None, :]
    mats, masks = [], []
    m = c // 2
    while m >= 1:
        p = (t // (2 * m)) * (2 * m) + m - 1
        mats.append(np.where(t > p, (u > p) & (u <= t), (u > t) & (u <= p)))
        masks.append(((t // (2 * m)) == (u // (2 * m))) & ((t % (2 * m)) >= m) & ((u % (2 * m)) < m))
        m //= 2
    mats.append(u <= t)
    mats.append(u > t)
    masks.append(np.eye(c, dtype=bool))
    a_all = np.concatenate(mats, 0).astype(np.float32)
    mask2 = np.stack([np.concatenate([mk, mk], 0) for mk in masks]).astype(np.float32)
    return jnp.asarray(a_all, dtype=BF16), jnp.asarray(mask2)


def _hgrn_body(q_ref, f_ref, i_ref, loglb_ref, log1m_ref, oml_ref, a_ref, mk_ref, o_ref, st_ref, *, chunks):
    @pl.when(pl.program_id(1) == 0)
    def _():
        st_ref[...] = jnp.zeros_like(st_ref)

    c = HG_CHUNK
    a_all = a_ref[...]
    lane = lax.broadcasted_iota(jnp.int32, (c, LANES), 1)
    lo_half = lane < HEAD_DIM
    r2 = lax.broadcasted_iota(jnp.int32, (LANES, LANES), 0)
    c2 = lax.broadcasted_iota(jnp.int32, (LANES, LANES), 1)
    same_head = (r2 < HEAD_DIM) == (c2 < HEAD_DIM)
    for ch in range(chunks):
        sl = slice(ch * c, (ch + 1) * c)
        q = q_ref[sl, :]
        fl = f_ref[sl, :]
        iv = i_ref[sl, :]
        la = loglb_ref[...]
        lb2 = log1m_ref[...] + _log_sigmoid(fl)
        lf = jnp.maximum(la, lb2) + jnp.log1p(jnp.exp(-jnp.abs(la - lb2)))
        k = oml_ref[...] * (1.0 / (1.0 + jnp.exp(fl)))
        hi, mid, lo = _split3(lf)
        d_all = _dot(a_all, hi) + _dot(a_all, mid) + _dot(a_all, lo)
        e_all = jnp.exp(d_all)
        for p in range(HG_HEADS // 2):
            ls = slice(p * LANES, (p + 1) * LANES)
            qp, kp, ip = q[:, ls], k[:, ls], iv[:, ls]
            att = jnp.zeros((2 * c, c), F32)
            for lv in range(N_LEVELS + 1):
                if lv < N_LEVELS:
                    el = e_all[lv * c:(lv + 1) * c, ls]
                    ql, kl = qp * el, kp * el
                else:
                    ql, kl = qp, kp
                qstack = jnp.concatenate([jnp.where(lo_half, ql, 0.0), jnp.where(lo_half, 0.0, ql)], axis=0)
                att = att + _dot_nt(qstack.astype(BF16), kl.astype(BF16)) * mk_ref[lv]
            ipb = ip.astype(BF16)
            o_intra = jnp.where(lo_half, _dot(att[:c].astype(BF16), ipb), _dot(att[c:].astype(BF16), ipb))
            eb = e_all[N_LEVELS * c:(N_LEVELS + 1) * c, ls]
            ebl = e_all[(N_LEVELS + 1) * c:(N_LEVELS + 2) * c, ls]
            st = st_ref[p]
            o_inter = _dot_nt((qp * eb).astype(BF16), st.astype(BF16))
            o_ref[sl, ls] = o_intra + o_inter
            upd = _dot_tn(ipb, (kp * ebl).astype(BF16))
            st_ref[p] = st * eb[c - 1:c, :] + jnp.where(same_head, upd, 0.0)


def _hgrn(h, lb, batch):
    n = h.shape[0]
    t = n // batch
    chunks = 4
    tm = chunks * HG_CHUNK
    steps = t // tm
    a_all, mask2 = _hgrn_constants()
    lb = lb.reshape(1, HG_WIDTH).astype(F32)
    full = lambda shape: pl.BlockSpec(shape, lambda b, i: (0,) * len(shape))
    col = lambda off: pl.BlockSpec((tm, HG_WIDTH), lambda b, i: (b * steps + i, off // HG_WIDTH))
    return pl.pallas_call(
        functools.partial(_hgrn_body, chunks=chunks),
        grid=(batch, steps),
        in_specs=[col(COL_HQ), col(COL_HF), col(COL_HI),
                  full((1, HG_WIDTH)), full((1, HG_WIDTH)), full((1, HG_WIDTH)),
                  full(a_all.shape), full(mask2.shape)],
        out_specs=pl.BlockSpec((tm, HG_WIDTH), lambda b, i: (b * steps + i, 0)),
        out_shape=jax.ShapeDtypeStruct((n, HG_WIDTH), F32),
        scratch_shapes=[pltpu.VMEM((HG_HEADS // 2, LANES, LANES), F32)],
        compiler_params=_params("parallel", "arbitrary"),
        name="hgrn2",
    )(h, h, h, jnp.log(lb), jnp.log1p(-lb), 1.0 - lb, a_all, mask2)


def _compress_body(x_ref, w1_ref, w2_ref, pos_ref, o_ref):
    x = x_ref[0, 0, 0].astype(BF16)
    w1 = w1_ref[0].astype(BF16)
    half = w1.shape[0] // 2
    a = _dot(x, w1[:half])
    b = _dot(x, w1[half:])
    pos = jnp.broadcast_to(pos_ref[0], (8, pos_ref.shape[-1])).astype(BF16)
    pw = _dot(pos, w1)[0:1]
    nrow = x.shape[0]
    b_next = jnp.concatenate([b[1:], b[:1]], axis=0)
    hid = _gelu(a + b_next + pw)
    out = _dot(hid.astype(BF16), w2_ref[0].astype(BF16))
    rid = lax.broadcasted_iota(jnp.int32, out.shape, 0)
    o_ref[0, 0, 0] = jnp.where(rid < nrow - 1, out, 0.0)


def _compress(xkv, w1, w2d, pos):
    b, _, hk, nr, wd = xkv.shape
    return pl.pallas_call(
        _compress_body,
        grid=(b, 2, hk),
        in_specs=[
            pl.BlockSpec((1, 1, 1, nr, wd), lambda i, j, k: (i, j, k, 0, 0)),
            pl.BlockSpec((1,) + w1.shape[1:], lambda i, j, k: (j, 0, 0)),
            pl.BlockSpec((1,) + w2d.shape[1:], lambda i, j, k: (j, 0, 0)),
            pl.BlockSpec((1, 1, pos.shape[-1]), lambda i, j, k: (j, 0, 0)),
        ],
        out_specs=pl.BlockSpec((1, 1, 1, nr, w2d.shape[-1]), lambda i, j, k: (i, j, k, 0, 0)),
        out_shape=jax.ShapeDtypeStruct((b, 2, hk, nr, w2d.shape[-1]), F32),
        compiler_params=_params("parallel", "parallel", "parallel"),
        name="nsa_compress",
    )(xkv, w1, w2d, pos)


def _select_body(q_ref, kvc_ref, ovt_ref, eye_ref, ocmp_ref, bias_ref, *, n_sel):
    c = pl.program_id(1)
    ncmp = kvc_ref.shape[-2]
    nblk = ovt_ref.shape[0]
    q = q_ref[...] * ATTN_SCALE
    lane = lax.broadcasted_iota(jnp.int32, (Q_BLOCK, LANES), 1)
    lo_half = lane < HEAD_DIM
    t_col = c * Q_BLOCK + lax.broadcasted_iota(jnp.int32, (Q_BLOCK, 1), 0)
    n_idx = lax.broadcasted_iota(jnp.int32, (1, ncmp), 1)
    valid = (n_idx * CMP_STRIDE + (CMP_BLOCK - 1)) <= t_col
    jb = lax.broadcasted_iota(jnp.int32, (nblk, 1), 0)
    t_row = c * Q_BLOCK + lax.broadcasted_iota(jnp.int32, (1, Q_BLOCK), 1)
    cur = lax.shift_right_logical(t_row, 6)
    forced = (jb == 0) | (jb == cur) | (jb == cur - 1)
    causal = jb * SLC_BLOCK <= t_row
    vals = []
    for hk in range(NSA_KV_HEADS):
        kc = kvc_ref[0, 0, hk].astype(BF16)
        vc = kvc_ref[0, 1, hk].astype(BF16)
        psum = jnp.zeros((Q_BLOCK, ncmp), F32)
        outs = []
        for g in range(NSA_GROUP):
            hq = hk * NSA_GROUP + g
            pair = q[:, (hq // 2) * LANES:(hq // 2 + 1) * LANES]
            qh = jnp.where(lo_half if hq % 2 == 0 else jnp.logical_not(lo_half), pair, 0.0).astype(BF16)
            s = _dot_nt(qh, kc)
            m = jnp.max(jnp.where(valid, s, NEG_INF), axis=-1, keepdims=True)
            e = jnp.where(valid, jnp.exp(s - m), 0.0)
            l = jnp.sum(e, axis=-1, keepdims=True)
            p = jnp.where(l > 0.0, e / l, 0.0)
            outs.append(_dot(p.astype(BF16), vc))
            psum = psum + p
        for j in range(NSA_GROUP // 2):
            col0 = (hk * (NSA_GROUP // 2) + j) * LANES
            ocmp_ref[:, col0:col0 + LANES] = jnp.where(lo_half, outs[2 * j], outs[2 * j + 1])
        ph, plo = _split2(psum)
        imp = _dot_nt(ovt_ref[...], ph) + _dot_nt(ovt_ref[...], plo)
        vals.append(jnp.where(causal, imp + jnp.where(forced, FORCE_SELECT, 0.0), NEG_INF))
    val0 = jnp.concatenate(vals, axis=1)
    val = val0
    for _ in range(n_sel):
        m = jnp.max(val, axis=0, keepdims=True)
        idx = jnp.min(jnp.where(val == m, jb, nblk), axis=0, keepdims=True)
        val = jnp.where(jb == idx, TAKEN, val)
    sel = jnp.where((val == TAKEN) & (val0 > 0.5 * NEG_INF), 1.0, 0.0).astype(BF16)
    for hk in range(NSA_KV_HEADS):
        sel_q = _dot_nt(eye_ref[...], sel[:, hk * Q_BLOCK:(hk + 1) * Q_BLOCK])
        bias_ref[0, hk] = jnp.where(sel_q > 0.5, 0.0, NEG_INF).astype(BF16)


def _select(h, kvcmp, batch):
    n = h.shape[0]
    t = n // batch
    nq = t // Q_BLOCK
    ncmp = kvcmp.shape[-2]
    nblk = t // SLC_BLOCK
    ii = np.arange(ncmp)[None, :]
    jj = np.arange(nblk)[:, None]
    ovt = ((ii * CMP_STRIDE < (jj + 1) * SLC_BLOCK) & (ii * CMP_STRIDE + CMP_BLOCK > jj * SLC_BLOCK)
           & (ii < ncmp - 1))
    ovt = jnp.asarray(ovt.astype(np.float32), dtype=BF16)
    eye = jnp.asarray(np.eye(Q_BLOCK, dtype=np.float32), dtype=BF16)
    full = lambda shape: pl.BlockSpec(shape, lambda b, i: (0,) * len(shape))
    kv_spec = pl.BlockSpec((1,) + kvcmp.shape[1:], lambda b, i: (b, 0, 0, 0, 0))
    return pl.pallas_call(
        functools.partial(_select_body, n_sel=min(N_SLC, nblk)),
        grid=(batch, nq),
        in_specs=[pl.BlockSpec((Q_BLOCK, NSA_WIDTH), lambda b, i: (b * nq + i, COL_NQ // NSA_WIDTH)),
                  kv_spec, full(ovt.shape), full(eye.shape)],
        out_specs=[pl.BlockSpec((Q_BLOCK, NSA_WIDTH), lambda b, i: (b * nq + i, 0)),
                   pl.BlockSpec((1, NSA_KV_HEADS, Q_BLOCK, nblk), lambda b, i: (b, 0, i, 0))],
        out_shape=[jax.ShapeDtypeStruct((n, NSA_WIDTH), F32),
                   jax.ShapeDtypeStruct((batch, NSA_KV_HEADS, t, nblk), BF16)],
        compiler_params=_params("parallel", "parallel"),
        name="nsa_select",
    )(h, kvcmp, ovt, eye)


def _softmax_tile_t(s, m_prev):
    m_new = jnp.maximum(m_prev, jnp.max(s, axis=0, keepdims=True))
    return m_new, jnp.exp2(s - m_new).astype(BF16)


def _emit_heads(acc, pe_ref, po_ref, out_ref):
    o_t = acc[0:HEAD_DIM] / acc[HEAD_DIM:HEAD_DIM + 1]
    hi, lo = _split2(o_t)
    for j in range(NSA_GROUP // 2):
        a = slice(2 * j * Q_BLOCK, (2 * j + 1) * Q_BLOCK)
        b = slice((2 * j + 1) * Q_BLOCK, (2 * j + 2) * Q_BLOCK)
        out_ref[:, j * LANES:(j + 1) * LANES] = (
            _dot_tn(hi[:, a], pe_ref[...]) + _dot_tn(lo[:, a], pe_ref[...])
            + _dot_tn(hi[:, b], po_ref[...]) + _dot_tn(lo[:, b], po_ref[...]))


def _attn_body(q_ref, bias_ref, kaug_ref, vst_ref, kw_ref, vwt_ref, pe_ref, po_ref, oslc_ref, owin_ref,
               qaug_sc, acc_sc, sa_sc, sb_sc):
    c = pl.program_id(2)
    rows = NSA_GROUP * Q_BLOCK
    q4 = q_ref[...] * (ATTN_SCALE * LOG2E)
    lane = lax.broadcasted_iota(jnp.int32, (Q_BLOCK, LANES), 1)
    lo_half = lane < HEAD_DIM
    parts = []
    for g in range(NSA_GROUP):
        pair = q4[:, (g // 2) * LANES:(g // 2 + 1) * LANES]
        parts.append(jnp.where(lo_half if g % 2 == 0 else jnp.logical_not(lo_half), pair, 0.0))
    qpart = jnp.concatenate(parts, axis=0).astype(BF16)
    qaug_sc[:, 0:LANES] = qpart
    bias = bias_ref[0, 0]
    for g in range(NSA_GROUP):
        qaug_sc[g * Q_BLOCK:(g + 1) * Q_BLOCK, LANES:] = bias
    t_row = c * Q_BLOCK + (lax.broadcasted_iota(jnp.int32, (1, rows), 1) & (Q_BLOCK - 1))

    acc_sc[...] = jnp.zeros(acc_sc.shape, F32)
    last_tile = kaug_ref.shape[2] // ATTN_TK - 1

    def scores(tile, s_ref):
        start = pl.multiple_of(jnp.minimum(tile, last_tile) * ATTN_TK, ATTN_TK)
        s = _dot_nt(kaug_ref[0, 0, pl.ds(start, ATTN_TK), :], qaug_sc[...])
        kpos = tile * ATTN_TK + lax.broadcasted_iota(jnp.int32, (ATTN_TK, 1), 0)
        s = jnp.where(kpos <= t_row, s, NEG_INF)
        s_ref[...] = s
        return jnp.max(s, axis=0, keepdims=True)

    def accumulate(tile, m_prev, m_tile, s_ref):
        m_new = jnp.maximum(m_prev, m_tile)
        p = jnp.exp2(s_ref[...] - m_new).astype(BF16)
        acc_sc[...] = (jnp.exp2(m_prev - m_new) * acc_sc[...]
                       + _dot(vst_ref[0, 0, jnp.minimum(tile, last_tile)], p))
        return m_new

    def body(i, carry):
        m, mt_a = carry
        mt_b = scores(2 * i + 1, sb_sc)
        m = accumulate(2 * i, m, mt_a, sa_sc)
        mt_a = scores(2 * i + 2, sa_sc)
        m = accumulate(2 * i + 1, m, mt_b, sb_sc)
        return m, mt_a

    n_full = (c * Q_BLOCK) // ATTN_TK
    carry = (jnp.full((1, rows), TAKEN, F32), scores(0, sa_sc))
    m, mt_a = lax.fori_loop(0, (n_full + 1) // 2, body, carry)
    accumulate(2 * ((n_full + 1) // 2), m, mt_a, sa_sc)
    _emit_heads(acc_sc[...], pe_ref, po_ref, oslc_ref)

    span = WINDOW + Q_BLOCK
    wblk = jnp.maximum(c - WINDOW // Q_BLOCK, 0)
    wstart = pl.multiple_of(wblk * Q_BLOCK, Q_BLOCK)
    s = _dot_nt(kw_ref[0, 0, pl.ds(wstart, span), :], qpart)
    kpos = wstart + lax.broadcasted_iota(jnp.int32, (span, 1), 0)
    ok = (kpos <= t_row) & (kpos > t_row - WINDOW)
    _, p = _softmax_tile_t(jnp.where(ok, s, NEG_INF), jnp.full((1, rows), TAKEN, F32))
    acc = _dot(vwt_ref[0, 0, wblk], p[0:Q_BLOCK])
    for i in range(1, span // Q_BLOCK):
        acc = acc + _dot(vwt_ref[0, 0, wblk + i], p[i * Q_BLOCK:(i + 1) * Q_BLOCK])
    _emit_heads(acc, pe_ref, po_ref, owin_ref)


def _values_t(v, tile):
    b, hk, t, dh = v.shape
    vt = jnp.concatenate([v.transpose(0, 1, 3, 2), jnp.ones((b, hk, 1, t), v.dtype),
                          jnp.zeros((b, hk, V_ROWS - dh - 1, t), v.dtype)], axis=2)
    return vt.reshape(b, hk, V_ROWS, t // tile, tile).transpose(0, 1, 3, 2, 4).astype(BF16)


def _attention(h, bias, kaug, vs, kw, vw, batch):
    n = h.shape[0]
    t = n // batch
    nq = t // Q_BLOCK
    gw = NSA_GROUP * HEAD_DIM
    rows = NSA_GROUP * Q_BLOCK
    vst = _values_t(vs, ATTN_TK)
    vwt = _values_t(vw, Q_BLOCK)
    eye = np.eye(HEAD_DIM, dtype=np.float32)
    zero = np.zeros_like(eye)
    pe = jnp.asarray(np.concatenate([eye, zero], axis=1), dtype=BF16)
    po = jnp.asarray(np.concatenate([zero, eye], axis=1), dtype=BF16)
    res = lambda a: pl.BlockSpec((1, 1) + a.shape[2:], lambda b, k, i: (b, k) + (0,) * (a.ndim - 2))
    full = lambda a: pl.BlockSpec(a.shape, lambda b, k, i: (0,) * a.ndim)
    out_spec = pl.BlockSpec((Q_BLOCK, gw), lambda b, k, i: (b * nq + i, k))
    return pl.pallas_call(
        _attn_body,
        grid=(batch, NSA_KV_HEADS, nq),
        in_specs=[pl.BlockSpec((Q_BLOCK, gw), lambda b, k, i: (b * nq + i, COL_NQ // gw + k)),
                  pl.BlockSpec((1, 1, Q_BLOCK, bias.shape[-1]), lambda b, k, i: (b, k, i, 0)),
                  res(kaug), res(vst), res(kw), res(vwt), full(pe), full(po)],
        out_specs=[out_spec, out_spec],
        out_shape=[jax.ShapeDtypeStruct((n, NSA_WIDTH), F32)] * 2,
        scratch_shapes=[pltpu.VMEM((rows, kaug.shape[-1]), BF16), pltpu.VMEM((V_ROWS, rows), F32),
                        pltpu.VMEM((ATTN_TK, rows), F32), pltpu.VMEM((ATTN_TK, rows), F32)],
        compiler_params=_params("parallel", "parallel", "arbitrary"),
        name="nsa_attention",
    )(h, bias, kaug, vst, kw, vwt, pe, po)


def _branch_expanders():
    e = np.zeros((N_BRANCH, GATE_PAD, NSA_WIDTH), np.float32)
    for hq in range(NSA_HEADS):
        for br in range(N_BRANCH):
            e[br, hq * N_BRANCH + br, hq * HEAD_DIM:(hq + 1) * HEAD_DIM] = 1.0
    return jnp.asarray(e, dtype=BF16)


def _route(logits):
    lane = lax.broadcasted_iota(jnp.int32, logits.shape, 1)
    is_g = lane < N_GROUPS
    gl = jnp.where(is_g, logits, -jnp.inf)
    gmax = jnp.max(gl, axis=-1, keepdims=True)
    gsum = jnp.sum(jnp.where(is_g, jnp.exp(logits - gmax), 0.0), axis=-1, keepdims=True)
    g_val = 1.0 / gsum
    g_idx = jnp.min(jnp.where(is_g & (logits == gmax), lane, ROUTER_PAD), axis=-1, keepdims=True)
    e_lo = EXPERT_LANE0 + g_idx * EXPERTS_PER_GROUP
    in_sel = (lane >= e_lo) & (lane < e_lo + EXPERTS_PER_GROUP)
    el = jnp.where(in_sel, logits, -jnp.inf)
    m1 = jnp.max(el, axis=-1, keepdims=True)
    i1 = jnp.min(jnp.where(in_sel & (logits == m1), lane, ROUTER_PAD), axis=-1, keepdims=True)
    el2 = jnp.where(lane == i1, -jnp.inf, el)
    m2 = jnp.max(el2, axis=-1, keepdims=True)
    i2 = jnp.min(jnp.where(el2 == m2, lane, ROUTER_PAD), axis=-1, keepdims=True)
    r = jnp.exp(m2 - m1)
    w1 = 1.0 / (1.0 + r)
    w2 = r * w1
    return g_val * (jnp.where(lane == i1, w1, 0.0) + jnp.where(lane == i2, w2, 0.0))


def _out_body(ya_ref, ocmp_ref, oslc_ref, owin_ref, gl_ref, yc_ref, hg_ref, x_ref, gb_ref, ex_ref,
              gmb_ref, gmc_ref, og_ref, w_ref, lg_ref, lb_ref, wrh_ref, wrl_ref, br_ref, x1_ref, gate_ref):
    g = _sigmoid(gl_ref[...] + gb_ref[...])
    yb = (_dot2(g, ex_ref[0]) * ocmp_ref[...] + _dot2(g, ex_ref[1]) * oslc_ref[...]
          + _dot2(g, ex_ref[2]) * owin_ref[...])
    n_ab = GM_WIDTH + NSA_WIDTH
    ybn = yb * lax.rsqrt(_dot2(yb * yb, gmb_ref[...]) + RMS_EPS) * og_ref[:, GM_WIDTH:n_ab]
    yc = yc_ref[...]
    hg = hg_ref[...]
    ycn = yc * lax.rsqrt(_dot2(yc * yc, gmc_ref[...]) + RMS_EPS) * og_ref[:, n_ab:] * (hg * _sigmoid(hg))
    y = (_dot(ya_ref[...].astype(BF16), w_ref[0:GM_WIDTH, :]) + _dot(ybn.astype(BF16), w_ref[GM_WIDTH:n_ab, :])
         + _dot(ycn.astype(BF16), w_ref[n_ab:, :]))
    x1 = _layer_norm(DEEPNORM_ALPHA * x_ref[...] + y, lg_ref[...], lb_ref[...])
    x1_ref[...] = x1
    xh, xl = _split2(x1)
    logits = _dot(xh, wrh_ref[...]) + _dot(xh, wrl_ref[...]) + _dot(xl, wrh_ref[...]) + br_ref[...]
    gate_ref[...] = _route(logits)


def _outproj(ya, ocmp, oslc, owin, h, yc, x2, gate_b, out_gain, w_out, ln_g, ln_b, wr, br):
    n, d = x2.shape
    tm = 256
    row = lambda a: a.reshape(1, -1)
    gb = jnp.pad(gate_b, (0, GATE_PAD - gate_b.shape[0])).reshape(1, GATE_PAD)
    wrh = wr.astype(BF16)
    wrl = (wr - wrh.astype(F32)).astype(BF16)
    ex = _branch_expanders()
    full = lambda shape: pl.BlockSpec(shape, lambda i: (0,) * len(shape))
    tile = lambda w, cb=0: pl.BlockSpec((tm, w), lambda i: (i, cb))
    return pl.pallas_call(
        _out_body,
        grid=(n // tm,),
        in_specs=[tile(GM_WIDTH), tile(NSA_WIDTH), tile(NSA_WIDTH), tile(NSA_WIDTH),
                  tile(GATE_PAD, COL_NG // GATE_PAD), tile(HG_WIDTH), tile(HG_WIDTH, COL_HG // HG_WIDTH), tile(d),
                  full((1, GATE_PAD)), full(ex.shape), full((NSA_WIDTH, NSA_WIDTH)), full((HG_WIDTH, HG_WIDTH)),
                  full((1, d)), full((d, d)), full((1, d)), full((1, d)),
                  full((d, ROUTER_PAD)), full((d, ROUTER_PAD)), full((1, ROUTER_PAD))],
        out_specs=[tile(d), tile(ROUTER_PAD)],
        out_shape=[jax.ShapeDtypeStruct((n, d), F32), jax.ShapeDtypeStruct((n, ROUTER_PAD), F32)],
        compiler_params=_params("parallel"),
        name="outproj_ln_router",
    )(ya, ocmp, oslc, owin, h, yc, h, x2, gb, ex, _group_mean_matrix(NSA_WIDTH), _group_mean_matrix(HG_WIDTH),
      row(out_gain), w_out.astype(BF16), row(ln_g), row(ln_b), wrh, wrl, br)


def _moe_body(x_ref, gate_ref, wg_ref, wu_ref, wd_ref, lg_ref, lb_ref, o_ref,
              xb_sc, acc_sc, slot_sc, slott_sc, cnt_sc):
    e = pl.program_id(1)
    tm = x_ref.shape[0]

    @pl.when(e == 0)
    def _():
        xb_sc[...] = x_ref[...].astype(BF16)
        acc_sc[...] = jnp.zeros_like(acc_sc)
        used = jnp.where(gate_ref[...] > 0.0, 1.0, 0.0)
        r = lax.broadcasted_iota(jnp.int32, (tm, tm), 0)
        q = lax.broadcasted_iota(jnp.int32, (tm, tm), 1)
        before = jnp.where(q < r, 1.0, 0.0).astype(BF16)
        after = jnp.where(r < q, 1.0, 0.0).astype(BF16)
        ub = used.astype(BF16)
        slot_sc[...] = jnp.where(used > 0.0, _dot(before, ub), -1.0)
        ri = lax.broadcasted_iota(jnp.int32, (ROUTER_PAD, ROUTER_PAD), 0)
        qi = lax.broadcasted_iota(jnp.int32, (ROUTER_PAD, ROUTER_PAD), 1)
        used_t = _dot_nt(jnp.where(ri == qi, 1.0, 0.0).astype(BF16), ub)
        slott_sc[...] = jnp.where(used_t > 0.0, _dot(used_t.astype(BF16), after), -1.0)
        cnt_sc[...] = jnp.broadcast_to(jnp.sum(used, axis=0, keepdims=True), cnt_sc.shape)

    lane = lax.broadcasted_iota(jnp.int32, gate_ref.shape, 1)
    mine = lane == e + EXPERT_LANE0
    gcol = jnp.sum(jnp.where(mine, gate_ref[...], 0.0), axis=-1, keepdims=True)
    scol = jnp.sum(jnp.where(mine, slot_sc[...], 0.0), axis=-1, keepdims=True)
    srow = slott_sc[pl.ds(e + EXPERT_LANE0, 1), :]
    n_rows = jnp.sum(jnp.where(mine[0:1], cnt_sc[0:1], 0.0)).astype(jnp.int32)

    def chunk(ci, carry):
        base = (ci * MOE_CHUNK).astype(F32)
        r_col = base + lax.broadcasted_iota(jnp.int32, (MOE_CHUNK, 1), 0).astype(F32)
        r_row = base + lax.broadcasted_iota(jnp.int32, (1, MOE_CHUNK), 1).astype(F32)
        pick = jnp.where(srow == r_col, 1.0, 0.0).astype(BF16)
        xg = _dot(pick, xb_sc[...]).astype(BF16)
        hg = _dot(xg, wg_ref[0])
        hu = _dot(xg, wu_ref[0])
        y = _dot((hg * _sigmoid(hg) * hu).astype(BF16), wd_ref[0]).astype(BF16)
        place = jnp.where(scol == r_row, 1.0, 0.0).astype(BF16)
        acc_sc[...] += gcol * _dot(place, y)
        return carry

    lax.fori_loop(0, (n_rows + MOE_CHUNK - 1) // MOE_CHUNK, chunk, 0)

    @pl.when(e == pl.num_programs(1) - 1)
    def _():
        o_ref[...] = _layer_norm(DEEPNORM_ALPHA * x_ref[...] + acc_sc[...], lg_ref[...], lb_ref[...])


def _moe(x1, gate, wg, wu, wd, ln_g, ln_b):
    n, d = x1.shape
    ne, _, de = wg.shape
    tm = MOE_TM
    row = lambda a: a.reshape(1, -1)
    return pl.pallas_call(
        _moe_body,
        grid=(n // tm, ne),
        in_specs=[pl.BlockSpec((tm, d), lambda i, e: (i, 0)),
                  pl.BlockSpec((tm, ROUTER_PAD), lambda i, e: (i, 0)),
                  pl.BlockSpec((1, d, de), lambda i, e: (e, 0, 0)),
                  pl.BlockSpec((1, d, de), lambda i, e: (e, 0, 0)),
                  pl.BlockSpec((1, de, d), lambda i, e: (e, 0, 0)),
                  pl.BlockSpec((1, d), lambda i, e: (0, 0)),
                  pl.BlockSpec((1, d), lambda i, e: (0, 0))],
        out_specs=pl.BlockSpec((tm, d), lambda i, e: (i, 0)),
        out_shape=jax.ShapeDtypeStruct((n, d), F32),
        scratch_shapes=[pltpu.VMEM((tm, d), BF16), pltpu.VMEM((tm, d), F32),
                        pltpu.VMEM((tm, ROUTER_PAD), F32), pltpu.VMEM((ROUTER_PAD, tm), F32),
                        pltpu.VMEM((8, ROUTER_PAD), F32)],
        compiler_params=_params("parallel", "arbitrary"),
        name="moe_ln",
    )(x1, gate, wg, wu, wd, row(ln_g), row(ln_b))


def _reorder_w_in(w_in):
    pts = np.cumsum((GM_WIDTH, GM_WIDTH, NSA_WIDTH) + (NSA_KV_WIDTH,) * 6 + (N_BRANCH * NSA_HEADS,)
                    + (HG_WIDTH,) * 4)
    ng0, ng1 = int(pts[8]), int(pts[9])
    pad = jnp.zeros((w_in.shape[0], GATE_PAD - (ng1 - ng0)), w_in.dtype)
    return jnp.concatenate([w_in[:, :ng0], w_in[:, ng1:], w_in[:, ng0:ng1], pad], axis=1).astype(BF16)


def _kv_heads(h, col, batch):
    t = h.shape[0] // batch
    a = h[:, col:col + NSA_KV_WIDTH].reshape(batch, t, NSA_KV_HEADS, HEAD_DIM)
    return a.transpose(0, 2, 1, 3)


def _layer(x2, batch, w_in, gm_v_gain, gm_v_bias, gm_w_s, gm_b_s, cmp_pos, cmp_w1, cmp_w2, nsa_gate_b, lb,
           out_gain, w_out, ln1_g, ln1_b, wr, br, wg, wu, wd, ln2_g, ln2_b):
    n = x2.shape[0]
    t = n // batch
    h = _inproj(x2, _reorder_w_in(w_in))

    ya = _gmlp(h, gm_v_gain, gm_v_bias, gm_w_s, gm_b_s, out_gain[:GM_WIDTH])
    yc = _hgrn(h, lb, batch)

    kvc = h[:, COL_KC:COL_KC + 2 * NSA_KV_WIDTH].reshape(batch, t, 2, NSA_KV_HEADS, HEAD_DIM)
    kvc = kvc.transpose(0, 2, 3, 1, 4).reshape(batch, 2, NSA_KV_HEADS, t // CMP_STRIDE, CMP_STRIDE * HEAD_DIM)
    w2d = jnp.concatenate([cmp_w2, cmp_w2], axis=-1)
    kvcmp = _compress(kvc, cmp_w1, w2d, cmp_pos.reshape(2, 1, CMP_BLOCK * HEAD_DIM))
    ocmp, bias = _select(h, kvcmp, batch)

    dup = lambda a: jnp.concatenate([a, a], axis=-1).astype(BF16)
    ks = _kv_heads(h, COL_KS, batch)
    blk_onehot = jnp.asarray((np.arange(t)[:, None] // SLC_BLOCK == np.arange(t // SLC_BLOCK)[None, :])
                             .astype(np.float32), dtype=BF16)
    kaug = jnp.concatenate([dup(ks), jnp.broadcast_to(blk_onehot, ks.shape[:2] + blk_onehot.shape)], axis=-1)
    oslc, owin = _attention(h, bias, kaug, _kv_heads(h, COL_VS, batch), dup(_kv_heads(h, COL_KW, batch)),
                            _kv_heads(h, COL_VW, batch), batch)

    x1, gate = _outproj(ya, ocmp, oslc, owin, h, yc, x2, nsa_gate_b, out_gain, w_out, ln1_g, ln1_b, wr, br)
    return _moe(x1, gate, wg, wu, wd, ln2_g, ln2_b)


def kernel(x, w_in, gm_v_gain, gm_v_bias, gm_w_s, gm_b_s, cmp_pos, cmp_w1, cmp_w2, nsa_gate_b, hg_lower, out_gain, w_out, ln1_g, ln1_b, router_group_w, router_group_b, router_expert_w, router_expert_b, exp_w_gate, exp_w_up, exp_w_down, ln2_g, ln2_b):
    batch, t, d = x.shape
    depth = w_in.shape[0]
    lb_all = jnp.cumsum(jax.nn.softmax(hg_lower.astype(F32), axis=0), axis=0)
    lb_all = lb_all - lb_all[0]
    x2 = x.reshape(batch * t, d)
    for l in range(depth):
        pad = ROUTER_PAD - N_GROUPS - N_EXPERTS
        wr = jnp.concatenate([router_group_w[l], router_expert_w[l], jnp.zeros((d, pad), F32)], axis=1)
        br = jnp.concatenate([router_group_b[l], router_expert_b[l], jnp.zeros((pad,), F32)]).reshape(1, ROUTER_PAD)
        x2 = _layer(x2, batch, w_in[l], gm_v_gain[l], gm_v_bias[l], gm_w_s[l], gm_b_s[l], cmp_pos[l], cmp_w1[l],
                    cmp_w2[l], nsa_gate_b[l], lb_all[l], out_gain[l], w_out[l], ln1_g[l], ln1_b[l], wr, br,
                    exp_w_gate[l].astype(BF16), exp_w_up[l].astype(BF16), exp_w_down[l].astype(BF16),
                    ln2_g[l], ln2_b[l])
    return x2.reshape(batch, t, d)
```

```python
import functools

import numpy as np
import jax
import jax.numpy as jnp
from jax import lax
from jax.experimental import pallas as pl
from jax.experimental.pallas import tpu as pltpu

F32 = jnp.float32
BF16 = jnp.bfloat16

D_MODEL = 1024
HEAD_DIM = 64
GM_GROUPS = 4
GM_WIDTH = GM_GROUPS * HEAD_DIM
GM_CHUNK = 128
NSA_HEADS = 8
NSA_KV_HEADS = 2
NSA_GROUP = NSA_HEADS // NSA_KV_HEADS
NSA_WIDTH = NSA_HEADS * HEAD_DIM
NSA_KV_WIDTH = NSA_KV_HEADS * HEAD_DIM
CMP_BLOCK = 32
CMP_STRIDE = 16
SLC_BLOCK = 64
N_SLC = 16
WINDOW = 512
Q_BLOCK = 128
N_BRANCH = 3
HG_HEADS = 4
HG_WIDTH = HG_HEADS * HEAD_DIM
HG_CHUNK = 64
N_GROUPS = 4
EXPERTS_PER_GROUP = 4
N_EXPERTS = N_GROUPS * EXPERTS_PER_GROUP
D_EXPERT = 512
DEPTH = 2
DEEPNORM_ALPHA = (2.0 * DEPTH) ** 0.25
LN_EPS = 1e-5
RMS_EPS = 1e-6
NEG_INF = -1e30
FORCE_SELECT = 1e4
TAKEN = -(2.0 ** 126)
ATTN_SCALE = HEAD_DIM ** -0.5

LANES = 128
GATE_PAD = LANES
ROUTER_PAD = LANES
EXPERT_LANE0 = N_GROUPS
VMEM_LIMIT = 56 * 1024 * 1024

COL_GU = 0
COL_GV = 256
COL_NQ = 512
COL_KC = 1024
COL_HQ = 1280
COL_HF = 1536
COL_HI = 1792
COL_HG = 2048
COL_NG = 2304
H_COLS = COL_NG + GATE_PAD
KDUP_COLS = 4 * NSA_KV_WIDTH

ATTN_TK = 512
V_ROWS = HEAD_DIM + 16
LOG2E = 1.4426950408889634
MOE_TM = 1024
MOE_CHUNK = 160


def _dot(a, b):
    return jnp.dot(a, b, preferred_element_type=F32)


def _dot_nt(a, b):
    return lax.dot_general(a, b, (((1,), (1,)), ((), ())), preferred_element_type=F32)


def _dot_tn(a, b):
    return lax.dot_general(a, b, (((0,), (0,)), ((), ())), preferred_element_type=F32)


def _split2(x):
    hi = x.astype(BF16)
    lo = (x - hi.astype(F32)).astype(BF16)
    return hi, lo


def _split3(x):
    hi = x.astype(BF16)
    r = x - hi.astype(F32)
    mid = r.astype(BF16)
    lo = (r - mid.astype(F32)).astype(BF16)
    return hi, mid, lo


def _dot2(x, m):
    hi, lo = _split2(x)
    return _dot(hi, m) + _dot(lo, m)


def _gelu(x):
    return 0.5 * x * (1.0 + jnp.tanh(0.7978845608028654 * (x + 0.044715 * (x * x * x))))


def _sigmoid(x):
    return 1.0 / (1.0 + jnp.exp(-x))


def _log_sigmoid(x):
    return jnp.minimum(x, 0.0) - jnp.log1p(jnp.exp(-jnp.abs(x)))


def _layer_norm(z, g, b):
    mu = jnp.mean(z, axis=-1, keepdims=True)
    zc = z - mu
    var = jnp.mean(zc * zc, axis=-1, keepdims=True)
    return zc * lax.rsqrt(var + LN_EPS) * g + b


def _params(*sem):
    return pltpu.CompilerParams(dimension_semantics=sem, vmem_limit_bytes=VMEM_LIMIT)


def _inproj_body(x_ref, w_ref, wvt_ref, h_ref, kaug_ref, kw_ref, vst_ref, vwt_ref):
    xb = x_ref[...].astype(BF16)
    tm = xb.shape[0]
    acc = _dot(xb, w_ref[...])
    h_ref[...] = acc[:, :H_COLS]
    nblk = kaug_ref.shape[-1] - LANES
    pos = pl.program_id(1) * tm + lax.broadcasted_iota(jnp.int32, (tm, nblk), 0)
    blk = lax.broadcasted_iota(jnp.int32, (tm, nblk), 1)
    onehot = jnp.where(lax.shift_right_logical(pos, 6) == blk, 1.0, 0.0).astype(BF16)
    for hk in range(NSA_KV_HEADS):
        c0 = H_COLS + hk * LANES
        kaug_ref[0, hk, :, 0:LANES] = acc[:, c0:c0 + LANES].astype(BF16)
        kaug_ref[0, hk, :, LANES:] = onehot
        c1 = H_COLS + (NSA_KV_HEADS + hk) * LANES
        kw_ref[0, hk] = acc[:, c1:c1 + LANES].astype(BF16)
    vt = _dot_nt(wvt_ref[...], xb)
    ones_then_zeros = lambda w: jnp.where(
        lax.broadcasted_iota(jnp.int32, (V_ROWS - HEAD_DIM, w), 0) == 0, 1.0, 0.0).astype(BF16)
    tail, tail_q = ones_then_zeros(tm), ones_then_zeros(Q_BLOCK)
    for hk in range(NSA_KV_HEADS):
        vst_ref[0, hk, 0, 0:HEAD_DIM, :] = vt[hk * HEAD_DIM:(hk + 1) * HEAD_DIM].astype(BF16)
        vst_ref[0, hk, 0, HEAD_DIM:, :] = tail
        r0 = (NSA_KV_HEADS + hk) * HEAD_DIM
        for i in range(tm // Q_BLOCK):
            ls = slice(i * Q_BLOCK, (i + 1) * Q_BLOCK)
            vwt_ref[0, hk, i, 0:HEAD_DIM, :] = vt[r0:r0 + HEAD_DIM, ls].astype(BF16)
            vwt_ref[0, hk, i, HEAD_DIM:, :] = tail_q


def _inproj(x2, w, wvt, batch):
    n, d = x2.shape
    t = n // batch
    tm = ATTN_TK
    steps = t // tm
    nblk = t // SLC_BLOCK
    hk = NSA_KV_HEADS
    return pl.pallas_call(
        _inproj_body,
        grid=(batch, steps),
        in_specs=[pl.BlockSpec((tm, d), lambda b, j: (b * steps + j, 0)),
                  pl.BlockSpec(w.shape, lambda b, j: (0, 0)),
                  pl.BlockSpec(wvt.shape, lambda b, j: (0, 0))],
        out_specs=[pl.BlockSpec((tm, H_COLS), lambda b, j: (b * steps + j, 0)),
                   pl.BlockSpec((1, hk, tm, LANES + nblk), lambda b, j: (b, 0, j, 0)),
                   pl.BlockSpec((1, hk, tm, LANES), lambda b, j: (b, 0, j, 0)),
                   pl.BlockSpec((1, hk, 1, V_ROWS, tm), lambda b, j: (b, 0, j, 0, 0)),
                   pl.BlockSpec((1, hk, tm // Q_BLOCK, V_ROWS, Q_BLOCK), lambda b, j: (b, 0, j, 0, 0))],
        out_shape=[jax.ShapeDtypeStruct((n, H_COLS), F32),
                   jax.ShapeDtypeStruct((batch, hk, t, LANES + nblk), BF16),
                   jax.ShapeDtypeStruct((batch, hk, t, LANES), BF16),
                   jax.ShapeDtypeStruct((batch, hk, steps, V_ROWS, tm), BF16),
                   jax.ShapeDtypeStruct((batch, hk, t // Q_BLOCK, V_ROWS, Q_BLOCK), BF16)],
        compiler_params=_params("parallel", "parallel"),
        name="inproj",
    )(x2, w, wvt)


def _group_mean_matrix(width):
    g = np.arange(width) // HEAD_DIM
    return jnp.asarray((g[:, None] == g[None, :]).astype(np.float32) / HEAD_DIM, dtype=BF16)


def _gmlp_body(gu_ref, gv_ref, gain_ref, bias_ref, ws_ref, bsx_ref, gm_ref, og_ref, o_ref, *, chunks):
    gm = gm_ref[...]
    row = lax.broadcasted_iota(jnp.int32, (GM_CHUNK, GM_CHUNK), 0)
    col = lax.broadcasted_iota(jnp.int32, (GM_CHUNK, GM_CHUNK), 1)
    tril = row >= col
    lane_grp = lax.shift_right_logical(lax.broadcasted_iota(jnp.int32, (GM_CHUNK, GM_WIDTH), 1), 6)
    ws = [jnp.where(tril, ws_ref[g], 0.0).astype(BF16) for g in range(GM_GROUPS)]
    for c in range(chunks):
        sl = slice(c * GM_CHUNK, (c + 1) * GM_CHUNK)
        u = _gelu(gu_ref[sl, :])
        v = _gelu(gv_ref[sl, :])
        mu = _dot2(v, gm)
        vc = v - mu
        var = _dot2(vc * vc, gm)
        vn = (vc * lax.rsqrt(var + LN_EPS) * gain_ref[...] + bias_ref[...]).astype(BF16)
        z = bsx_ref[...]
        for g in range(GM_GROUPS):
            z = z + jnp.where(lane_grp == g, _dot(ws[g], vn), 0.0)
        y = u * z
        ms = _dot2(y * y, gm)
        o_ref[sl, :] = y * lax.rsqrt(ms + RMS_EPS) * og_ref[...]


def _gmlp(h, gain, bias, w_s, b_s, out_gain_a):
    n = h.shape[0]
    chunks = 4
    tm = chunks * GM_CHUNK
    bsx = jnp.repeat(b_s.T, HEAD_DIM, axis=1)
    row = lambda a: a.reshape(1, -1)
    full = lambda shape: pl.BlockSpec(shape, lambda i: (0,) * len(shape))
    return pl.pallas_call(
        functools.partial(_gmlp_body, chunks=chunks),
        grid=(n // tm,),
        in_specs=[
            pl.BlockSpec((tm, GM_WIDTH), lambda i: (i, COL_GU // GM_WIDTH)),
            pl.BlockSpec((tm, GM_WIDTH), lambda i: (i, COL_GV // GM_WIDTH)),
            full((1, GM_WIDTH)), full((1, GM_WIDTH)),
            full((GM_GROUPS, GM_CHUNK, GM_CHUNK)), full((GM_CHUNK, GM_WIDTH)),
            full((GM_WIDTH, GM_WIDTH)), full((1, GM_WIDTH)),
        ],
        out_specs=pl.BlockSpec((tm, GM_WIDTH), lambda i: (i, 0)),
        out_shape=jax.ShapeDtypeStruct((n, GM_WIDTH), F32),
        compiler_params=_params("parallel"),
        name="gmlp",
    )(h, h, row(gain), row(bias), w_s, bsx, _group_mean_matrix(GM_WIDTH), row(out_gain_a))


N_LEVELS = 6


def _hgrn_constants():
    c = HG_CHUNK
    t = np.arange(c)[:, None]
    u = np.arange(c)[None, :]
    mats, masks = [], []
    m = c // 2
    while m >= 1:
        p = (t // (2 * m)) * (2 * m) + m - 1
        mats.append(np.where(t > p, (u > p) & (u <= t), (u > t) & (u <= p)))
        masks.append(((t // (2 * m)) == (u // (2 * m))) & ((t % (2 * m)) >= m) & ((u % (2 * m)) < m))
        m //= 2
    mats.append(u <= t)
    mats.append(u > t)
    masks.append(np.eye(c, dtype=bool))
    a_all = np.concatenate(mats, 0).astype(np.float32)
    mask2 = np.stack([np.concatenate([mk, mk], 0) for mk in masks]).astype(np.float32)
    return jnp.asarray(a_all, dtype=BF16), jnp.asarray(mask2)


def _hgrn_body(q_ref, f_ref, i_ref, loglb_ref, log1m_ref, oml_ref, a_ref, mk_ref, o_ref, st_ref, *, chunks):
    @pl.when(pl.program_id(1) == 0)
    def _():
        st_ref[...] = jnp.zeros_like(st_ref)

    c = HG_CHUNK
    a_all = a_ref[...]
    lane = lax.broadcasted_iota(jnp.int32, (c, LANES), 1)
    lo_half = lane < HEAD_DIM
    r2 = lax.broadcasted_iota(jnp.int32, (LANES, LANES), 0)
    c2 = lax.broadcasted_iota(jnp.int32, (LANES, LANES), 1)
    same_head = (r2 < HEAD_DIM) == (c2 < HEAD_DIM)
    for ch in range(chunks):
        sl = slice(ch * c, (ch + 1) * c)
        q = q_ref[sl, :]
        fl = f_ref[sl, :]
        iv = i_ref[sl, :]
        la = loglb_ref[...]
        lb2 = log1m_ref[...] + _log_sigmoid(fl)
        lf = jnp.maximum(la, lb2) + jnp.log1p(jnp.exp(-jnp.abs(la - lb2)))
        k = oml_ref[...] * (1.0 / (1.0 + jnp.exp(fl)))
        hi, mid, lo = _split3(lf)
        d_all = _dot(a_all, hi) + _dot(a_all, mid) + _dot(a_all, lo)
        e_all = jnp.exp(d_all)
        for p in range(HG_HEADS // 2):
            ls = slice(p * LANES, (p + 1) * LANES)
            qp, kp, ip = q[:, ls], k[:, ls], iv[:, ls]
            att = jnp.zeros((2 * c, c), F32)
            for lv in range(N_LEVELS + 1):
                if lv < N_LEVELS:
                    el = e_all[lv * c:(lv + 1) * c, ls]
                    ql, kl = qp * el, kp * el
                else:
                    ql, kl = qp, kp
                qstack = jnp.concatenate([jnp.where(lo_half, ql, 0.0), jnp.where(lo_half, 0.0, ql)], axis=0)
                att = att + _dot_nt(qstack.astype(BF16), kl.astype(BF16)) * mk_ref[lv]
            ipb = ip.astype(BF16)
            o_intra = jnp.where(lo_half, _dot(att[:c].astype(BF16), ipb), _dot(att[c:].astype(BF16), ipb))
            eb = e_all[N_LEVELS * c:(N_LEVELS + 1) * c, ls]
            ebl = e_all[(N_LEVELS + 1) * c:(N_LEVELS + 2) * c, ls]
            st = st_ref[p]
            o_inter = _dot_nt((qp * eb).astype(BF16), st.astype(BF16))
            o_ref[sl, ls] = o_intra + o_inter
            upd = _dot_tn(ipb, (kp * ebl).astype(BF16))
            st_ref[p] = st * eb[c - 1:c, :] + jnp.where(same_head, upd, 0.0)


def _hgrn(h, lb, batch):
    n = h.shape[0]
    t = n // batch
    chunks = 4
    tm = chunks * HG_CHUNK
    steps = t // tm
    a_all, mask2 = _hgrn_constants()
    lb = lb.reshape(1, HG_WIDTH).astype(F32)
    full = lambda shape: pl.BlockSpec(shape, lambda b, i: (0,) * len(shape))
    col = lambda off: pl.BlockSpec((tm, HG_WIDTH), lambda b, i: (b * steps + i, off // HG_WIDTH))
    return pl.pallas_call(
        functools.partial(_hgrn_body, chunks=chunks),
        grid=(batch, steps),
        in_specs=[col(COL_HQ), col(COL_HF), col(COL_HI),
                  full((1, HG_WIDTH)), full((1, HG_WIDTH)), full((1, HG_WIDTH)),
                  full(a_all.shape), full(mask2.shape)],
        out_specs=pl.BlockSpec((tm, HG_WIDTH), lambda b, i: (b * steps + i, 0)),
        out_shape=jax.ShapeDtypeStruct((n, HG_WIDTH), F32),
        scratch_shapes=[pltpu.VMEM((HG_HEADS // 2, LANES, LANES), F32)],
        compiler_params=_params("parallel", "arbitrary"),
        name="hgrn2",
    )(h, h, h, jnp.log(lb), jnp.log1p(-lb), 1.0 - lb, a_all, mask2)


def _compress_body(x_ref, w1_ref, w2k_ref, w2vt_ref, pos_ref, kc_ref, vct_ref):
    def hidden(kv):
        x = x_ref[0, kv, 0].astype(BF16)
        w1 = w1_ref[kv].astype(BF16)
        half = w1.shape[0] // 2
        a = _dot(x, w1[:half])
        b = _dot(x, w1[half:])
        pos = jnp.broadcast_to(pos_ref[kv], (8, pos_ref.shape[-1])).astype(BF16)
        pw = _dot(pos, w1)[0:1]
        b_next = jnp.concatenate([b[1:], b[:1]], axis=0)
        return _gelu(a + b_next + pw).astype(BF16)

    nrow = x_ref.shape[3]
    kc = _dot(hidden(0), w2k_ref[...].astype(BF16))
    rid = lax.broadcasted_iota(jnp.int32, kc.shape, 0)
    kc_ref[0, 0] = jnp.where(rid < nrow - 1, kc, 0.0).astype(BF16)
    vct = _dot_nt(w2vt_ref[...].astype(BF16), hidden(1))
    cid = lax.broadcasted_iota(jnp.int32, vct.shape, 1)
    vct_ref[0, 0] = jnp.where(cid < nrow - 1, vct, 0.0).astype(BF16)


def _compress(xkv, w1, w2, pos):
    b, _, hk, nr, wd = xkv.shape
    dh = w2.shape[-1]
    w2k = jnp.concatenate([w2[0], w2[0]], axis=-1)
    w2vt = w2[1].T
    full = lambda a: pl.BlockSpec(a.shape, lambda i, k: (0,) * a.ndim)
    return pl.pallas_call(
        _compress_body,
        grid=(b, hk),
        in_specs=[pl.BlockSpec((1, 2, 1, nr, wd), lambda i, k: (i, 0, k, 0, 0)),
                  full(w1), full(w2k), full(w2vt), full(pos)],
        out_specs=[pl.BlockSpec((1, 1, nr, 2 * dh), lambda i, k: (i, k, 0, 0)),
                   pl.BlockSpec((1, 1, dh, nr), lambda i, k: (i, k, 0, 0))],
        out_shape=[jax.ShapeDtypeStruct((b, hk, nr, 2 * dh), BF16),
                   jax.ShapeDtypeStruct((b, hk, dh, nr), BF16)],
        compiler_params=_params("parallel", "parallel"),
        name="nsa_compress",
    )(xkv, w1, w2k, w2vt, pos)


def _emit_heads(o_t, pe_ref, po_ref, out_ref, pair0=0):
    hi, lo = _split2(o_t)
    for j in range(NSA_GROUP // 2):
        a = slice(2 * j * Q_BLOCK, (2 * j + 1) * Q_BLOCK)
        b = slice((2 * j + 1) * Q_BLOCK, (2 * j + 2) * Q_BLOCK)
        out_ref[:, (pair0 + j) * LANES:(pair0 + j + 1) * LANES] = (
            _dot_tn(hi[:, a], pe_ref[...]) + _dot_tn(lo[:, a], pe_ref[...])
            + _dot_tn(hi[:, b], po_ref[...]) + _dot_tn(lo[:, b], po_ref[...]))


def _head_placers():
    eye = np.eye(HEAD_DIM, dtype=np.float32)
    zero = np.zeros_like(eye)
    pe = jnp.asarray(np.concatenate([eye, zero], axis=1), dtype=BF16)
    po = jnp.asarray(np.concatenate([zero, eye], axis=1), dtype=BF16)
    return jnp.asarray(np.eye(LANES, dtype=np.float32), dtype=BF16), pe, po


def _select_body(q_ref, kc_ref, vct_ref, ovt_ref, eye_ref, pe_ref, po_ref, ocmp_ref, bias_ref, *, n_sel):
    c = pl.program_id(1)
    ncmp = kc_ref.shape[-2]
    nblk = ovt_ref.shape[0]
    rows = NSA_GROUP * Q_BLOCK
    q = q_ref[...] * (ATTN_SCALE * LOG2E)
    lane = lax.broadcasted_iota(jnp.int32, (Q_BLOCK, LANES), 1)
    lo_half = lane < HEAD_DIM
    t_lane = c * Q_BLOCK + (lax.broadcasted_iota(jnp.int32, (1, rows), 1) & (Q_BLOCK - 1))
    n_idx = lax.broadcasted_iota(jnp.int32, (ncmp, 1), 0)
    neg = jnp.where((n_idx * CMP_STRIDE + (CMP_BLOCK - 1)) <= t_lane, 0.0, NEG_INF)
    jb = lax.broadcasted_iota(jnp.int32, (nblk, 1), 0)
    t_row = c * Q_BLOCK + lax.broadcasted_iota(jnp.int32, (1, Q_BLOCK), 1)
    cur = lax.shift_right_logical(t_row, 6)
    forced = (jb == 0) | (jb == cur) | (jb == cur - 1)
    causal = jb * SLC_BLOCK <= t_row
    vals = []
    for hk in range(NSA_KV_HEADS):
        qts = []
        for g in range(NSA_GROUP):
            hq = hk * NSA_GROUP + g
            pair = q[:, (hq // 2) * LANES:(hq // 2 + 1) * LANES]
            qm = jnp.where(lo_half if hq % 2 == 0 else jnp.logical_not(lo_half), pair, 0.0).astype(BF16)
            qts.append(_dot_nt(eye_ref[...], qm).astype(BF16))
        s = _dot(kc_ref[0, hk], jnp.concatenate(qts, axis=1)) + neg
        m = jnp.max(s, axis=0, keepdims=True)
        e = jnp.exp2(s - m)
        l = jnp.sum(e, axis=0, keepdims=True)
        p = e * jnp.where(m > 0.5 * NEG_INF, 1.0 / l, 0.0)
        o_t = _dot(vct_ref[0, hk], p.astype(BF16))
        _emit_heads(o_t, pe_ref, po_ref, ocmp_ref, hk * (NSA_GROUP // 2))
        psum = p[:, 0:Q_BLOCK]
        for g in range(1, NSA_GROUP):
            psum = psum + p[:, g * Q_BLOCK:(g + 1) * Q_BLOCK]
        ph, plo = _split2(psum)
        imp = _dot(ovt_ref[...], ph) + _dot(ovt_ref[...], plo)
        vals.append(jnp.where(causal, imp + jnp.where(forced, FORCE_SELECT, 0.0), NEG_INF))
    val0 = jnp.concatenate(vals, axis=1)
    val = val0
    for _ in range(n_sel):
        m = jnp.max(val, axis=0, keepdims=True)
        idx = jnp.min(jnp.where(val == m, jb, nblk), axis=0, keepdims=True)
        val = jnp.where(jb == idx, TAKEN, val)
    bias = jnp.where((val == TAKEN) & (val0 > 0.5 * NEG_INF), 0.0, NEG_INF).astype(BF16)
    for hk in range(NSA_KV_HEADS):
        bias_ref[0, hk] = bias[:, hk * Q_BLOCK:(hk + 1) * Q_BLOCK]


def _select(h, kcmp, vcmp_t, batch):
    n = h.shape[0]
    t = n // batch
    nq = t // Q_BLOCK
    ncmp = kcmp.shape[-2]
    nblk = t // SLC_BLOCK
    eye_q, pe, po = _head_placers()
    ii = np.arange(ncmp)[None, :]
    jj = np.arange(nblk)[:, None]
    ovt = ((ii * CMP_STRIDE < (jj + 1) * SLC_BLOCK) & (ii * CMP_STRIDE + CMP_BLOCK > jj * SLC_BLOCK)
           & (ii < ncmp - 1))
    ovt = jnp.asarray(ovt.astype(np.float32), dtype=BF16)
    full = lambda a: pl.BlockSpec(a.shape, lambda b, i: (0,) * a.ndim)
    per_batch = lambda a: pl.BlockSpec((1,) + a.shape[1:], lambda b, i: (b,) + (0,) * (a.ndim - 1))
    return pl.pallas_call(
        functools.partial(_select_body, n_sel=min(N_SLC, nblk)),
        grid=(batch, nq),
        in_specs=[pl.BlockSpec((Q_BLOCK, NSA_WIDTH), lambda b, i: (b * nq + i, COL_NQ // NSA_WIDTH)),
                  per_batch(kcmp), per_batch(vcmp_t), full(ovt), full(eye_q), full(pe), full(po)],
        out_specs=[pl.BlockSpec((Q_BLOCK, NSA_WIDTH), lambda b, i: (b * nq + i, 0)),
                   pl.BlockSpec((1, NSA_KV_HEADS, nblk, Q_BLOCK), lambda b, i: (b, 0, 0, i))],
        out_shape=[jax.ShapeDtypeStruct((n, NSA_WIDTH), F32),
                   jax.ShapeDtypeStruct((batch, NSA_KV_HEADS, nblk, t), BF16)],
        compiler_params=_params("parallel", "parallel"),
        name="nsa_select",
    )(h, kcmp, vcmp_t, ovt, eye_q, pe, po)


def _softmax_tile_t(s, m_prev):
    m_new = jnp.maximum(m_prev, jnp.max(s, axis=0, keepdims=True))
    return m_new, jnp.exp2(s - m_new).astype(BF16)


def _attn_body(q_ref, bias_ref, kaug_ref, vst_ref, kw_ref, vwt_ref, eye_ref, pe_ref, po_ref, oslc_ref, owin_ref,
               qaug_sc, acc_sc, sa_sc, sb_sc):
    c = pl.program_id(2)
    rows = NSA_GROUP * Q_BLOCK
    q4 = q_ref[...] * (ATTN_SCALE * LOG2E)
    lane = lax.broadcasted_iota(jnp.int32, (Q_BLOCK, LANES), 1)
    lo_half = lane < HEAD_DIM
    bias = bias_ref[0, 0]
    for g in range(NSA_GROUP):
        pair = q4[:, (g // 2) * LANES:(g // 2 + 1) * LANES]
        qm = jnp.where(lo_half if g % 2 == 0 else jnp.logical_not(lo_half), pair, 0.0).astype(BF16)
        qaug_sc[0:LANES, g * Q_BLOCK:(g + 1) * Q_BLOCK] = _dot_nt(eye_ref[...], qm).astype(BF16)
        qaug_sc[LANES:, g * Q_BLOCK:(g + 1) * Q_BLOCK] = bias
    t_row = c * Q_BLOCK + (lax.broadcasted_iota(jnp.int32, (1, rows), 1) & (Q_BLOCK - 1))

    acc_sc[...] = jnp.zeros(acc_sc.shape, F32)
    last_tile = kaug_ref.shape[2] // ATTN_TK - 1

    def scores(tile, s_ref, masked):
        start = pl.multiple_of(jnp.minimum(tile, last_tile) * ATTN_TK, ATTN_TK)
        s = _dot(kaug_ref[0, 0, pl.ds(start, ATTN_TK), :], qaug_sc[...])
        if masked:
            kpos = tile * ATTN_TK + lax.broadcasted_iota(jnp.int32, (ATTN_TK, 1), 0)
            s = jnp.where(kpos <= t_row, s, NEG_INF)
        s_ref[...] = s
        return jnp.max(s, axis=0, keepdims=True)

    def accumulate(tile, m_prev, m_tile, s_ref):
        m_new = jnp.maximum(m_prev, m_tile)
        p = jnp.exp2(s_ref[...] - m_new).astype(BF16)
        acc_sc[...] = (jnp.exp2(m_prev - m_new) * acc_sc[...]
                       + _dot(vst_ref[0, 0, jnp.minimum(tile, last_tile)], p))
        return m_new

    def two_tiles(t0, carry, masked):
        m, mt_a = carry
        mt_b = scores(t0 + 1, sb_sc, masked)
        m = accumulate(t0, m, mt_a, sa_sc)
        mt_a = scores(t0 + 2, sa_sc, masked)
        m = accumulate(t0 + 1, m, mt_b, sb_sc)
        return m, mt_a

    n_full = (c * Q_BLOCK) // ATTN_TK
    n_loop = jnp.maximum(n_full - 1, 0) // 2
    carry = (jnp.full((1, rows), TAKEN, F32), scores(0, sa_sc, True))
    carry = lax.fori_loop(0, n_loop, lambda i, cr: two_tiles(2 * i, cr, False), carry)
    m, mt_a = two_tiles(2 * n_loop, carry, True)
    accumulate(2 * n_loop + 2, m, mt_a, sa_sc)
    acc = acc_sc[...]
    _emit_heads(acc[0:HEAD_DIM] / acc[HEAD_DIM:HEAD_DIM + 1], pe_ref, po_ref, oslc_ref)

    span = WINDOW + Q_BLOCK
    wblk = jnp.maximum(c - WINDOW // Q_BLOCK, 0)
    wstart = pl.multiple_of(wblk * Q_BLOCK, Q_BLOCK)
    s = _dot(kw_ref[0, 0, pl.ds(wstart, span), :], qaug_sc[0:LANES, :])
    kpos = wstart + lax.broadcasted_iota(jnp.int32, (span, 1), 0)
    ok = (kpos <= t_row) & (kpos > t_row - WINDOW)
    _, p = _softmax_tile_t(jnp.where(ok, s, NEG_INF), jnp.full((1, rows), TAKEN, F32))
    acc = _dot(vwt_ref[0, 0, wblk], p[0:Q_BLOCK])
    for i in range(1, span // Q_BLOCK):
        acc = acc + _dot(vwt_ref[0, 0, wblk + i], p[i * Q_BLOCK:(i + 1) * Q_BLOCK])
    _emit_heads(acc[0:HEAD_DIM] / acc[HEAD_DIM:HEAD_DIM + 1], pe_ref, po_ref, owin_ref)


def _attention(h, bias_t, kaug, vst, kw, vwt, batch):
    n = h.shape[0]
    t = n // batch
    nq = t // Q_BLOCK
    gw = NSA_GROUP * HEAD_DIM
    rows = NSA_GROUP * Q_BLOCK
    eye_q, pe, po = _head_placers()
    res = lambda a: pl.BlockSpec((1, 1) + a.shape[2:], lambda b, k, i: (b, k) + (0,) * (a.ndim - 2))
    full = lambda a: pl.BlockSpec(a.shape, lambda b, k, i: (0,) * a.ndim)
    out_spec = pl.BlockSpec((Q_BLOCK, gw), lambda b, k, i: (b * nq + i, k))
    return pl.pallas_call(
        _attn_body,
        grid=(batch, NSA_KV_HEADS, nq),
        in_specs=[pl.BlockSpec((Q_BLOCK, gw), lambda b, k, i: (b * nq + i, COL_NQ // gw + k)),
                  pl.BlockSpec((1, 1, bias_t.shape[2], Q_BLOCK), lambda b, k, i: (b, k, 0, i)),
                  res(kaug), res(vst), res(kw), res(vwt), full(eye_q), full(pe), full(po)],
        out_specs=[out_spec, out_spec],
        out_shape=[jax.ShapeDtypeStruct((n, NSA_WIDTH), F32)] * 2,
        scratch_shapes=[pltpu.VMEM((kaug.shape[-1], rows), BF16), pltpu.VMEM((V_ROWS, rows), F32),
                        pltpu.VMEM((ATTN_TK, rows), F32), pltpu.VMEM((ATTN_TK, rows), F32)],
        compiler_params=_params("parallel", "parallel", "arbitrary"),
        name="nsa_attention",
    )(h, bias_t, kaug, vst, kw, vwt, eye_q, pe, po)


def _branch_expanders():
    e = np.zeros((N_BRANCH, GATE_PAD, NSA_WIDTH), np.float32)
    for hq in range(NSA_HEADS):
        for br in range(N_BRANCH):
            e[br, hq * N_BRANCH + br, hq * HEAD_DIM:(hq + 1) * HEAD_DIM] = 1.0
    return jnp.asarray(e, dtype=BF16)


def _route(logits):
    lane = lax.broadcasted_iota(jnp.int32, logits.shape, 1)
    is_g = lane < N_GROUPS
    gl = jnp.where(is_g, logits, -jnp.inf)
    gmax = jnp.max(gl, axis=-1, keepdims=True)
    gsum = jnp.sum(jnp.where(is_g, jnp.exp(logits - gmax), 0.0), axis=-1, keepdims=True)
    g_val = 1.0 / gsum
    g_idx = jnp.min(jnp.where(is_g & (logits == gmax), lane, ROUTER_PAD), axis=-1, keepdims=True)
    e_lo = EXPERT_LANE0 + g_idx * EXPERTS_PER_GROUP
    in_sel = (lane >= e_lo) & (lane < e_lo + EXPERTS_PER_GROUP)
    el = jnp.where(in_sel, logits, -jnp.inf)
    m1 = jnp.max(el, axis=-1, keepdims=True)
    i1 = jnp.min(jnp.where(in_sel & (logits == m1), lane, ROUTER_PAD), axis=-1, keepdims=True)
    el2 = jnp.where(lane == i1, -jnp.inf, el)
    m2 = jnp.max(el2, axis=-1, keepdims=True)
    i2 = jnp.min(jnp.where(el2 == m2, lane, ROUTER_PAD), axis=-1, keepdims=True)
    r = jnp.exp(m2 - m1)
    w1 = 1.0 / (1.0 + r)
    w2 = r * w1
    return g_val * (jnp.where(lane == i1, w1, 0.0) + jnp.where(lane == i2, w2, 0.0))


def _out_body(ya_ref, ocmp_ref, oslc_ref, owin_ref, gl_ref, yc_ref, hg_ref, x_ref, gb_ref, ex_ref,
              gmb_ref, gmc_ref, og_ref, w_ref, lg_ref, lb_ref, wrh_ref, wrl_ref, br_ref, x1_ref, gate_ref):
    g = _sigmoid(gl_ref[...] + gb_ref[...])
    yb = (_dot2(g, ex_ref[0]) * ocmp_ref[...] + _dot2(g, ex_ref[1]) * oslc_ref[...]
          + _dot2(g, ex_ref[2]) * owin_ref[...])
    n_ab = GM_WIDTH + NSA_WIDTH
    ybn = yb * lax.rsqrt(_dot2(yb * yb, gmb_ref[...]) + RMS_EPS) * og_ref[:, GM_WIDTH:n_ab]
    yc = yc_ref[...]
    hg = hg_ref[...]
    ycn = yc * lax.rsqrt(_dot2(yc * yc, gmc_ref[...]) + RMS_EPS) * og_ref[:, n_ab:] * (hg * _sigmoid(hg))
    y = (_dot(ya_ref[...].astype(BF16), w_ref[0:GM_WIDTH, :]) + _dot(ybn.astype(BF16), w_ref[GM_WIDTH:n_ab, :])
         + _dot(ycn.astype(BF16), w_ref[n_ab:, :]))
    x1 = _layer_norm(DEEPNORM_ALPHA * x_ref[...] + y, lg_ref[...], lb_ref[...])
    x1_ref[...] = x1
    xh, xl = _split2(x1)
    logits = _dot(xh, wrh_ref[...]) + _dot(xh, wrl_ref[...]) + _dot(xl, wrh_ref[...]) + br_ref[...]
    gate_ref[...] = _route(logits)


def _outproj(ya, ocmp, oslc, owin, h, yc, x2, gate_b, out_gain, w_out, ln_g, ln_b, wr, br):
    n, d = x2.shape
    tm = 256
    row = lambda a: a.reshape(1, -1)
    gb = jnp.pad(gate_b, (0, GATE_PAD - gate_b.shape[0])).reshape(1, GATE_PAD)
    wrh = wr.astype(BF16)
    wrl = (wr - wrh.astype(F32)).astype(BF16)
    ex = _branch_expanders()
    full = lambda shape: pl.BlockSpec(shape, lambda i: (0,) * len(shape))
    tile = lambda w, cb=0: pl.BlockSpec((tm, w), lambda i: (i, cb))
    return pl.pallas_call(
        _out_body,
        grid=(n // tm,),
        in_specs=[tile(GM_WIDTH), tile(NSA_WIDTH), tile(NSA_WIDTH), tile(NSA_WIDTH),
                  tile(GATE_PAD, COL_NG // GATE_PAD), tile(HG_WIDTH), tile(HG_WIDTH, COL_HG // HG_WIDTH), tile(d),
                  full((1, GATE_PAD)), full(ex.shape), full((NSA_WIDTH, NSA_WIDTH)), full((HG_WIDTH, HG_WIDTH)),
                  full((1, d)), full((d, d)), full((1, d)), full((1, d)),
                  full((d, ROUTER_PAD)), full((d, ROUTER_PAD)), full((1, ROUTER_PAD))],
        out_specs=[tile(d), tile(ROUTER_PAD)],
        out_shape=[jax.ShapeDtypeStruct((n, d), F32), jax.ShapeDtypeStruct((n, ROUTER_PAD), F32)],
        compiler_params=_params("parallel"),
        name="outproj_ln_router",
    )(ya, ocmp, oslc, owin, h, yc, h, x2, gb, ex, _group_mean_matrix(NSA_WIDTH), _group_mean_matrix(HG_WIDTH),
      row(out_gain), w_out.astype(BF16), row(ln_g), row(ln_b), wrh, wrl, br)


def _moe_body(x_ref, gate_ref, wg_ref, wu_ref, wd_ref, lg_ref, lb_ref, o_ref,
              xb_sc, acc_sc, slot_sc, slott_sc, cnt_sc):
    e = pl.program_id(1)
    tm = x_ref.shape[0]

    @pl.when(e == 0)
    def _():
        xb_sc[...] = x_ref[...].astype(BF16)
        acc_sc[...] = jnp.zeros_like(acc_sc)
        used = jnp.where(gate_ref[...] > 0.0, 1.0, 0.0)
        r = lax.broadcasted_iota(jnp.int32, (tm, tm), 0)
        q = lax.broadcasted_iota(jnp.int32, (tm, tm), 1)
        before = jnp.where(q < r, 1.0, 0.0).astype(BF16)
        after = jnp.where(r < q, 1.0, 0.0).astype(BF16)
        ub = used.astype(BF16)
        slot_sc[...] = jnp.where(used > 0.0, _dot(before, ub), -1.0)
        ri = lax.broadcasted_iota(jnp.int32, (ROUTER_PAD, ROUTER_PAD), 0)
        qi = lax.broadcasted_iota(jnp.int32, (ROUTER_PAD, ROUTER_PAD), 1)
        used_t = _dot_nt(jnp.where(ri == qi, 1.0, 0.0).astype(BF16), ub)
        slott_sc[...] = jnp.where(used_t > 0.0, _dot(used_t.astype(BF16), after), -1.0)
        cnt_sc[...] = jnp.broadcast_to(jnp.sum(used, axis=0, keepdims=True), cnt_sc.shape)

    lane = lax.broadcasted_iota(jnp.int32, gate_ref.shape, 1)
    mine = lane == e + EXPERT_LANE0
    gcol = jnp.sum(jnp.where(mine, gate_ref[...], 0.0), axis=-1, keepdims=True)
    scol = jnp.sum(jnp.where(mine, slot_sc[...], 0.0), axis=-1, keepdims=True)
    srow = slott_sc[pl.ds(e + EXPERT_LANE0, 1), :]
    n_rows = jnp.sum(jnp.where(mine[0:1], cnt_sc[0:1], 0.0)).astype(jnp.int32)

    def chunk(ci, carry):
        base = (ci * MOE_CHUNK).astype(F32)
        r_col = base + lax.broadcasted_iota(jnp.int32, (MOE_CHUNK, 1), 0).astype(F32)
        r_row = base + lax.broadcasted_iota(jnp.int32, (1, MOE_CHUNK), 1).astype(F32)
        pick = jnp.where(srow == r_col, 1.0, 0.0).astype(BF16)
        xg = _dot(pick, xb_sc[...]).astype(BF16)
        hg = _dot(xg, wg_ref[0])
        hu = _dot(xg, wu_ref[0])
        y = _dot((hg * _sigmoid(hg) * hu).astype(BF16), wd_ref[0]).astype(BF16)
        place = jnp.where(scol == r_row, 1.0, 0.0).astype(BF16)
        acc_sc[...] += gcol * _dot(place, y)
        return carry

    lax.fori_loop(0, (n_rows + MOE_CHUNK - 1) // MOE_CHUNK, chunk, 0)

    @pl.when(e == pl.num_programs(1) - 1)
    def _():
        o_ref[...] = _layer_norm(DEEPNORM_ALPHA * x_ref[...] + acc_sc[...], lg_ref[...], lb_ref[...])


def _moe(x1, gate, wg, wu, wd, ln_g, ln_b):
    n, d = x1.shape
    ne, _, de = wg.shape
    tm = MOE_TM
    row = lambda a: a.reshape(1, -1)
    return pl.pallas_call(
        _moe_body,
        grid=(n // tm, ne),
        in_specs=[pl.BlockSpec((tm, d), lambda i, e: (i, 0)),
                  pl.BlockSpec((tm, ROUTER_PAD), lambda i, e: (i, 0)),
                  pl.BlockSpec((1, d, de), lambda i, e: (e, 0, 0)),
                  pl.BlockSpec((1, d, de), lambda i, e: (e, 0, 0)),
                  pl.BlockSpec((1, de, d), lambda i, e: (e, 0, 0)),
                  pl.BlockSpec((1, d), lambda i, e: (0, 0)),
                  pl.BlockSpec((1, d), lambda i, e: (0, 0))],
        out_specs=pl.BlockSpec((tm, d), lambda i, e: (i, 0)),
        out_shape=jax.ShapeDtypeStruct((n, d), F32),
        scratch_shapes=[pltpu.VMEM((tm, d), BF16), pltpu.VMEM((tm, d), F32),
                        pltpu.VMEM((tm, ROUTER_PAD), F32), pltpu.VMEM((ROUTER_PAD, tm), F32),
                        pltpu.VMEM((8, ROUTER_PAD), F32)],
        compiler_params=_params("parallel", "arbitrary"),
        name="moe_ln",
    )(x1, gate, wg, wu, wd, row(ln_g), row(ln_b))


def _prep_w_in(w_in):
    names = ("gu", "gv", "nq", "kc", "vc", "ks", "vs", "kw", "vw", "ng", "hq", "hf", "hi", "hg")
    widths = (GM_WIDTH, GM_WIDTH, NSA_WIDTH) + (NSA_KV_WIDTH,) * 6 + (N_BRANCH * NSA_HEADS,) + (HG_WIDTH,) * 4
    ends = np.cumsum(widths)
    col = {nm: w_in[:, int(e - wd):int(e)] for nm, e, wd in zip(names, ends, widths)}
    pad = jnp.zeros((w_in.shape[0], GATE_PAD - N_BRANCH * NSA_HEADS), w_in.dtype)
    heads = lambda a: [a[:, i * HEAD_DIM:(i + 1) * HEAD_DIM] for i in range(NSA_KV_HEADS)]
    kdup = [a for k in (col["ks"], col["kw"]) for hd in heads(k) for a in (hd, hd)]
    w = jnp.concatenate([col[nm] for nm in ("gu", "gv", "nq", "kc", "vc", "hq", "hf", "hi", "hg", "ng")]
                        + [pad] + kdup, axis=1)
    wvt = jnp.concatenate([col["vs"], col["vw"]], axis=1).T
    return w.astype(BF16), wvt.astype(BF16)


def _layer(x2, batch, w_in, gm_v_gain, gm_v_bias, gm_w_s, gm_b_s, cmp_pos, cmp_w1, cmp_w2, nsa_gate_b, lb,
           out_gain, w_out, ln1_g, ln1_b, wr, br, wg, wu, wd, ln2_g, ln2_b):
    n = x2.shape[0]
    t = n // batch
    h, kaug, kw, vst, vwt = _inproj(x2, *_prep_w_in(w_in), batch)

    ya = _gmlp(h, gm_v_gain, gm_v_bias, gm_w_s, gm_b_s, out_gain[:GM_WIDTH])
    yc = _hgrn(h, lb, batch)

    kvc = h[:, COL_KC:COL_KC + 2 * NSA_KV_WIDTH].reshape(batch, t, 2, NSA_KV_HEADS, HEAD_DIM)
    kvc = kvc.transpose(0, 2, 3, 1, 4).reshape(batch, 2, NSA_KV_HEADS, t // CMP_STRIDE, CMP_STRIDE * HEAD_DIM)
    kcmp, vcmp_t = _compress(kvc, cmp_w1, cmp_w2, cmp_pos.reshape(2, 1, CMP_BLOCK * HEAD_DIM))
    ocmp, bias_t = _select(h, kcmp, vcmp_t, batch)
    oslc, owin = _attention(h, bias_t, kaug, vst, kw, vwt, batch)

    x1, gate = _outproj(ya, ocmp, oslc, owin, h, yc, x2, nsa_gate_b, out_gain, w_out, ln1_g, ln1_b, wr, br)
    return _moe(x1, gate, wg, wu, wd, ln2_g, ln2_b)


def kernel(x, w_in, gm_v_gain, gm_v_bias, gm_w_s, gm_b_s, cmp_pos, cmp_w1, cmp_w2, nsa_gate_b, hg_lower, out_gain, w_out, ln1_g, ln1_b, router_group_w, router_group_b, router_expert_w, router_expert_b, exp_w_gate, exp_w_up, exp_w_down, ln2_g, ln2_b):
    batch, t, d = x.shape
    depth = w_in.shape[0]
    lb_all = jnp.cumsum(jax.nn.softmax(hg_lower.astype(F32), axis=0), axis=0)
    lb_all = lb_all - lb_all[0]
    x2 = x.reshape(batch * t, d)
    for l in range(depth):
        pad = ROUTER_PAD - N_GROUPS - N_EXPERTS
        wr = jnp.concatenate([router_group_w[l], router_expert_w[l], jnp.zeros((d, pad), F32)], axis=1)
        br = jnp.concatenate([router_group_b[l], router_expert_b[l], jnp.zeros((pad,), F32)]).reshape(1, ROUTER_PAD)
        x2 = _layer(x2, batch, w_in[l], gm_v_gain[l], gm_v_bias[l], gm_w_s[l], gm_b_s[l], cmp_pos[l], cmp_w1[l],
                    cmp_w2[l], nsa_gate_b[l], lb_all[l], out_gain[l], w_out[l], ln1_g[l], ln1_b[l], wr, br,
                    exp_w_gate[l].astype(BF16), exp_w_up[l].astype(BF16), exp_w_down[l].astype(BF16),
                    ln2_g[l], ln2_b[l])
    return x2.reshape(batch, t, d)
```

```python
import functools

import numpy as np
import jax
import jax.numpy as jnp
from jax import lax
from jax.experimental import pallas as pl
from jax.experimental.pallas import tpu as pltpu

F32 = jnp.float32
BF16 = jnp.bfloat16

D_MODEL = 1024
HEAD_DIM = 64
GM_GROUPS = 4
GM_WIDTH = GM_GROUPS * HEAD_DIM
GM_CHUNK = 128
NSA_HEADS = 8
NSA_KV_HEADS = 2
NSA_GROUP = NSA_HEADS // NSA_KV_HEADS
NSA_WIDTH = NSA_HEADS * HEAD_DIM
NSA_KV_WIDTH = NSA_KV_HEADS * HEAD_DIM
CMP_BLOCK = 32
CMP_STRIDE = 16
SLC_BLOCK = 64
N_SLC = 16
WINDOW = 512
Q_BLOCK = 128
N_BRANCH = 3
HG_HEADS = 4
HG_WIDTH = HG_HEADS * HEAD_DIM
HG_CHUNK = 64
N_GROUPS = 4
EXPERTS_PER_GROUP = 4
N_EXPERTS = N_GROUPS * EXPERTS_PER_GROUP
D_EXPERT = 512
DEPTH = 2
DEEPNORM_ALPHA = (2.0 * DEPTH) ** 0.25
LN_EPS = 1e-5
RMS_EPS = 1e-6
NEG_INF = -1e30
FORCE_SELECT = 1e4
TAKEN = -(2.0 ** 126)
ATTN_SCALE = HEAD_DIM ** -0.5

LANES = 128
GATE_PAD = LANES
ROUTER_PAD = LANES
EXPERT_LANE0 = N_GROUPS
VMEM_LIMIT = 56 * 1024 * 1024

COL_GU = 0
COL_GV = 256
COL_NQ = 512
COL_KC = 1024
COL_HQ = 1280
COL_HF = 1536
COL_HI = 1792
COL_HG = 2048
COL_NG = 2304
H_COLS = COL_NG + GATE_PAD
KDUP_COLS = 4 * NSA_KV_WIDTH

ATTN_TK = 512
V_ROWS = HEAD_DIM + 16
LOG2E = 1.4426950408889634
MOE_TM = 1024
MOE_CHUNK = 160


def _dot(a, b):
    return jnp.dot(a, b, preferred_element_type=F32)


def _dot_nt(a, b):
    return lax.dot_general(a, b, (((1,), (1,)), ((), ())), preferred_element_type=F32)


def _dot_tn(a, b):
    return lax.dot_general(a, b, (((0,), (0,)), ((), ())), preferred_element_type=F32)


def _split2(x):
    hi = x.astype(BF16)
    lo = (x - hi.astype(F32)).astype(BF16)
    return hi, lo


def _dot2(x, m, lhs=False):
    hi, lo = _split2(x)
    return _dot(m, hi) + _dot(m, lo) if lhs else _dot(hi, m) + _dot(lo, m)


def _gelu(x):
    return 0.5 * x * (1.0 + jnp.tanh(0.7978845608028654 * (x + 0.044715 * (x * x * x))))


def _sigmoid(x):
    return 1.0 / (1.0 + jnp.exp(-x))


def _log_sigmoid(x):
    return jnp.minimum(x, 0.0) - jnp.log1p(jnp.exp(-jnp.abs(x)))


def _layer_norm(z, g, b):
    mu = jnp.mean(z, axis=-1, keepdims=True)
    zc = z - mu
    var = jnp.mean(zc * zc, axis=-1, keepdims=True)
    return zc * lax.rsqrt(var + LN_EPS) * g + b


def _params(*sem):
    return pltpu.CompilerParams(dimension_semantics=sem, vmem_limit_bytes=VMEM_LIMIT)


def _inproj_body(x_ref, w_ref, wvt_ref, h_ref, kaug_ref, kw_ref, vst_ref, vwt_ref):
    xb = x_ref[...].astype(BF16)
    tm = xb.shape[0]
    acc = _dot(xb, w_ref[...])
    h_ref[...] = acc[:, :H_COLS]
    nblk = kaug_ref.shape[-1] - LANES
    pos = pl.program_id(1) * tm + lax.broadcasted_iota(jnp.int32, (tm, nblk), 0)
    blk = lax.broadcasted_iota(jnp.int32, (tm, nblk), 1)
    onehot = jnp.where(lax.shift_right_logical(pos, 6) == blk, 1.0, 0.0).astype(BF16)
    for hk in range(NSA_KV_HEADS):
        c0 = H_COLS + hk * LANES
        kaug_ref[0, hk, :, 0:LANES] = acc[:, c0:c0 + LANES].astype(BF16)
        kaug_ref[0, hk, :, LANES:] = onehot
        c1 = H_COLS + (NSA_KV_HEADS + hk) * LANES
        kw_ref[0, hk] = acc[:, c1:c1 + LANES].astype(BF16)
    vt = _dot_nt(wvt_ref[...], xb)
    ones_then_zeros = lambda w: jnp.where(
        lax.broadcasted_iota(jnp.int32, (V_ROWS - HEAD_DIM, w), 0) == 0, 1.0, 0.0).astype(BF16)
    tail, tail_q = ones_then_zeros(tm), ones_then_zeros(Q_BLOCK)
    for hk in range(NSA_KV_HEADS):
        vst_ref[0, hk, 0, 0:HEAD_DIM, :] = vt[hk * HEAD_DIM:(hk + 1) * HEAD_DIM].astype(BF16)
        vst_ref[0, hk, 0, HEAD_DIM:, :] = tail
        r0 = (NSA_KV_HEADS + hk) * HEAD_DIM
        for i in range(tm // Q_BLOCK):
            ls = slice(i * Q_BLOCK, (i + 1) * Q_BLOCK)
            vwt_ref[0, hk, i, 0:HEAD_DIM, :] = vt[r0:r0 + HEAD_DIM, ls].astype(BF16)
            vwt_ref[0, hk, i, HEAD_DIM:, :] = tail_q


def _inproj(x2, w, wvt, batch):
    n, d = x2.shape
    t = n // batch
    tm = ATTN_TK
    steps = t // tm
    nblk = t // SLC_BLOCK
    hk = NSA_KV_HEADS
    return pl.pallas_call(
        _inproj_body,
        grid=(batch, steps),
        in_specs=[pl.BlockSpec((tm, d), lambda b, j: (b * steps + j, 0)),
                  pl.BlockSpec(w.shape, lambda b, j: (0, 0)),
                  pl.BlockSpec(wvt.shape, lambda b, j: (0, 0))],
        out_specs=[pl.BlockSpec((tm, H_COLS), lambda b, j: (b * steps + j, 0)),
                   pl.BlockSpec((1, hk, tm, LANES + nblk), lambda b, j: (b, 0, j, 0)),
                   pl.BlockSpec((1, hk, tm, LANES), lambda b, j: (b, 0, j, 0)),
                   pl.BlockSpec((1, hk, 1, V_ROWS, tm), lambda b, j: (b, 0, j, 0, 0)),
                   pl.BlockSpec((1, hk, tm // Q_BLOCK, V_ROWS, Q_BLOCK), lambda b, j: (b, 0, j, 0, 0))],
        out_shape=[jax.ShapeDtypeStruct((n, H_COLS), F32),
                   jax.ShapeDtypeStruct((batch, hk, t, LANES + nblk), BF16),
                   jax.ShapeDtypeStruct((batch, hk, t, LANES), BF16),
                   jax.ShapeDtypeStruct((batch, hk, steps, V_ROWS, tm), BF16),
                   jax.ShapeDtypeStruct((batch, hk, t // Q_BLOCK, V_ROWS, Q_BLOCK), BF16)],
        compiler_params=_params("parallel", "parallel"),
        name="inproj",
    )(x2, w, wvt)


def _group_mean_matrix(width):
    g = np.arange(width) // HEAD_DIM
    return jnp.asarray((g[:, None] == g[None, :]).astype(np.float32) / HEAD_DIM, dtype=BF16)


def _gmlp_body(gu_ref, gv_ref, gain_ref, bias_ref, ws_ref, bsx_ref, gm_ref, og_ref, o_ref, *, chunks):
    gm = gm_ref[...]
    row = lax.broadcasted_iota(jnp.int32, (GM_CHUNK, GM_CHUNK), 0)
    col = lax.broadcasted_iota(jnp.int32, (GM_CHUNK, GM_CHUNK), 1)
    tril = row >= col
    lane_grp = lax.shift_right_logical(lax.broadcasted_iota(jnp.int32, (GM_CHUNK, GM_WIDTH), 1), 6)
    ws = [jnp.where(tril, ws_ref[g], 0.0).astype(BF16) for g in range(GM_GROUPS)]
    for c in range(chunks):
        sl = slice(c * GM_CHUNK, (c + 1) * GM_CHUNK)
        u = _gelu(gu_ref[sl, :])
        v = _gelu(gv_ref[sl, :])
        mu = _dot2(v, gm)
        vc = v - mu
        var = _dot2(vc * vc, gm)
        vn = (vc * lax.rsqrt(var + LN_EPS) * gain_ref[...] + bias_ref[...]).astype(BF16)
        z = bsx_ref[...]
        for g in range(GM_GROUPS):
            z = z + jnp.where(lane_grp == g, _dot(ws[g], vn), 0.0)
        y = u * z
        ms = _dot2(y * y, gm)
        o_ref[sl, :] = y * lax.rsqrt(ms + RMS_EPS) * og_ref[...]


def _gmlp(h, gain, bias, w_s, b_s, out_gain_a):
    n = h.shape[0]
    chunks = 4
    tm = chunks * GM_CHUNK
    bsx = jnp.repeat(b_s.T, HEAD_DIM, axis=1)
    row = lambda a: a.reshape(1, -1)
    full = lambda shape: pl.BlockSpec(shape, lambda i: (0,) * len(shape))
    return pl.pallas_call(
        functools.partial(_gmlp_body, chunks=chunks),
        grid=(n // tm,),
        in_specs=[
            pl.BlockSpec((tm, GM_WIDTH), lambda i: (i, COL_GU // GM_WIDTH)),
            pl.BlockSpec((tm, GM_WIDTH), lambda i: (i, COL_GV // GM_WIDTH)),
            full((1, GM_WIDTH)), full((1, GM_WIDTH)),
            full((GM_GROUPS, GM_CHUNK, GM_CHUNK)), full((GM_CHUNK, GM_WIDTH)),
            full((GM_WIDTH, GM_WIDTH)), full((1, GM_WIDTH)),
        ],
        out_specs=pl.BlockSpec((tm, GM_WIDTH), lambda i: (i, 0)),
        out_shape=jax.ShapeDtypeStruct((n, GM_WIDTH), F32),
        compiler_params=_params("parallel"),
        name="gmlp",
    )(h, h, row(gain), row(bias), w_s, bsx, _group_mean_matrix(GM_WIDTH), row(out_gain_a))


N_LEVELS = 6


def _hgrn_constants():
    c = HG_CHUNK
    t = np.arange(c)[:, None]
    u = np.arange(c)[None, :]
    mats, masks = [], []
    m = c // 2
    while m >= 1:
        p = (t // (2 * m)) * (2 * m) + m - 1
        mats.append(np.where(t > p, (u > p) & (u <= t), (u > t) & (u <= p)))
        masks.append(((t // (2 * m)) == (u // (2 * m))) & ((t % (2 * m)) >= m) & ((u % (2 * m)) < m))
        m //= 2
    mats.append(u <= t)
    mats.append(u > t)
    masks.append(np.eye(c, dtype=bool))
    a_all = np.concatenate(mats, 0).astype(np.float32)
    mask2 = np.stack([np.concatenate([mk, mk], 0) for mk in masks]).astype(np.float32)
    return jnp.asarray(a_all, dtype=BF16), jnp.asarray(mask2)


def _hgrn_body(q_ref, f_ref, i_ref, loglb_ref, log1m_ref, oml_ref, a_ref, mk_ref, o_ref, st_ref, *, chunks):
    @pl.when(pl.program_id(1) == 0)
    def _():
        st_ref[...] = jnp.zeros_like(st_ref)

    c = HG_CHUNK
    a_all = a_ref[...]
    lane = lax.broadcasted_iota(jnp.int32, (c, LANES), 1)
    lo_half = lane < HEAD_DIM
    r2 = lax.broadcasted_iota(jnp.int32, (LANES, LANES), 0)
    c2 = lax.broadcasted_iota(jnp.int32, (LANES, LANES), 1)
    same_head = (r2 < HEAD_DIM) == (c2 < HEAD_DIM)
    fl = f_ref[...]
    la = loglb_ref[...]
    lb2 = log1m_ref[...] + _log_sigmoid(fl)
    lf_all = jnp.maximum(la, lb2) + jnp.log1p(jnp.exp(-jnp.abs(la - lb2)))
    k_all = oml_ref[...] * (1.0 / (1.0 + jnp.exp(fl)))
    e_alls = []
    for ch in range(chunks):
        d_all = _dot2(lf_all[ch * c:(ch + 1) * c], a_all, lhs=True)
        e_alls.append(jnp.exp(d_all))
    pending = []
    for ch in range(chunks):
        sl = slice(ch * c, (ch + 1) * c)
        q, k, iv, e_all = q_ref[sl, :], k_all[sl], i_ref[sl, :], e_alls[ch]
        for p in range(HG_HEADS // 2):
            ls = slice(p * LANES, (p + 1) * LANES)
            qp, kp, ip = q[:, ls], k[:, ls], iv[:, ls]
            att = jnp.zeros((2 * c, c), F32)
            for lv in range(N_LEVELS + 1):
                if lv < N_LEVELS:
                    el = e_all[lv * c:(lv + 1) * c, ls]
                    ql, kl = qp * el, kp * el
                else:
                    ql, kl = qp, kp
                qstack = jnp.concatenate([jnp.where(lo_half, ql, 0.0), jnp.where(lo_half, 0.0, ql)], axis=0)
                att = att + _dot_nt(qstack.astype(BF16), kl.astype(BF16)) * mk_ref[lv]
            ipb = ip.astype(BF16)
            o_intra = jnp.where(lo_half, _dot(att[:c].astype(BF16), ipb), _dot(att[c:].astype(BF16), ipb))
            eb = e_all[N_LEVELS * c:(N_LEVELS + 1) * c, ls]
            ebl = e_all[(N_LEVELS + 1) * c:(N_LEVELS + 2) * c, ls]
            upd = jnp.where(same_head, _dot_tn(ipb, (kp * ebl).astype(BF16)), 0.0)
            pending.append((sl, ls, p, o_intra, (qp * eb).astype(BF16), eb[c - 1:c, :], upd))
    states = [st_ref[p] for p in range(HG_HEADS // 2)]
    for sl, ls, p, o_intra, qe, decay, upd in pending:
        o_ref[sl, ls] = o_intra + _dot_nt(qe, states[p].astype(BF16))
        states[p] = states[p] * decay + upd
    for p in range(HG_HEADS // 2):
        st_ref[p] = states[p]


def _hgrn(h, lb, batch):
    n = h.shape[0]
    t = n // batch
    chunks = 4
    tm = chunks * HG_CHUNK
    steps = t // tm
    a_all, mask2 = _hgrn_constants()
    lb = lb.reshape(1, HG_WIDTH).astype(F32)
    full = lambda shape: pl.BlockSpec(shape, lambda b, i: (0,) * len(shape))
    col = lambda off: pl.BlockSpec((tm, HG_WIDTH), lambda b, i: (b * steps + i, off // HG_WIDTH))
    return pl.pallas_call(
        functools.partial(_hgrn_body, chunks=chunks),
        grid=(batch, steps),
        in_specs=[col(COL_HQ), col(COL_HF), col(COL_HI),
                  full((1, HG_WIDTH)), full((1, HG_WIDTH)), full((1, HG_WIDTH)),
                  full(a_all.shape), full(mask2.shape)],
        out_specs=pl.BlockSpec((tm, HG_WIDTH), lambda b, i: (b * steps + i, 0)),
        out_shape=jax.ShapeDtypeStruct((n, HG_WIDTH), F32),
        scratch_shapes=[pltpu.VMEM((HG_HEADS // 2, LANES, LANES), F32)],
        compiler_params=_params("parallel", "arbitrary"),
        name="hgrn2",
    )(h, h, h, jnp.log(lb), jnp.log1p(-lb), 1.0 - lb, a_all, mask2)


def _compress_body(x_ref, w1_ref, w2k_ref, w2vt_ref, pos_ref, kc_ref, vct_ref):
    def hidden(kv):
        x = x_ref[0, kv, 0].astype(BF16)
        w1 = w1_ref[kv].astype(BF16)
        half = w1.shape[0] // 2
        a = _dot(x, w1[:half])
        b = _dot(x, w1[half:])
        pos = jnp.broadcast_to(pos_ref[kv], (8, pos_ref.shape[-1])).astype(BF16)
        pw = _dot(pos, w1)[0:1]
        b_next = jnp.concatenate([b[1:], b[:1]], axis=0)
        return _gelu(a + b_next + pw).astype(BF16)

    nrow = x_ref.shape[3]
    kc = _dot(hidden(0), w2k_ref[...].astype(BF16))
    rid = lax.broadcasted_iota(jnp.int32, kc.shape, 0)
    kc_ref[0, 0] = jnp.where(rid < nrow - 1, kc, 0.0).astype(BF16)
    vct = _dot_nt(w2vt_ref[...].astype(BF16), hidden(1))
    cid = lax.broadcasted_iota(jnp.int32, vct.shape, 1)
    vct_ref[0, 0] = jnp.where(cid < nrow - 1, vct, 0.0).astype(BF16)


def _compress(xkv, w1, w2, pos):
    b, _, hk, nr, wd = xkv.shape
    dh = w2.shape[-1]
    w2k = jnp.concatenate([w2[0], w2[0]], axis=-1)
    w2vt = w2[1].T
    full = lambda a: pl.BlockSpec(a.shape, lambda i, k: (0,) * a.ndim)
    return pl.pallas_call(
        _compress_body,
        grid=(b, hk),
        in_specs=[pl.BlockSpec((1, 2, 1, nr, wd), lambda i, k: (i, 0, k, 0, 0)),
                  full(w1), full(w2k), full(w2vt), full(pos)],
        out_specs=[pl.BlockSpec((1, 1, nr, 2 * dh), lambda i, k: (i, k, 0, 0)),
                   pl.BlockSpec((1, 1, dh, nr), lambda i, k: (i, k, 0, 0))],
        out_shape=[jax.ShapeDtypeStruct((b, hk, nr, 2 * dh), BF16),
                   jax.ShapeDtypeStruct((b, hk, dh, nr), BF16)],
        compiler_params=_params("parallel", "parallel"),
        name="nsa_compress",
    )(xkv, w1, w2k, w2vt, pos)


def _emit_heads(o_t, pe_ref, po_ref, out_ref, pair0=0):
    hi, lo = _split2(o_t)
    for j in range(NSA_GROUP // 2):
        a = slice(2 * j * Q_BLOCK, (2 * j + 1) * Q_BLOCK)
        b = slice((2 * j + 1) * Q_BLOCK, (2 * j + 2) * Q_BLOCK)
        out_ref[:, (pair0 + j) * LANES:(pair0 + j + 1) * LANES] = (
            _dot_tn(hi[:, a], pe_ref[...]) + _dot_tn(lo[:, a], pe_ref[...])
            + _dot_tn(hi[:, b], po_ref[...]) + _dot_tn(lo[:, b], po_ref[...]))


def _head_placers():
    eye = np.eye(HEAD_DIM, dtype=np.float32)
    zero = np.zeros_like(eye)
    pe = jnp.asarray(np.concatenate([eye, zero], axis=1), dtype=BF16)
    po = jnp.asarray(np.concatenate([zero, eye], axis=1), dtype=BF16)
    return jnp.asarray(np.eye(LANES, dtype=np.float32), dtype=BF16), pe, po


def _select_body(q_ref, kc_ref, vct_ref, ovt_ref, eye_ref, pe_ref, po_ref, ocmp_ref, bias_ref, *, n_sel):
    c = pl.program_id(1)
    ncmp = kc_ref.shape[-2]
    nblk = ovt_ref.shape[0]
    rows = NSA_GROUP * Q_BLOCK
    q = q_ref[...] * (ATTN_SCALE * LOG2E)
    lane = lax.broadcasted_iota(jnp.int32, (Q_BLOCK, LANES), 1)
    lo_half = lane < HEAD_DIM
    t_lane = c * Q_BLOCK + (lax.broadcasted_iota(jnp.int32, (1, rows), 1) & (Q_BLOCK - 1))
    n_idx = lax.broadcasted_iota(jnp.int32, (ncmp, 1), 0)
    neg = jnp.where((n_idx * CMP_STRIDE + (CMP_BLOCK - 1)) <= t_lane, 0.0, NEG_INF)
    jb = lax.broadcasted_iota(jnp.int32, (nblk, 1), 0)
    t_row = c * Q_BLOCK + lax.broadcasted_iota(jnp.int32, (1, Q_BLOCK), 1)
    cur = lax.shift_right_logical(t_row, 6)
    forced = (jb == 0) | (jb == cur) | (jb == cur - 1)
    causal = jb * SLC_BLOCK <= t_row
    probs, vals = [], []
    for hk in range(NSA_KV_HEADS):
        qts = []
        for g in range(NSA_GROUP):
            hq = hk * NSA_GROUP + g
            pair = q[:, (hq // 2) * LANES:(hq // 2 + 1) * LANES]
            qm = jnp.where(lo_half if hq % 2 == 0 else jnp.logical_not(lo_half), pair, 0.0).astype(BF16)
            qts.append(_dot_nt(eye_ref[...], qm).astype(BF16))
        s = _dot(kc_ref[0, hk], jnp.concatenate(qts, axis=1)) + neg
        m = jnp.max(s, axis=0, keepdims=True)
        e = jnp.exp2(s - m)
        l = jnp.sum(e, axis=0, keepdims=True)
        p = e * jnp.where(m > 0.5 * NEG_INF, 1.0 / l, 0.0)
        probs.append(p.astype(BF16))
        psum = p[:, 0:Q_BLOCK]
        for g in range(1, NSA_GROUP):
            psum = psum + p[:, g * Q_BLOCK:(g + 1) * Q_BLOCK]
        ph, plo = _split2(psum)
        imp = _dot(ovt_ref[...], ph) + _dot(ovt_ref[...], plo)
        vals.append(jnp.where(causal, imp + jnp.where(forced, FORCE_SELECT, 0.0), NEG_INF))
    val0 = jnp.concatenate(vals, axis=1)
    val = val0
    for _ in range(n_sel):
        m = jnp.max(val, axis=0, keepdims=True)
        idx = jnp.min(jnp.where(val == m, jb, nblk), axis=0, keepdims=True)
        val = jnp.where(jb == idx, TAKEN, val)
    bias = jnp.where((val == TAKEN) & (val0 > 0.5 * NEG_INF), 0.0, NEG_INF).astype(BF16)
    for hk in range(NSA_KV_HEADS):
        bias_ref[0, hk] = bias[:, hk * Q_BLOCK:(hk + 1) * Q_BLOCK]
        o_t = _dot(vct_ref[0, hk], probs[hk])
        _emit_heads(o_t, pe_ref, po_ref, ocmp_ref, hk * (NSA_GROUP // 2))


def _select(h, kcmp, vcmp_t, batch):
    n = h.shape[0]
    t = n // batch
    nq = t // Q_BLOCK
    ncmp = kcmp.shape[-2]
    nblk = t // SLC_BLOCK
    eye_q, pe, po = _head_placers()
    ii = np.arange(ncmp)[None, :]
    jj = np.arange(nblk)[:, None]
    ovt = ((ii * CMP_STRIDE < (jj + 1) * SLC_BLOCK) & (ii * CMP_STRIDE + CMP_BLOCK > jj * SLC_BLOCK)
           & (ii < ncmp - 1))
    ovt = jnp.asarray(ovt.astype(np.float32), dtype=BF16)
    full = lambda a: pl.BlockSpec(a.shape, lambda b, i: (0,) * a.ndim)
    per_batch = lambda a: pl.BlockSpec((1,) + a.shape[1:], lambda b, i: (b,) + (0,) * (a.ndim - 1))
    return pl.pallas_call(
        functools.partial(_select_body, n_sel=min(N_SLC, nblk)),
        grid=(batch, nq),
        in_specs=[pl.BlockSpec((Q_BLOCK, NSA_WIDTH), lambda b, i: (b * nq + i, COL_NQ // NSA_WIDTH)),
                  per_batch(kcmp), per_batch(vcmp_t), full(ovt), full(eye_q), full(pe), full(po)],
        out_specs=[pl.BlockSpec((Q_BLOCK, NSA_WIDTH), lambda b, i: (b * nq + i, 0)),
                   pl.BlockSpec((1, NSA_KV_HEADS, nblk, Q_BLOCK), lambda b, i: (b, 0, 0, i))],
        out_shape=[jax.ShapeDtypeStruct((n, NSA_WIDTH), F32),
                   jax.ShapeDtypeStruct((batch, NSA_KV_HEADS, nblk, t), BF16)],
        compiler_params=_params("parallel", "parallel"),
        name="nsa_select",
    )(h, kcmp, vcmp_t, ovt, eye_q, pe, po)


def _softmax_tile_t(s, m_prev):
    m_new = jnp.maximum(m_prev, jnp.max(s, axis=0, keepdims=True))
    return m_new, jnp.exp2(s - m_new).astype(BF16)


def _attn_body(q_ref, bias_ref, kaug_ref, vst_ref, kw_ref, vwt_ref, eye_ref, pe_ref, po_ref, oslc_ref, owin_ref,
               qaug_sc, acc_sc, sa_sc, sb_sc):
    c = pl.program_id(2)
    rows = NSA_GROUP * Q_BLOCK
    q4 = q_ref[...] * (ATTN_SCALE * LOG2E)
    lane = lax.broadcasted_iota(jnp.int32, (Q_BLOCK, LANES), 1)
    lo_half = lane < HEAD_DIM
    bias = bias_ref[0, 0]
    for g in range(NSA_GROUP):
        pair = q4[:, (g // 2) * LANES:(g // 2 + 1) * LANES]
        qm = jnp.where(lo_half if g % 2 == 0 else jnp.logical_not(lo_half), pair, 0.0).astype(BF16)
        qaug_sc[0:LANES, g * Q_BLOCK:(g + 1) * Q_BLOCK] = _dot_nt(eye_ref[...], qm).astype(BF16)
        qaug_sc[LANES:, g * Q_BLOCK:(g + 1) * Q_BLOCK] = bias
    t_row = c * Q_BLOCK + (lax.broadcasted_iota(jnp.int32, (1, rows), 1) & (Q_BLOCK - 1))

    acc_sc[...] = jnp.zeros(acc_sc.shape, F32)
    last_tile = kaug_ref.shape[2] // ATTN_TK - 1

    def scores(tile, s_ref, masked):
        start = pl.multiple_of(jnp.minimum(tile, last_tile) * ATTN_TK, ATTN_TK)
        s = _dot(kaug_ref[0, 0, pl.ds(start, ATTN_TK), :], qaug_sc[...])
        if masked:
            kpos = tile * ATTN_TK + lax.broadcasted_iota(jnp.int32, (ATTN_TK, 1), 0)
            s = jnp.where(kpos <= t_row, s, NEG_INF)
        s_ref[...] = s
        return jnp.max(s, axis=0, keepdims=True)

    def accumulate(tile, m_prev, m_tile, s_ref):
        p = jnp.exp2(s_ref[...] - m_tile).astype(BF16)
        pv = _dot(vst_ref[0, 0, jnp.minimum(tile, last_tile)], p)
        m_new = jnp.maximum(m_prev, m_tile)
        acc_sc[...] = jnp.exp2(m_prev - m_new) * acc_sc[...] + jnp.exp2(m_tile - m_new) * pv
        return m_new

    def two_tiles(t0, carry, masked):
        m, mt_a = carry
        mt_b = scores(t0 + 1, sb_sc, masked)
        m = accumulate(t0, m, mt_a, sa_sc)
        mt_a = scores(t0 + 2, sa_sc, masked)
        m = accumulate(t0 + 1, m, mt_b, sb_sc)
        return m, mt_a

    n_full = (c * Q_BLOCK) // ATTN_TK
    n_loop = jnp.maximum(n_full - 1, 0) // 2
    mt_0 = scores(0, sa_sc, True)

    span = WINDOW + Q_BLOCK
    wblk = jnp.maximum(c - WINDOW // Q_BLOCK, 0)
    wstart = pl.multiple_of(wblk * Q_BLOCK, Q_BLOCK)
    s = _dot(kw_ref[0, 0, pl.ds(wstart, span), :], qaug_sc[0:LANES, :])
    kpos = wstart + lax.broadcasted_iota(jnp.int32, (span, 1), 0)
    ok = (kpos <= t_row) & (kpos > t_row - WINDOW)
    _, p = _softmax_tile_t(jnp.where(ok, s, NEG_INF), jnp.full((1, rows), TAKEN, F32))
    acc = _dot(vwt_ref[0, 0, wblk], p[0:Q_BLOCK])
    for i in range(1, span // Q_BLOCK):
        acc = acc + _dot(vwt_ref[0, 0, wblk + i], p[i * Q_BLOCK:(i + 1) * Q_BLOCK])
    _emit_heads(acc[0:HEAD_DIM] / acc[HEAD_DIM:HEAD_DIM + 1], pe_ref, po_ref, owin_ref)

    carry = (jnp.full((1, rows), TAKEN, F32), mt_0)
    carry = lax.fori_loop(0, n_loop // 2,
                          lambda i, cr: two_tiles(4 * i + 2, two_tiles(4 * i, cr, False), False), carry)
    carry = lax.fori_loop(n_loop - n_loop % 2, n_loop, lambda i, cr: two_tiles(2 * i, cr, False), carry)
    m, mt_a = two_tiles(2 * n_loop, carry, True)
    accumulate(2 * n_loop + 2, m, mt_a, sa_sc)
    acc = acc_sc[...]
    _emit_heads(acc[0:HEAD_DIM] / acc[HEAD_DIM:HEAD_DIM + 1], pe_ref, po_ref, oslc_ref)


def _attention(h, bias_t, kaug, vst, kw, vwt, batch):
    n = h.shape[0]
    t = n // batch
    nq = t // Q_BLOCK
    gw = NSA_GROUP * HEAD_DIM
    rows = NSA_GROUP * Q_BLOCK
    eye_q, pe, po = _head_placers()
    res = lambda a: pl.BlockSpec((1, 1) + a.shape[2:], lambda b, k, i: (b, k) + (0,) * (a.ndim - 2))
    full = lambda a: pl.BlockSpec(a.shape, lambda b, k, i: (0,) * a.ndim)
    out_spec = pl.BlockSpec((Q_BLOCK, gw), lambda b, k, i: (b * nq + i, k))
    return pl.pallas_call(
        _attn_body,
        grid=(batch, NSA_KV_HEADS, nq),
        in_specs=[pl.BlockSpec((Q_BLOCK, gw), lambda b, k, i: (b * nq + i, COL_NQ // gw + k)),
                  pl.BlockSpec((1, 1, bias_t.shape[2], Q_BLOCK), lambda b, k, i: (b, k, 0, i)),
                  res(kaug), res(vst), res(kw), res(vwt), full(eye_q), full(pe), full(po)],
        out_specs=[out_spec, out_spec],
        out_shape=[jax.ShapeDtypeStruct((n, NSA_WIDTH), F32)] * 2,
        scratch_shapes=[pltpu.VMEM((kaug.shape[-1], rows), BF16), pltpu.VMEM((V_ROWS, rows), F32),
                        pltpu.VMEM((ATTN_TK, rows), F32), pltpu.VMEM((ATTN_TK, rows), F32)],
        compiler_params=_params("parallel", "parallel", "arbitrary"),
        name="nsa_attention",
    )(h, bias_t, kaug, vst, kw, vwt, eye_q, pe, po)


def _branch_expanders():
    e = np.zeros((N_BRANCH, GATE_PAD, NSA_WIDTH), np.float32)
    for hq in range(NSA_HEADS):
        for br in range(N_BRANCH):
            e[br, hq * N_BRANCH + br, hq * HEAD_DIM:(hq + 1) * HEAD_DIM] = 1.0
    return jnp.asarray(e, dtype=BF16)


def _route(logits):
    lane = lax.broadcasted_iota(jnp.int32, logits.shape, 1)
    is_g = lane < N_GROUPS
    gl = jnp.where(is_g, logits, -jnp.inf)
    gmax = jnp.max(gl, axis=-1, keepdims=True)
    gsum = jnp.sum(jnp.where(is_g, jnp.exp(logits - gmax), 0.0), axis=-1, keepdims=True)
    g_val = 1.0 / gsum
    g_idx = jnp.min(jnp.where(is_g & (logits == gmax), lane, ROUTER_PAD), axis=-1, keepdims=True)
    e_lo = EXPERT_LANE0 + g_idx * EXPERTS_PER_GROUP
    in_sel = (lane >= e_lo) & (lane < e_lo + EXPERTS_PER_GROUP)
    el = jnp.where(in_sel, logits, -jnp.inf)
    m1 = jnp.max(el, axis=-1, keepdims=True)
    i1 = jnp.min(jnp.where(in_sel & (logits == m1), lane, ROUTER_PAD), axis=-1, keepdims=True)
    el2 = jnp.where(lane == i1, -jnp.inf, el)
    m2 = jnp.max(el2, axis=-1, keepdims=True)
    i2 = jnp.min(jnp.where(el2 == m2, lane, ROUTER_PAD), axis=-1, keepdims=True)
    r = jnp.exp(m2 - m1)
    w1 = 1.0 / (1.0 + r)
    w2 = r * w1
    return g_val * (jnp.where(lane == i1, w1, 0.0) + jnp.where(lane == i2, w2, 0.0))


def _out_body(ya_ref, ocmp_ref, oslc_ref, owin_ref, gl_ref, yc_ref, hg_ref, x_ref, gb_ref, ex_ref,
              gmb_ref, gmc_ref, og_ref, w_ref, lg_ref, lb_ref, wrh_ref, wrl_ref, br_ref, x1_ref, gate_ref):
    g = _sigmoid(gl_ref[...] + gb_ref[...])
    yb = (_dot2(g, ex_ref[0]) * ocmp_ref[...] + _dot2(g, ex_ref[1]) * oslc_ref[...]
          + _dot2(g, ex_ref[2]) * owin_ref[...])
    n_ab = GM_WIDTH + NSA_WIDTH
    ybn = yb * lax.rsqrt(_dot2(yb * yb, gmb_ref[...]) + RMS_EPS) * og_ref[:, GM_WIDTH:n_ab]
    yc = yc_ref[...]
    hg = hg_ref[...]
    ycn = yc * lax.rsqrt(_dot2(yc * yc, gmc_ref[...]) + RMS_EPS) * og_ref[:, n_ab:] * (hg * _sigmoid(hg))
    y = (_dot(ya_ref[...].astype(BF16), w_ref[0:GM_WIDTH, :]) + _dot(ybn.astype(BF16), w_ref[GM_WIDTH:n_ab, :])
         + _dot(ycn.astype(BF16), w_ref[n_ab:, :]))
    x1 = _layer_norm(DEEPNORM_ALPHA * x_ref[...] + y, lg_ref[...], lb_ref[...])
    x1_ref[...] = x1
    xh, xl = _split2(x1)
    logits = _dot(xh, wrh_ref[...]) + _dot(xh, wrl_ref[...]) + _dot(xl, wrh_ref[...]) + br_ref[...]
    gate_ref[...] = _route(logits)


def _outproj(ya, ocmp, oslc, owin, h, yc, x2, gate_b, out_gain, w_out, ln_g, ln_b, wr, br):
    n, d = x2.shape
    tm = 256
    row = lambda a: a.reshape(1, -1)
    gb = jnp.pad(gate_b, (0, GATE_PAD - gate_b.shape[0])).reshape(1, GATE_PAD)
    wrh = wr.astype(BF16)
    wrl = (wr - wrh.astype(F32)).astype(BF16)
    ex = _branch_expanders()
    full = lambda shape: pl.BlockSpec(shape, lambda i: (0,) * len(shape))
    tile = lambda w, cb=0: pl.BlockSpec((tm, w), lambda i: (i, cb))
    return pl.pallas_call(
        _out_body,
        grid=(n // tm,),
        in_specs=[tile(GM_WIDTH), tile(NSA_WIDTH), tile(NSA_WIDTH), tile(NSA_WIDTH),
                  tile(GATE_PAD, COL_NG // GATE_PAD), tile(HG_WIDTH), tile(HG_WIDTH, COL_HG // HG_WIDTH), tile(d),
                  full((1, GATE_PAD)), full(ex.shape), full((NSA_WIDTH, NSA_WIDTH)), full((HG_WIDTH, HG_WIDTH)),
                  full((1, d)), full((d, d)), full((1, d)), full((1, d)),
                  full((d, ROUTER_PAD)), full((d, ROUTER_PAD)), full((1, ROUTER_PAD))],
        out_specs=[tile(d), tile(ROUTER_PAD)],
        out_shape=[jax.ShapeDtypeStruct((n, d), F32), jax.ShapeDtypeStruct((n, ROUTER_PAD), F32)],
        compiler_params=_params("parallel"),
        name="outproj_ln_router",
    )(ya, ocmp, oslc, owin, h, yc, h, x2, gb, ex, _group_mean_matrix(NSA_WIDTH), _group_mean_matrix(HG_WIDTH),
      row(out_gain), w_out.astype(BF16), row(ln_g), row(ln_b), wrh, wrl, br)


def _moe_body(x_ref, gate_ref, wg_ref, wu_ref, wd_ref, lg_ref, lb_ref, o_ref,
              xb_sc, acc_sc, slot_sc, slott_sc, cnt_sc):
    e = pl.program_id(1)
    tm = x_ref.shape[0]

    @pl.when(e == 0)
    def _():
        xb_sc[...] = x_ref[...].astype(BF16)
        acc_sc[...] = jnp.zeros_like(acc_sc)
        used = jnp.where(gate_ref[...] > 0.0, 1.0, 0.0)
        r = lax.broadcasted_iota(jnp.int32, (tm, tm), 0)
        q = lax.broadcasted_iota(jnp.int32, (tm, tm), 1)
        before = jnp.where(q < r, 1.0, 0.0).astype(BF16)
        after = jnp.where(r < q, 1.0, 0.0).astype(BF16)
        ub = used.astype(BF16)
        slot_sc[...] = jnp.where(used > 0.0, _dot(before, ub), -1.0)
        ri = lax.broadcasted_iota(jnp.int32, (ROUTER_PAD, ROUTER_PAD), 0)
        qi = lax.broadcasted_iota(jnp.int32, (ROUTER_PAD, ROUTER_PAD), 1)
        used_t = _dot_nt(jnp.where(ri == qi, 1.0, 0.0).astype(BF16), ub)
        slott_sc[...] = jnp.where(used_t > 0.0, _dot(used_t.astype(BF16), after), -1.0)
        cnt_sc[...] = jnp.broadcast_to(jnp.sum(used, axis=0, keepdims=True), cnt_sc.shape)

    lane = lax.broadcasted_iota(jnp.int32, gate_ref.shape, 1)
    mine = lane == e + EXPERT_LANE0
    gcol = jnp.sum(jnp.where(mine, gate_ref[...], 0.0), axis=-1, keepdims=True)
    scol = jnp.sum(jnp.where(mine, slot_sc[...], 0.0), axis=-1, keepdims=True)
    srow = slott_sc[pl.ds(e + EXPERT_LANE0, 1), :]
    n_rows = jnp.sum(jnp.where(mine[0:1], cnt_sc[0:1], 0.0)).astype(jnp.int32)

    def chunk(ci, carry):
        base = (ci * MOE_CHUNK).astype(F32)
        r_col = base + lax.broadcasted_iota(jnp.int32, (MOE_CHUNK, 1), 0).astype(F32)
        r_row = base + lax.broadcasted_iota(jnp.int32, (1, MOE_CHUNK), 1).astype(F32)
        pick = jnp.where(srow == r_col, 1.0, 0.0).astype(BF16)
        xg = _dot(pick, xb_sc[...]).astype(BF16)
        hg = _dot(xg, wg_ref[0])
        hu = _dot(xg, wu_ref[0])
        y = _dot((hg * _sigmoid(hg) * hu).astype(BF16), wd_ref[0]).astype(BF16)
        place = jnp.where(scol == r_row, 1.0, 0.0).astype(BF16)
        acc_sc[...] += gcol * _dot(place, y)
        return carry

    lax.fori_loop(0, (n_rows + MOE_CHUNK - 1) // MOE_CHUNK, chunk, 0)

    @pl.when(e == pl.num_programs(1) - 1)
    def _():
        o_ref[...] = _layer_norm(DEEPNORM_ALPHA * x_ref[...] + acc_sc[...], lg_ref[...], lb_ref[...])


def _moe(x1, gate, wg, wu, wd, ln_g, ln_b):
    n, d = x1.shape
    ne, _, de = wg.shape
    tm = MOE_TM
    row = lambda a: a.reshape(1, -1)
    return pl.pallas_call(
        _moe_body,
        grid=(n // tm, ne),
        in_specs=[pl.BlockSpec((tm, d), lambda i, e: (i, 0)),
                  pl.BlockSpec((tm, ROUTER_PAD), lambda i, e: (i, 0)),
                  pl.BlockSpec((1, d, de), lambda i, e: (e, 0, 0)),
                  pl.BlockSpec((1, d, de), lambda i, e: (e, 0, 0)),
                  pl.BlockSpec((1, de, d), lambda i, e: (e, 0, 0)),
                  pl.BlockSpec((1, d), lambda i, e: (0, 0)),
                  pl.BlockSpec((1, d), lambda i, e: (0, 0))],
        out_specs=pl.BlockSpec((tm, d), lambda i, e: (i, 0)),
        out_shape=jax.ShapeDtypeStruct((n, d), F32),
        scratch_shapes=[pltpu.VMEM((tm, d), BF16), pltpu.VMEM((tm, d), F32),
                        pltpu.VMEM((tm, ROUTER_PAD), F32), pltpu.VMEM((ROUTER_PAD, tm), F32),
                        pltpu.VMEM((8, ROUTER_PAD), F32)],
        compiler_params=_params("parallel", "arbitrary"),
        name="moe_ln",
    )(x1, gate, wg, wu, wd, row(ln_g), row(ln_b))


def _prep_w_in(w_in):
    names = ("gu", "gv", "nq", "kc", "vc", "ks", "vs", "kw", "vw", "ng", "hq", "hf", "hi", "hg")
    widths = (GM_WIDTH, GM_WIDTH, NSA_WIDTH) + (NSA_KV_WIDTH,) * 6 + (N_BRANCH * NSA_HEADS,) + (HG_WIDTH,) * 4
    ends = np.cumsum(widths)
    col = {nm: w_in[:, int(e - wd):int(e)] for nm, e, wd in zip(names, ends, widths)}
    pad = jnp.zeros((w_in.shape[0], GATE_PAD - N_BRANCH * NSA_HEADS), w_in.dtype)
    heads = lambda a: [a[:, i * HEAD_DIM:(i + 1) * HEAD_DIM] for i in range(NSA_KV_HEADS)]
    kdup = [a for k in (col["ks"], col["kw"]) for hd in heads(k) for a in (hd, hd)]
    w = jnp.concatenate([col[nm] for nm in ("gu", "gv", "nq", "kc", "vc", "hq", "hf", "hi", "hg", "ng")]
                        + [pad] + kdup, axis=1)
    wvt = jnp.concatenate([col["vs"], col["vw"]], axis=1).T
    return w.astype(BF16), wvt.astype(BF16)


def _layer(x2, batch, w_in, gm_v_gain, gm_v_bias, gm_w_s, gm_b_s, cmp_pos, cmp_w1, cmp_w2, nsa_gate_b, lb,
           out_gain, w_out, ln1_g, ln1_b, wr, br, wg, wu, wd, ln2_g, ln2_b):
    n = x2.shape[0]
    t = n // batch
    h, kaug, kw, vst, vwt = _inproj(x2, *_prep_w_in(w_in), batch)

    ya = _gmlp(h, gm_v_gain, gm_v_bias, gm_w_s, gm_b_s, out_gain[:GM_WIDTH])
    yc = _hgrn(h, lb, batch)

    kvc = h[:, COL_KC:COL_KC + 2 * NSA_KV_WIDTH].reshape(batch, t, 2, NSA_KV_HEADS, HEAD_DIM)
    kvc = kvc.transpose(0, 2, 3, 1, 4).reshape(batch, 2, NSA_KV_HEADS, t // CMP_STRIDE, CMP_STRIDE * HEAD_DIM)
    kcmp, vcmp_t = _compress(kvc, cmp_w1, cmp_w2, cmp_pos.reshape(2, 1, CMP_BLOCK * HEAD_DIM))
    ocmp, bias_t = _select(h, kcmp, vcmp_t, batch)
    oslc, owin = _attention(h, bias_t, kaug, vst, kw, vwt, batch)

    x1, gate = _outproj(ya, ocmp, oslc, owin, h, yc, x2, nsa_gate_b, out_gain, w_out, ln1_g, ln1_b, wr, br)
    return _moe(x1, gate, wg, wu, wd, ln2_g, ln2_b)


def kernel(x, w_in, gm_v_gain, gm_v_bias, gm_w_s, gm_b_s, cmp_pos, cmp_w1, cmp_w2, nsa_gate_b, hg_lower, out_gain, w_out, ln1_g, ln1_b, router_group_w, router_group_b, router_expert_w, router_expert_b, exp_w_gate, exp_w_up, exp_w_down, ln2_g, ln2_b):
    batch, t, d = x.shape
    depth = w_in.shape[0]
    lb_all = jnp.cumsum(jax.nn.softmax(hg_lower.astype(F32), axis=0), axis=0)
    lb_all = lb_all - lb_all[0]
    x2 = x.reshape(batch * t, d)
    for l in range(depth):
        pad = ROUTER_PAD - N_GROUPS - N_EXPERTS
        wr = jnp.concatenate([router_group_w[l], router_expert_w[l], jnp.zeros((d, pad), F32)], axis=1)
        br = jnp.concatenate([router_group_b[l], router_expert_b[l], jnp.zeros((pad,), F32)]).reshape(1, ROUTER_PAD)
        x2 = _layer(x2, batch, w_in[l], gm_v_gain[l], gm_v_bias[l], gm_w_s[l], gm_b_s[l], cmp_pos[l], cmp_w1[l],
                    cmp_w2[l], nsa_gate_b[l], lb_all[l], out_gain[l], w_out[l], ln1_g[l], ln1_b[l], wr, br,
                    exp_w_gate[l].astype(BF16), exp_w_up[l].astype(BF16), exp_w_down[l].astype(BF16),
                    ln2_g[l], ln2_b[l])
    return x2.reshape(batch, t, d)
```

```python
import functools

import numpy as np
import jax
import jax.numpy as jnp
from jax import lax
from jax.experimental import pallas as pl
from jax.experimental.pallas import tpu as pltpu

F32 = jnp.float32
BF16 = jnp.bfloat16

D_MODEL = 1024
HEAD_DIM = 64
GM_GROUPS = 4
GM_WIDTH = GM_GROUPS * HEAD_DIM
GM_CHUNK = 128
NSA_HEADS = 8
NSA_KV_HEADS = 2
NSA_GROUP = NSA_HEADS // NSA_KV_HEADS
NSA_WIDTH = NSA_HEADS * HEAD_DIM
NSA_KV_WIDTH = NSA_KV_HEADS * HEAD_DIM
CMP_BLOCK = 32
CMP_STRIDE = 16
SLC_BLOCK = 64
N_SLC = 16
WINDOW = 512
Q_BLOCK = 128
N_BRANCH = 3
HG_HEADS = 4
HG_WIDTH = HG_HEADS * HEAD_DIM
HG_CHUNK = 64
N_GROUPS = 4
EXPERTS_PER_GROUP = 4
N_EXPERTS = N_GROUPS * EXPERTS_PER_GROUP
D_EXPERT = 512
DEPTH = 2
DEEPNORM_ALPHA = (2.0 * DEPTH) ** 0.25
LN_EPS = 1e-5
RMS_EPS = 1e-6
NEG_INF = -1e30
FORCE_SELECT = 1e4
TAKEN = -(2.0 ** 126)
ATTN_SCALE = HEAD_DIM ** -0.5

LANES = 128
GATE_PAD = LANES
ROUTER_PAD = LANES
EXPERT_LANE0 = N_GROUPS
VMEM_LIMIT = 56 * 1024 * 1024

COL_GU = 0
COL_GV = 256
COL_NQ = 512
COL_KC = 1024
COL_HQ = 1280
COL_HF = 1536
COL_HI = 1792
COL_HG = 2048
COL_NG = 2304
H_COLS = COL_NG + GATE_PAD
KDUP_COLS = 4 * NSA_KV_WIDTH

ATTN_TK = 512
V_ROWS = HEAD_DIM + 16
LOG2E = 1.4426950408889634
MOE_TM = 1024
MOE_CHUNK = 160
MOE_GROUP = 8


def _dot(a, b):
    return jnp.dot(a, b, preferred_element_type=F32)


def _dot_nt(a, b):
    return lax.dot_general(a, b, (((1,), (1,)), ((), ())), preferred_element_type=F32)


def _dot_tn(a, b):
    return lax.dot_general(a, b, (((0,), (0,)), ((), ())), preferred_element_type=F32)


def _split2(x):
    hi = x.astype(BF16)
    lo = (x - hi.astype(F32)).astype(BF16)
    return hi, lo


def _dot2(x, m, lhs=False):
    hi, lo = _split2(x)
    return _dot(m, hi) + _dot(m, lo) if lhs else _dot(hi, m) + _dot(lo, m)


def _gelu(x):
    return 0.5 * x * (1.0 + jnp.tanh(0.7978845608028654 * (x + 0.044715 * (x * x * x))))


def _sigmoid(x):
    return 1.0 / (1.0 + jnp.exp(-x))


def _log_sigmoid(x):
    return jnp.minimum(x, 0.0) - jnp.log1p(jnp.exp(-jnp.abs(x)))


def _layer_norm(z, g, b):
    mu = jnp.mean(z, axis=-1, keepdims=True)
    zc = z - mu
    var = jnp.mean(zc * zc, axis=-1, keepdims=True)
    return zc * lax.rsqrt(var + LN_EPS) * g + b


def _params(*sem):
    return pltpu.CompilerParams(dimension_semantics=sem, vmem_limit_bytes=VMEM_LIMIT)


def _inproj_body(x_ref, w_ref, wvt_ref, h_ref, kaug_ref, kw_ref, vst_ref, vwt_ref):
    xb = x_ref[...].astype(BF16)
    tm = xb.shape[0]
    acc = _dot(xb, w_ref[...])
    h_ref[...] = acc[:, :H_COLS]
    nblk = kaug_ref.shape[-1] - LANES
    pos = pl.program_id(1) * tm + lax.broadcasted_iota(jnp.int32, (tm, nblk), 0)
    blk = lax.broadcasted_iota(jnp.int32, (tm, nblk), 1)
    onehot = jnp.where(lax.shift_right_logical(pos, 6) == blk, 1.0, 0.0).astype(BF16)
    for hk in range(NSA_KV_HEADS):
        c0 = H_COLS + hk * LANES
        kaug_ref[0, hk, :, 0:LANES] = acc[:, c0:c0 + LANES].astype(BF16)
        kaug_ref[0, hk, :, LANES:] = onehot
        c1 = H_COLS + (NSA_KV_HEADS + hk) * LANES
        kw_ref[0, hk] = acc[:, c1:c1 + LANES].astype(BF16)
    vt = _dot_nt(wvt_ref[...], xb)
    ones_then_zeros = lambda w: jnp.where(
        lax.broadcasted_iota(jnp.int32, (V_ROWS - HEAD_DIM, w), 0) == 0, 1.0, 0.0).astype(BF16)
    tail, tail_q = ones_then_zeros(tm), ones_then_zeros(Q_BLOCK)
    for hk in range(NSA_KV_HEADS):
        vst_ref[0, hk, 0, 0:HEAD_DIM, :] = vt[hk * HEAD_DIM:(hk + 1) * HEAD_DIM].astype(BF16)
        vst_ref[0, hk, 0, HEAD_DIM:, :] = tail
        r0 = (NSA_KV_HEADS + hk) * HEAD_DIM
        for i in range(tm // Q_BLOCK):
            ls = slice(i * Q_BLOCK, (i + 1) * Q_BLOCK)
            vwt_ref[0, hk, i, 0:HEAD_DIM, :] = vt[r0:r0 + HEAD_DIM, ls].astype(BF16)
            vwt_ref[0, hk, i, HEAD_DIM:, :] = tail_q


def _inproj(x2, w, wvt, batch):
    n, d = x2.shape
    t = n // batch
    tm = ATTN_TK
    steps = t // tm
    nblk = t // SLC_BLOCK
    hk = NSA_KV_HEADS
    return pl.pallas_call(
        _inproj_body,
        grid=(batch, steps),
        in_specs=[pl.BlockSpec((tm, d), lambda b, j: (b * steps + j, 0)),
                  pl.BlockSpec(w.shape, lambda b, j: (0, 0)),
                  pl.BlockSpec(wvt.shape, lambda b, j: (0, 0))],
        out_specs=[pl.BlockSpec((tm, H_COLS), lambda b, j: (b * steps + j, 0)),
                   pl.BlockSpec((1, hk, tm, LANES + nblk), lambda b, j: (b, 0, j, 0)),
                   pl.BlockSpec((1, hk, tm, LANES), lambda b, j: (b, 0, j, 0)),
                   pl.BlockSpec((1, hk, 1, V_ROWS, tm), lambda b, j: (b, 0, j, 0, 0)),
                   pl.BlockSpec((1, hk, tm // Q_BLOCK, V_ROWS, Q_BLOCK), lambda b, j: (b, 0, j, 0, 0))],
        out_shape=[jax.ShapeDtypeStruct((n, H_COLS), F32),
                   jax.ShapeDtypeStruct((batch, hk, t, LANES + nblk), BF16),
                   jax.ShapeDtypeStruct((batch, hk, t, LANES), BF16),
                   jax.ShapeDtypeStruct((batch, hk, steps, V_ROWS, tm), BF16),
                   jax.ShapeDtypeStruct((batch, hk, t // Q_BLOCK, V_ROWS, Q_BLOCK), BF16)],
        compiler_params=_params("parallel", "parallel"),
        name="inproj",
    )(x2, w, wvt)


def _group_mean_matrix(width):
    g = np.arange(width) // HEAD_DIM
    return jnp.asarray((g[:, None] == g[None, :]).astype(np.float32) / HEAD_DIM, dtype=BF16)


def _gmlp_body(gu_ref, gv_ref, gain_ref, bias_ref, ws_ref, bsx_ref, gm_ref, og_ref, o_ref, *, chunks):
    gm = gm_ref[...]
    row = lax.broadcasted_iota(jnp.int32, (GM_CHUNK, GM_CHUNK), 0)
    col = lax.broadcasted_iota(jnp.int32, (GM_CHUNK, GM_CHUNK), 1)
    tril = row >= col
    lane_grp = lax.shift_right_logical(lax.broadcasted_iota(jnp.int32, (GM_CHUNK, GM_WIDTH), 1), 6)
    ws = [jnp.where(tril, ws_ref[g], 0.0).astype(BF16) for g in range(GM_GROUPS)]
    for c in range(chunks):
        sl = slice(c * GM_CHUNK, (c + 1) * GM_CHUNK)
        u = _gelu(gu_ref[sl, :])
        v = _gelu(gv_ref[sl, :])
        mu = _dot2(v, gm)
        vc = v - mu
        var = _dot2(vc * vc, gm)
        vn = (vc * lax.rsqrt(var + LN_EPS) * gain_ref[...] + bias_ref[...]).astype(BF16)
        z = bsx_ref[...]
        for g in range(GM_GROUPS):
            z = z + jnp.where(lane_grp == g, _dot(ws[g], vn), 0.0)
        y = u * z
        ms = _dot2(y * y, gm)
        o_ref[sl, :] = y * lax.rsqrt(ms + RMS_EPS) * og_ref[...]


def _gmlp(h, gain, bias, w_s, b_s, out_gain_a):
    n = h.shape[0]
    chunks = 4
    tm = chunks * GM_CHUNK
    bsx = jnp.repeat(b_s.T, HEAD_DIM, axis=1)
    row = lambda a: a.reshape(1, -1)
    full = lambda shape: pl.BlockSpec(shape, lambda i: (0,) * len(shape))
    return pl.pallas_call(
        functools.partial(_gmlp_body, chunks=chunks),
        grid=(n // tm,),
        in_specs=[
            pl.BlockSpec((tm, GM_WIDTH), lambda i: (i, COL_GU // GM_WIDTH)),
            pl.BlockSpec((tm, GM_WIDTH), lambda i: (i, COL_GV // GM_WIDTH)),
            full((1, GM_WIDTH)), full((1, GM_WIDTH)),
            full((GM_GROUPS, GM_CHUNK, GM_CHUNK)), full((GM_CHUNK, GM_WIDTH)),
            full((GM_WIDTH, GM_WIDTH)), full((1, GM_WIDTH)),
        ],
        out_specs=pl.BlockSpec((tm, GM_WIDTH), lambda i: (i, 0)),
        out_shape=jax.ShapeDtypeStruct((n, GM_WIDTH), F32),
        compiler_params=_params("parallel"),
        name="gmlp",
    )(h, h, row(gain), row(bias), w_s, bsx, _group_mean_matrix(GM_WIDTH), row(out_gain_a))


N_LEVELS = 6


def _hgrn_constants():
    c = HG_CHUNK
    t = np.arange(c)[:, None]
    u = np.arange(c)[None, :]
    mats, masks = [], []
    m = c // 2
    while m >= 1:
        p = (t // (2 * m)) * (2 * m) + m - 1
        mats.append(np.where(t > p, (u > p) & (u <= t), (u > t) & (u <= p)))
        masks.append(((t // (2 * m)) == (u // (2 * m))) & ((t % (2 * m)) >= m) & ((u % (2 * m)) < m))
        m //= 2
    mats.append(u <= t)
    mats.append(u > t)
    masks.append(np.eye(c, dtype=bool))
    a_all = np.concatenate(mats, 0).astype(np.float32)
    mask2 = np.stack([np.concatenate([mk, mk], 0) for mk in masks]).astype(np.float32)
    return jnp.asarray(a_all, dtype=BF16), jnp.asarray(mask2)


def _hgrn_body(q_ref, f_ref, i_ref, loglb_ref, log1m_ref, oml_ref, a_ref, mk_ref, o_ref, st_ref, *, chunks):
    @pl.when(pl.program_id(1) == 0)
    def _():
        st_ref[...] = jnp.zeros_like(st_ref)

    c = HG_CHUNK
    a_all = a_ref[...]
    lane = lax.broadcasted_iota(jnp.int32, (c, LANES), 1)
    lo_half = lane < HEAD_DIM
    r2 = lax.broadcasted_iota(jnp.int32, (LANES, LANES), 0)
    c2 = lax.broadcasted_iota(jnp.int32, (LANES, LANES), 1)
    same_head = (r2 < HEAD_DIM) == (c2 < HEAD_DIM)
    fl = f_ref[...]
    la = loglb_ref[...]
    lb2 = log1m_ref[...] + _log_sigmoid(fl)
    lf_all = jnp.maximum(la, lb2) + jnp.log1p(jnp.exp(-jnp.abs(la - lb2)))
    k_all = oml_ref[...] * (1.0 / (1.0 + jnp.exp(fl)))
    e_alls = []
    for ch in range(chunks):
        d_all = _dot2(lf_all[ch * c:(ch + 1) * c], a_all, lhs=True)
        e_alls.append(jnp.exp(d_all))
    pending = []
    for ch in range(chunks):
        sl = slice(ch * c, (ch + 1) * c)
        q, k, iv, e_all = q_ref[sl, :], k_all[sl], i_ref[sl, :], e_alls[ch]
        for p in range(HG_HEADS // 2):
            ls = slice(p * LANES, (p + 1) * LANES)
            qp, kp, ip = q[:, ls], k[:, ls], iv[:, ls]
            att = jnp.zeros((2 * c, c), F32)
            for lv in range(N_LEVELS + 1):
                if lv < N_LEVELS:
                    el = e_all[lv * c:(lv + 1) * c, ls]
                    ql, kl = qp * el, kp * el
                else:
                    ql, kl = qp, kp
                qstack = jnp.concatenate([jnp.where(lo_half, ql, 0.0), jnp.where(lo_half, 0.0, ql)], axis=0)
                att = att + _dot_nt(qstack.astype(BF16), kl.astype(BF16)) * mk_ref[lv]
            ipb = ip.astype(BF16)
            o_intra = jnp.where(lo_half, _dot(att[:c].astype(BF16), ipb), _dot(att[c:].astype(BF16), ipb))
            eb = e_all[N_LEVELS * c:(N_LEVELS + 1) * c, ls]
            ebl = e_all[(N_LEVELS + 1) * c:(N_LEVELS + 2) * c, ls]
            upd = jnp.where(same_head, _dot_tn(ipb, (kp * ebl).astype(BF16)), 0.0)
            pending.append((sl, ls, p, o_intra, (qp * eb).astype(BF16), eb[c - 1:c, :], upd))
    states = [st_ref[p] for p in range(HG_HEADS // 2)]
    for sl, ls, p, o_intra, qe, decay, upd in pending:
        o_ref[sl, ls] = o_intra + _dot_nt(qe, states[p].astype(BF16))
        states[p] = states[p] * decay + upd
    for p in range(HG_HEADS // 2):
        st_ref[p] = states[p]


def _hgrn(h, lb, batch):
    n = h.shape[0]
    t = n // batch
    chunks = 4
    tm = chunks * HG_CHUNK
    steps = t // tm
    a_all, mask2 = _hgrn_constants()
    lb = lb.reshape(1, HG_WIDTH).astype(F32)
    full = lambda shape: pl.BlockSpec(shape, lambda b, i: (0,) * len(shape))
    col = lambda off: pl.BlockSpec((tm, HG_WIDTH), lambda b, i: (b * steps + i, off // HG_WIDTH))
    return pl.pallas_call(
        functools.partial(_hgrn_body, chunks=chunks),
        grid=(batch, steps),
        in_specs=[col(COL_HQ), col(COL_HF), col(COL_HI),
                  full((1, HG_WIDTH)), full((1, HG_WIDTH)), full((1, HG_WIDTH)),
                  full(a_all.shape), full(mask2.shape)],
        out_specs=pl.BlockSpec((tm, HG_WIDTH), lambda b, i: (b * steps + i, 0)),
        out_shape=jax.ShapeDtypeStruct((n, HG_WIDTH), F32),
        scratch_shapes=[pltpu.VMEM((HG_HEADS // 2, LANES, LANES), F32)],
        compiler_params=_params("parallel", "arbitrary"),
        name="hgrn2",
    )(h, h, h, jnp.log(lb), jnp.log1p(-lb), 1.0 - lb, a_all, mask2)


def _compress_body(x_ref, w1_ref, w2k_ref, w2vt_ref, pos_ref, kc_ref, vct_ref):
    def hidden(kv):
        x = x_ref[0, kv, 0].astype(BF16)
        w1 = w1_ref[kv].astype(BF16)
        half = w1.shape[0] // 2
        a = _dot(x, w1[:half])
        b = _dot(x, w1[half:])
        pos = jnp.broadcast_to(pos_ref[kv], (8, pos_ref.shape[-1])).astype(BF16)
        pw = _dot(pos, w1)[0:1]
        b_next = jnp.concatenate([b[1:], b[:1]], axis=0)
        return _gelu(a + b_next + pw).astype(BF16)

    nrow = x_ref.shape[3]
    kc = _dot(hidden(0), w2k_ref[...].astype(BF16))
    rid = lax.broadcasted_iota(jnp.int32, kc.shape, 0)
    kc_ref[0, 0] = jnp.where(rid < nrow - 1, kc, 0.0).astype(BF16)
    vct = _dot_nt(w2vt_ref[...].astype(BF16), hidden(1))
    cid = lax.broadcasted_iota(jnp.int32, vct.shape, 1)
    vct_ref[0, 0] = jnp.where(cid < nrow - 1, vct, 0.0).astype(BF16)


def _compress(xkv, w1, w2, pos):
    b, _, hk, nr, wd = xkv.shape
    dh = w2.shape[-1]
    w2k = jnp.concatenate([w2[0], w2[0]], axis=-1)
    w2vt = w2[1].T
    full = lambda a: pl.BlockSpec(a.shape, lambda i, k: (0,) * a.ndim)
    return pl.pallas_call(
        _compress_body,
        grid=(b, hk),
        in_specs=[pl.BlockSpec((1, 2, 1, nr, wd), lambda i, k: (i, 0, k, 0, 0)),
                  full(w1), full(w2k), full(w2vt), full(pos)],
        out_specs=[pl.BlockSpec((1, 1, nr, 2 * dh), lambda i, k: (i, k, 0, 0)),
                   pl.BlockSpec((1, 1, dh, nr), lambda i, k: (i, k, 0, 0))],
        out_shape=[jax.ShapeDtypeStruct((b, hk, nr, 2 * dh), BF16),
                   jax.ShapeDtypeStruct((b, hk, dh, nr), BF16)],
        compiler_params=_params("parallel", "parallel"),
        name="nsa_compress",
    )(xkv, w1, w2k, w2vt, pos)


def _emit_heads(o_t, pe_ref, po_ref, out_ref, pair0=0):
    hi, lo = _split2(o_t)
    for j in range(NSA_GROUP // 2):
        a = slice(2 * j * Q_BLOCK, (2 * j + 1) * Q_BLOCK)
        b = slice((2 * j + 1) * Q_BLOCK, (2 * j + 2) * Q_BLOCK)
        out_ref[:, (pair0 + j) * LANES:(pair0 + j + 1) * LANES] = (
            _dot_tn(hi[:, a], pe_ref[...]) + _dot_tn(lo[:, a], pe_ref[...])
            + _dot_tn(hi[:, b], po_ref[...]) + _dot_tn(lo[:, b], po_ref[...]))


def _head_placers():
    eye = np.eye(HEAD_DIM, dtype=np.float32)
    zero = np.zeros_like(eye)
    pe = jnp.asarray(np.concatenate([eye, zero], axis=1), dtype=BF16)
    po = jnp.asarray(np.concatenate([zero, eye], axis=1), dtype=BF16)
    return jnp.asarray(np.eye(LANES, dtype=np.float32), dtype=BF16), pe, po


def _select_body(q_ref, kc_ref, vct_ref, ovt_ref, eye_ref, pe_ref, po_ref, ocmp_ref, bias_ref, *, n_sel):
    c = pl.program_id(1)
    ncmp = kc_ref.shape[-2]
    nblk = ovt_ref.shape[0]
    rows = NSA_GROUP * Q_BLOCK
    q = q_ref[...] * (ATTN_SCALE * LOG2E)
    lane = lax.broadcasted_iota(jnp.int32, (Q_BLOCK, LANES), 1)
    lo_half = lane < HEAD_DIM
    t_lane = c * Q_BLOCK + (lax.broadcasted_iota(jnp.int32, (1, rows), 1) & (Q_BLOCK - 1))
    n_idx = lax.broadcasted_iota(jnp.int32, (ncmp, 1), 0)
    neg = jnp.where((n_idx * CMP_STRIDE + (CMP_BLOCK - 1)) <= t_lane, 0.0, NEG_INF)
    jb = lax.broadcasted_iota(jnp.int32, (nblk, 1), 0)
    t_row = c * Q_BLOCK + lax.broadcasted_iota(jnp.int32, (1, Q_BLOCK), 1)
    cur = lax.shift_right_logical(t_row, 6)
    forced = (jb == 0) | (jb == cur) | (jb == cur - 1)
    causal = jb * SLC_BLOCK <= t_row
    probs, vals = [], []
    for hk in range(NSA_KV_HEADS):
        qts = []
        for g in range(NSA_GROUP):
            hq = hk * NSA_GROUP + g
            pair = q[:, (hq // 2) * LANES:(hq // 2 + 1) * LANES]
            qm = jnp.where(lo_half if hq % 2 == 0 else jnp.logical_not(lo_half), pair, 0.0).astype(BF16)
            qts.append(_dot_nt(eye_ref[...], qm).astype(BF16))
        s = _dot(kc_ref[0, hk], jnp.concatenate(qts, axis=1)) + neg
        m = jnp.max(s, axis=0, keepdims=True)
        e = jnp.exp2(s - m)
        l = jnp.sum(e, axis=0, keepdims=True)
        p = e * jnp.where(m > 0.5 * NEG_INF, 1.0 / l, 0.0)
        probs.append(p.astype(BF16))
        psum = p[:, 0:Q_BLOCK]
        for g in range(1, NSA_GROUP):
            psum = psum + p[:, g * Q_BLOCK:(g + 1) * Q_BLOCK]
        ph, plo = _split2(psum)
        imp = _dot(ovt_ref[...], ph) + _dot(ovt_ref[...], plo)
        vals.append(jnp.where(causal, imp + jnp.where(forced, FORCE_SELECT, 0.0), NEG_INF))
    val0 = jnp.concatenate(vals, axis=1)
    val = val0
    for _ in range(n_sel):
        m = jnp.max(val, axis=0, keepdims=True)
        idx = jnp.min(jnp.where(val == m, jb, nblk), axis=0, keepdims=True)
        val = jnp.where(jb == idx, TAKEN, val)
    bias = jnp.where((val == TAKEN) & (val0 > 0.5 * NEG_INF), 0.0, NEG_INF).astype(BF16)
    for hk in range(NSA_KV_HEADS):
        bias_ref[0, hk] = bias[:, hk * Q_BLOCK:(hk + 1) * Q_BLOCK]
        o_t = _dot(vct_ref[0, hk], probs[hk])
        _emit_heads(o_t, pe_ref, po_ref, ocmp_ref, hk * (NSA_GROUP // 2))


def _select(h, kcmp, vcmp_t, batch):
    n = h.shape[0]
    t = n // batch
    nq = t // Q_BLOCK
    ncmp = kcmp.shape[-2]
    nblk = t // SLC_BLOCK
    eye_q, pe, po = _head_placers()
    ii = np.arange(ncmp)[None, :]
    jj = np.arange(nblk)[:, None]
    ovt = ((ii * CMP_STRIDE < (jj + 1) * SLC_BLOCK) & (ii * CMP_STRIDE + CMP_BLOCK > jj * SLC_BLOCK)
           & (ii < ncmp - 1))
    ovt = jnp.asarray(ovt.astype(np.float32), dtype=BF16)
    full = lambda a: pl.BlockSpec(a.shape, lambda b, i: (0,) * a.ndim)
    per_batch = lambda a: pl.BlockSpec((1,) + a.shape[1:], lambda b, i: (b,) + (0,) * (a.ndim - 1))
    return pl.pallas_call(
        functools.partial(_select_body, n_sel=min(N_SLC, nblk)),
        grid=(batch, nq),
        in_specs=[pl.BlockSpec((Q_BLOCK, NSA_WIDTH), lambda b, i: (b * nq + i, COL_NQ // NSA_WIDTH)),
                  per_batch(kcmp), per_batch(vcmp_t), full(ovt), full(eye_q), full(pe), full(po)],
        out_specs=[pl.BlockSpec((Q_BLOCK, NSA_WIDTH), lambda b, i: (b * nq + i, 0)),
                   pl.BlockSpec((1, NSA_KV_HEADS, nblk, Q_BLOCK), lambda b, i: (b, 0, 0, i))],
        out_shape=[jax.ShapeDtypeStruct((n, NSA_WIDTH), F32),
                   jax.ShapeDtypeStruct((batch, NSA_KV_HEADS, nblk, t), BF16)],
        compiler_params=_params("parallel", "parallel"),
        name="nsa_select",
    )(h, kcmp, vcmp_t, ovt, eye_q, pe, po)


def _softmax_tile_t(s, m_prev):
    m_new = jnp.maximum(m_prev, jnp.max(s, axis=0, keepdims=True))
    return m_new, jnp.exp2(s - m_new).astype(BF16)


def _attn_body(q_ref, bias_ref, kaug_ref, vst_ref, kw_ref, vwt_ref, eye_ref, pe_ref, po_ref, oslc_ref, owin_ref,
               qaug_sc, acc_sc, sa_sc, sb_sc):
    c = pl.program_id(2)
    rows = NSA_GROUP * Q_BLOCK
    q4 = q_ref[...] * (ATTN_SCALE * LOG2E)
    lane = lax.broadcasted_iota(jnp.int32, (Q_BLOCK, LANES), 1)
    lo_half = lane < HEAD_DIM
    bias = bias_ref[0, 0]
    for g in range(NSA_GROUP):
        pair = q4[:, (g // 2) * LANES:(g // 2 + 1) * LANES]
        qm = jnp.where(lo_half if g % 2 == 0 else jnp.logical_not(lo_half), pair, 0.0).astype(BF16)
        qaug_sc[0:LANES, g * Q_BLOCK:(g + 1) * Q_BLOCK] = _dot_nt(eye_ref[...], qm).astype(BF16)
        qaug_sc[LANES:, g * Q_BLOCK:(g + 1) * Q_BLOCK] = bias
    t_row = c * Q_BLOCK + (lax.broadcasted_iota(jnp.int32, (1, rows), 1) & (Q_BLOCK - 1))

    acc_sc[...] = jnp.zeros(acc_sc.shape, F32)
    last_tile = kaug_ref.shape[2] // ATTN_TK - 1

    def scores(tile, s_ref, masked):
        start = pl.multiple_of(jnp.minimum(tile, last_tile) * ATTN_TK, ATTN_TK)
        s = _dot(kaug_ref[0, 0, pl.ds(start, ATTN_TK), :], qaug_sc[...])
        if masked:
            kpos = tile * ATTN_TK + lax.broadcasted_iota(jnp.int32, (ATTN_TK, 1), 0)
            s = jnp.where(kpos <= t_row, s, NEG_INF)
        s_ref[...] = s
        return jnp.max(s, axis=0, keepdims=True)

    def accumulate(tile, m_prev, m_tile, s_ref):
        p = jnp.exp2(s_ref[...] - m_tile).astype(BF16)
        pv = _dot(vst_ref[0, 0, jnp.minimum(tile, last_tile)], p)
        m_new = jnp.maximum(m_prev, m_tile)
        acc_sc[...] = jnp.exp2(m_prev - m_new) * acc_sc[...] + jnp.exp2(m_tile - m_new) * pv
        return m_new

    def two_tiles(t0, carry, masked):
        m, mt_a = carry
        mt_b = scores(t0 + 1, sb_sc, masked)
        m = accumulate(t0, m, mt_a, sa_sc)
        mt_a = scores(t0 + 2, sa_sc, masked)
        m = accumulate(t0 + 1, m, mt_b, sb_sc)
        return m, mt_a

    n_full = (c * Q_BLOCK) // ATTN_TK
    n_loop = jnp.maximum(n_full - 1, 0) // 2
    mt_0 = scores(0, sa_sc, True)

    span = WINDOW + Q_BLOCK
    wblk = jnp.maximum(c - WINDOW // Q_BLOCK, 0)
    wstart = pl.multiple_of(wblk * Q_BLOCK, Q_BLOCK)
    s = _dot(kw_ref[0, 0, pl.ds(wstart, span), :], qaug_sc[0:LANES, :])
    kpos = wstart + lax.broadcasted_iota(jnp.int32, (span, 1), 0)
    ok = (kpos <= t_row) & (kpos > t_row - WINDOW)
    _, p = _softmax_tile_t(jnp.where(ok, s, NEG_INF), jnp.full((1, rows), TAKEN, F32))
    acc = _dot(vwt_ref[0, 0, wblk], p[0:Q_BLOCK])
    for i in range(1, span // Q_BLOCK):
        acc = acc + _dot(vwt_ref[0, 0, wblk + i], p[i * Q_BLOCK:(i + 1) * Q_BLOCK])
    _emit_heads(acc[0:HEAD_DIM] / acc[HEAD_DIM:HEAD_DIM + 1], pe_ref, po_ref, owin_ref)

    carry = (jnp.full((1, rows), TAKEN, F32), mt_0)
    carry = lax.fori_loop(0, n_loop // 2,
                          lambda i, cr: two_tiles(4 * i + 2, two_tiles(4 * i, cr, False), False), carry)
    carry = lax.fori_loop(n_loop - n_loop % 2, n_loop, lambda i, cr: two_tiles(2 * i, cr, False), carry)
    m, mt_a = two_tiles(2 * n_loop, carry, True)
    accumulate(2 * n_loop + 2, m, mt_a, sa_sc)
    acc = acc_sc[...]
    _emit_heads(acc[0:HEAD_DIM] / acc[HEAD_DIM:HEAD_DIM + 1], pe_ref, po_ref, oslc_ref)


def _attention(h, bias_t, kaug, vst, kw, vwt, batch):
    n = h.shape[0]
    t = n // batch
    nq = t // Q_BLOCK
    gw = NSA_GROUP * HEAD_DIM
    rows = NSA_GROUP * Q_BLOCK
    eye_q, pe, po = _head_placers()
    res = lambda a: pl.BlockSpec((1, 1) + a.shape[2:], lambda b, k, i: (b, k) + (0,) * (a.ndim - 2))
    full = lambda a: pl.BlockSpec(a.shape, lambda b, k, i: (0,) * a.ndim)
    out_spec = pl.BlockSpec((Q_BLOCK, gw), lambda b, k, i: (b * nq + i, k))
    return pl.pallas_call(
        _attn_body,
        grid=(batch, NSA_KV_HEADS, nq),
        in_specs=[pl.BlockSpec((Q_BLOCK, gw), lambda b, k, i: (b * nq + i, COL_NQ // gw + k)),
                  pl.BlockSpec((1, 1, bias_t.shape[2], Q_BLOCK), lambda b, k, i: (b, k, 0, i)),
                  res(kaug), res(vst), res(kw), res(vwt), full(eye_q), full(pe), full(po)],
        out_specs=[out_spec, out_spec],
        out_shape=[jax.ShapeDtypeStruct((n, NSA_WIDTH), F32)] * 2,
        scratch_shapes=[pltpu.VMEM((kaug.shape[-1], rows), BF16), pltpu.VMEM((V_ROWS, rows), F32),
                        pltpu.VMEM((ATTN_TK, rows), F32), pltpu.VMEM((ATTN_TK, rows), F32)],
        compiler_params=_params("parallel", "parallel", "arbitrary"),
        name="nsa_attention",
    )(h, bias_t, kaug, vst, kw, vwt, eye_q, pe, po)


def _branch_expanders():
    e = np.zeros((N_BRANCH, GATE_PAD, NSA_WIDTH), np.float32)
    for hq in range(NSA_HEADS):
        for br in range(N_BRANCH):
            e[br, hq * N_BRANCH + br, hq * HEAD_DIM:(hq + 1) * HEAD_DIM] = 1.0
    return jnp.asarray(e, dtype=BF16)


def _route(logits):
    lane = lax.broadcasted_iota(jnp.int32, logits.shape, 1)
    is_g = lane < N_GROUPS
    gl = jnp.where(is_g, logits, -jnp.inf)
    gmax = jnp.max(gl, axis=-1, keepdims=True)
    gsum = jnp.sum(jnp.where(is_g, jnp.exp(logits - gmax), 0.0), axis=-1, keepdims=True)
    g_val = 1.0 / gsum
    g_idx = jnp.min(jnp.where(is_g & (logits == gmax), lane, ROUTER_PAD), axis=-1, keepdims=True)
    e_lo = EXPERT_LANE0 + g_idx * EXPERTS_PER_GROUP
    in_sel = (lane >= e_lo) & (lane < e_lo + EXPERTS_PER_GROUP)
    el = jnp.where(in_sel, logits, -jnp.inf)
    m1 = jnp.max(el, axis=-1, keepdims=True)
    i1 = jnp.min(jnp.where(in_sel & (logits == m1), lane, ROUTER_PAD), axis=-1, keepdims=True)
    el2 = jnp.where(lane == i1, -jnp.inf, el)
    m2 = jnp.max(el2, axis=-1, keepdims=True)
    i2 = jnp.min(jnp.where(el2 == m2, lane, ROUTER_PAD), axis=-1, keepdims=True)
    r = jnp.exp(m2 - m1)
    w1 = 1.0 / (1.0 + r)
    w2 = r * w1
    return g_val * (jnp.where(lane == i1, w1, 0.0) + jnp.where(lane == i2, w2, 0.0))


def _out_body(ya_ref, ocmp_ref, oslc_ref, owin_ref, gl_ref, yc_ref, hg_ref, x_ref, gb_ref, ex_ref,
              gmb_ref, gmc_ref, og_ref, w_ref, lg_ref, lb_ref, wrh_ref, wrl_ref, br_ref, x1_ref, gate_ref):
    g = _sigmoid(gl_ref[...] + gb_ref[...])
    yb = (_dot2(g, ex_ref[0]) * ocmp_ref[...] + _dot2(g, ex_ref[1]) * oslc_ref[...]
          + _dot2(g, ex_ref[2]) * owin_ref[...])
    n_ab = GM_WIDTH + NSA_WIDTH
    ybn = yb * lax.rsqrt(_dot2(yb * yb, gmb_ref[...]) + RMS_EPS) * og_ref[:, GM_WIDTH:n_ab]
    yc = yc_ref[...]
    hg = hg_ref[...]
    ycn = yc * lax.rsqrt(_dot2(yc * yc, gmc_ref[...]) + RMS_EPS) * og_ref[:, n_ab:] * (hg * _sigmoid(hg))
    y = (_dot(ya_ref[...].astype(BF16), w_ref[0:GM_WIDTH, :]) + _dot(ybn.astype(BF16), w_ref[GM_WIDTH:n_ab, :])
         + _dot(ycn.astype(BF16), w_ref[n_ab:, :]))
    x1 = _layer_norm(DEEPNORM_ALPHA * x_ref[...] + y, lg_ref[...], lb_ref[...])
    x1_ref[...] = x1
    xh, xl = _split2(x1)
    logits = _dot(xh, wrh_ref[...]) + _dot(xh, wrl_ref[...]) + _dot(xl, wrh_ref[...]) + br_ref[...]
    gate_ref[...] = _route(logits)


def _outproj(ya, ocmp, oslc, owin, h, yc, x2, gate_b, out_gain, w_out, ln_g, ln_b, wr, br):
    n, d = x2.shape
    tm = 256
    row = lambda a: a.reshape(1, -1)
    gb = jnp.pad(gate_b, (0, GATE_PAD - gate_b.shape[0])).reshape(1, GATE_PAD)
    wrh = wr.astype(BF16)
    wrl = (wr - wrh.astype(F32)).astype(BF16)
    ex = _branch_expanders()
    full = lambda shape: pl.BlockSpec(shape, lambda i: (0,) * len(shape))
    tile = lambda w, cb=0: pl.BlockSpec((tm, w), lambda i: (i, cb))
    return pl.pallas_call(
        _out_body,
        grid=(n // tm,),
        in_specs=[tile(GM_WIDTH), tile(NSA_WIDTH), tile(NSA_WIDTH), tile(NSA_WIDTH),
                  tile(GATE_PAD, COL_NG // GATE_PAD), tile(HG_WIDTH), tile(HG_WIDTH, COL_HG // HG_WIDTH), tile(d),
                  full((1, GATE_PAD)), full(ex.shape), full((NSA_WIDTH, NSA_WIDTH)), full((HG_WIDTH, HG_WIDTH)),
                  full((1, d)), full((d, d)), full((1, d)), full((1, d)),
                  full((d, ROUTER_PAD)), full((d, ROUTER_PAD)), full((1, ROUTER_PAD))],
        out_specs=[tile(d), tile(ROUTER_PAD)],
        out_shape=[jax.ShapeDtypeStruct((n, d), F32), jax.ShapeDtypeStruct((n, ROUTER_PAD), F32)],
        compiler_params=_params("parallel"),
        name="outproj_ln_router",
    )(ya, ocmp, oslc, owin, h, yc, h, x2, gb, ex, _group_mean_matrix(NSA_WIDTH), _group_mean_matrix(HG_WIDTH),
      row(out_gain), w_out.astype(BF16), row(ln_g), row(ln_b), wrh, wrl, br)


def _moe_body(x_ref, gate_ref, wg_ref, wu_ref, wd_ref, lg_ref, lb_ref, o_ref,
              xb_sc, acc_sc, gatet_sc, slott_sc, cnt_sc, pstack_sc, ystack_sc):
    e = pl.program_id(1)
    tm = x_ref.shape[0]

    @pl.when(e == 0)
    def _():
        xb_sc[...] = x_ref[...].astype(BF16)
        acc_sc[...] = jnp.zeros_like(acc_sc)
        used = jnp.where(gate_ref[...] > 0.0, 1.0, 0.0)
        r = lax.broadcasted_iota(jnp.int32, (tm, tm), 0)
        q = lax.broadcasted_iota(jnp.int32, (tm, tm), 1)
        after = jnp.where(r < q, 1.0, 0.0).astype(BF16)
        ub = used.astype(BF16)
        ri = lax.broadcasted_iota(jnp.int32, (ROUTER_PAD, ROUTER_PAD), 0)
        qi = lax.broadcasted_iota(jnp.int32, (ROUTER_PAD, ROUTER_PAD), 1)
        eye = jnp.where(ri == qi, 1.0, 0.0).astype(BF16)
        used_t = _dot_nt(eye, ub)
        slott_sc[...] = jnp.where(used_t > 0.0, _dot(used_t.astype(BF16), after), -1.0)
        g_hi, g_mid = _split2(gate_ref[...])
        g_lo = (gate_ref[...] - g_hi.astype(F32) - g_mid.astype(F32)).astype(BF16)
        gatet_sc[...] = _dot_nt(eye, g_hi) + _dot_nt(eye, g_mid) + _dot_nt(eye, g_lo)
        cnt_sc[...] = jnp.broadcast_to(jnp.sum(used, axis=0, keepdims=True), cnt_sc.shape)

    lane = lax.broadcasted_iota(jnp.int32, gate_ref.shape, 1)
    mine = lane == e + EXPERT_LANE0
    grow = gatet_sc[pl.ds(e + EXPERT_LANE0, 1), :]
    srow = slott_sc[pl.ds(e + EXPERT_LANE0, 1), :]
    n_rows = jnp.sum(jnp.where(mine[0:1], cnt_sc[0:1], 0.0)).astype(jnp.int32)

    def expert_rows(ci):
        base = (ci * MOE_CHUNK).astype(F32)
        r_col = base + lax.broadcasted_iota(jnp.int32, (MOE_CHUNK, 1), 0).astype(F32)
        hit = srow == r_col
        pick = jnp.where(hit, 1.0, 0.0).astype(BF16)
        g_r = jnp.sum(jnp.where(hit, grow, 0.0), axis=-1, keepdims=True)
        xg = _dot(pick, xb_sc[...]).astype(BF16)
        hg = _dot(xg, wg_ref[0])
        hu = _dot(xg, wu_ref[0])
        y = (_dot((hg * _sigmoid(hg) * hu).astype(BF16), wd_ref[0]) * g_r).astype(BF16)
        return pick, y

    slot = lax.rem(e, MOE_GROUP)
    off = pl.multiple_of(slot * MOE_CHUNK, MOE_CHUNK)
    pick0, y0 = expert_rows(jnp.int32(0))
    pstack_sc[pl.ds(off, MOE_CHUNK), :] = pick0
    ystack_sc[pl.ds(off, MOE_CHUNK), :] = y0

    def overflow(ci, carry):
        pick, y = expert_rows(ci)
        acc_sc[...] += _dot_tn(pick, y)
        return carry

    lax.fori_loop(1, (n_rows + MOE_CHUNK - 1) // MOE_CHUNK, overflow, 0)

    @pl.when(slot == MOE_GROUP - 1)
    def _():
        acc_sc[...] += _dot_tn(pstack_sc[...], ystack_sc[...])

    @pl.when(e == pl.num_programs(1) - 1)
    def _():
        o_ref[...] = _layer_norm(DEEPNORM_ALPHA * x_ref[...] + acc_sc[...], lg_ref[...], lb_ref[...])


def _moe(x1, gate, wg, wu, wd, ln_g, ln_b):
    n, d = x1.shape
    ne, _, de = wg.shape
    tm = MOE_TM
    row = lambda a: a.reshape(1, -1)
    return pl.pallas_call(
        _moe_body,
        grid=(n // tm, ne),
        in_specs=[pl.BlockSpec((tm, d), lambda i, e: (i, 0)),
                  pl.BlockSpec((tm, ROUTER_PAD), lambda i, e: (i, 0)),
                  pl.BlockSpec((1, d, de), lambda i, e: (e, 0, 0)),
                  pl.BlockSpec((1, d, de), lambda i, e: (e, 0, 0)),
                  pl.BlockSpec((1, de, d), lambda i, e: (e, 0, 0)),
                  pl.BlockSpec((1, d), lambda i, e: (0, 0)),
                  pl.BlockSpec((1, d), lambda i, e: (0, 0))],
        out_specs=pl.BlockSpec((tm, d), lambda i, e: (i, 0)),
        out_shape=jax.ShapeDtypeStruct((n, d), F32),
        scratch_shapes=[pltpu.VMEM((tm, d), BF16), pltpu.VMEM((tm, d), F32),
                        pltpu.VMEM((ROUTER_PAD, tm), F32), pltpu.VMEM((ROUTER_PAD, tm), F32),
                        pltpu.VMEM((8, ROUTER_PAD), F32),
                        pltpu.VMEM((MOE_GROUP * MOE_CHUNK, tm), BF16), pltpu.VMEM((MOE_GROUP * MOE_CHUNK, d), BF16)],
        compiler_params=_params("parallel", "arbitrary"),
        name="moe_ln",
    )(x1, gate, wg, wu, wd, row(ln_g), row(ln_b))


def _prep_w_in(w_in):
    names = ("gu", "gv", "nq", "kc", "vc", "ks", "vs", "kw", "vw", "ng", "hq", "hf", "hi", "hg")
    widths = (GM_WIDTH, GM_WIDTH, NSA_WIDTH) + (NSA_KV_WIDTH,) * 6 + (N_BRANCH * NSA_HEADS,) + (HG_WIDTH,) * 4
    ends = np.cumsum(widths)
    col = {nm: w_in[:, int(e - wd):int(e)] for nm, e, wd in zip(names, ends, widths)}
    pad = jnp.zeros((w_in.shape[0], GATE_PAD - N_BRANCH * NSA_HEADS), w_in.dtype)
    heads = lambda a: [a[:, i * HEAD_DIM:(i + 1) * HEAD_DIM] for i in range(NSA_KV_HEADS)]
    kdup = [a for k in (col["ks"], col["kw"]) for hd in heads(k) for a in (hd, hd)]
    w = jnp.concatenate([col[nm] for nm in ("gu", "gv", "nq", "kc", "vc", "hq", "hf", "hi", "hg", "ng")]
                        + [pad] + kdup, axis=1)
    wvt = jnp.concatenate([col["vs"], col["vw"]], axis=1).T
    return w.astype(BF16), wvt.astype(BF16)


def _layer(x2, batch, w_in, gm_v_gain, gm_v_bias, gm_w_s, gm_b_s, cmp_pos, cmp_w1, cmp_w2, nsa_gate_b, lb,
           out_gain, w_out, ln1_g, ln1_b, wr, br, wg, wu, wd, ln2_g, ln2_b):
    n = x2.shape[0]
    t = n // batch
    h, kaug, kw, vst, vwt = _inproj(x2, *_prep_w_in(w_in), batch)

    ya = _gmlp(h, gm_v_gain, gm_v_bias, gm_w_s, gm_b_s, out_gain[:GM_WIDTH])
    yc = _hgrn(h, lb, batch)

    kvc = h[:, COL_KC:COL_KC + 2 * NSA_KV_WIDTH].reshape(batch, t, 2, NSA_KV_HEADS, HEAD_DIM)
    kvc = kvc.transpose(0, 2, 3, 1, 4).reshape(batch, 2, NSA_KV_HEADS, t // CMP_STRIDE, CMP_STRIDE * HEAD_DIM)
    kcmp, vcmp_t = _compress(kvc, cmp_w1, cmp_w2, cmp_pos.reshape(2, 1, CMP_BLOCK * HEAD_DIM))
    ocmp, bias_t = _select(h, kcmp, vcmp_t, batch)
    oslc, owin = _attention(h, bias_t, kaug, vst, kw, vwt, batch)

    x1, gate = _outproj(ya, ocmp, oslc, owin, h, yc, x2, nsa_gate_b, out_gain, w_out, ln1_g, ln1_b, wr, br)
    return _moe(x1, gate, wg, wu, wd, ln2_g, ln2_b)


def kernel(x, w_in, gm_v_gain, gm_v_bias, gm_w_s, gm_b_s, cmp_pos, cmp_w1, cmp_w2, nsa_gate_b, hg_lower, out_gain, w_out, ln1_g, ln1_b, router_group_w, router_group_b, router_expert_w, router_expert_b, exp_w_gate, exp_w_up, exp_w_down, ln2_g, ln2_b):
    batch, t, d = x.shape
    depth = w_in.shape[0]
    lb_all = jnp.cumsum(jax.nn.softmax(hg_lower.astype(F32), axis=0), axis=0)
    lb_all = lb_all - lb_all[0]
    x2 = x.reshape(batch * t, d)
    for l in range(depth):
        pad = ROUTER_PAD - N_GROUPS - N_EXPERTS
        wr = jnp.concatenate([router_group_w[l], router_expert_w[l], jnp.zeros((d, pad), F32)], axis=1)
        br = jnp.concatenate([router_group_b[l], router_expert_b[l], jnp.zeros((pad,), F32)]).reshape(1, ROUTER_PAD)
        x2 = _layer(x2, batch, w_in[l], gm_v_gain[l], gm_v_bias[l], gm_w_s[l], gm_b_s[l], cmp_pos[l], cmp_w1[l],
                    cmp_w2[l], nsa_gate_b[l], lb_all[l], out_gain[l], w_out[l], ln1_g[l], ln1_b[l], wr, br,
                    exp_w_gate[l].astype(BF16), exp_w_up[l].astype(BF16), exp_w_down[l].astype(BF16),
                    ln2_g[l], ln2_b[l])
    return x2.reshape(batch, t, d)
```

```python
import functools

import numpy as np
import jax
import jax.numpy as jnp
from jax import lax
from jax.experimental import pallas as pl
from jax.experimental.pallas import tpu as pltpu

F32 = jnp.float32
BF16 = jnp.bfloat16

D_MODEL = 1024
HEAD_DIM = 64
GM_GROUPS = 4
GM_WIDTH = GM_GROUPS * HEAD_DIM
GM_CHUNK = 128
NSA_HEADS = 8
NSA_KV_HEADS = 2
NSA_GROUP = NSA_HEADS // NSA_KV_HEADS
NSA_WIDTH = NSA_HEADS * HEAD_DIM
NSA_KV_WIDTH = NSA_KV_HEADS * HEAD_DIM
CMP_BLOCK = 32
CMP_STRIDE = 16
SLC_BLOCK = 64
N_SLC = 16
WINDOW = 512
Q_BLOCK = 128
N_BRANCH = 3
HG_HEADS = 4
HG_WIDTH = HG_HEADS * HEAD_DIM
HG_CHUNK = 64
N_GROUPS = 4
EXPERTS_PER_GROUP = 4
N_EXPERTS = N_GROUPS * EXPERTS_PER_GROUP
D_EXPERT = 512
DEPTH = 2
DEEPNORM_ALPHA = (2.0 * DEPTH) ** 0.25
LN_EPS = 1e-5
RMS_EPS = 1e-6
NEG_INF = -1e30
FORCE_SELECT = 1e4
TAKEN = -(2.0 ** 126)
ATTN_SCALE = HEAD_DIM ** -0.5

LANES = 128
GATE_PAD = LANES
ROUTER_PAD = LANES
EXPERT_LANE0 = N_GROUPS
VMEM_LIMIT = 56 * 1024 * 1024

COL_GU = 0
COL_GV = 256
COL_NQ = 512
COL_KC = 1024
COL_HQ = 1280
COL_HF = 1536
COL_HI = 1792
COL_HG = 2048
COL_NG = 2304
H_COLS = COL_NG + GATE_PAD
KDUP_COLS = 4 * NSA_KV_WIDTH

ATTN_TK = 512
V_ROWS = HEAD_DIM + 16
LOG2E = 1.4426950408889634
MOE_TM = 1024
MOE_CHUNK = 160
MOE_GROUP = 8
MOE_EPS = 2


def _dot(a, b):
    return jnp.dot(a, b, preferred_element_type=F32)


def _dot_nt(a, b):
    return lax.dot_general(a, b, (((1,), (1,)), ((), ())), preferred_element_type=F32)


def _dot_tn(a, b):
    return lax.dot_general(a, b, (((0,), (0,)), ((), ())), preferred_element_type=F32)


def _split2(x):
    hi = x.astype(BF16)
    lo = (x - hi.astype(F32)).astype(BF16)
    return hi, lo


def _dot2(x, m, lhs=False):
    hi, lo = _split2(x)
    return _dot(m, hi) + _dot(m, lo) if lhs else _dot(hi, m) + _dot(lo, m)


def _gelu(x):
    return 0.5 * x * (1.0 + jnp.tanh(0.7978845608028654 * (x + 0.044715 * (x * x * x))))


def _sigmoid(x):
    return 1.0 / (1.0 + jnp.exp(-x))


def _log_sigmoid(x):
    return jnp.minimum(x, 0.0) - jnp.log1p(jnp.exp(-jnp.abs(x)))


def _layer_norm(z, g, b):
    mu = jnp.mean(z, axis=-1, keepdims=True)
    zc = z - mu
    var = jnp.mean(zc * zc, axis=-1, keepdims=True)
    return zc * lax.rsqrt(var + LN_EPS) * g + b


def _params(*sem):
    return pltpu.CompilerParams(dimension_semantics=sem, vmem_limit_bytes=VMEM_LIMIT)


def _inproj_body(x_ref, w_ref, wvt_ref, h_ref, kaug_ref, kw_ref, vst_ref, vwt_ref):
    xb = x_ref[...].astype(BF16)
    tm = xb.shape[0]
    acc = _dot(xb, w_ref[...])
    h_ref[...] = acc[:, :H_COLS]
    nblk = kaug_ref.shape[-1] - LANES
    pos = pl.program_id(1) * tm + lax.broadcasted_iota(jnp.int32, (tm, nblk), 0)
    blk = lax.broadcasted_iota(jnp.int32, (tm, nblk), 1)
    onehot = jnp.where(lax.shift_right_logical(pos, 6) == blk, 1.0, 0.0).astype(BF16)
    for hk in range(NSA_KV_HEADS):
        c0 = H_COLS + hk * LANES
        kaug_ref[0, hk, :, 0:LANES] = acc[:, c0:c0 + LANES].astype(BF16)
        kaug_ref[0, hk, :, LANES:] = onehot
        c1 = H_COLS + (NSA_KV_HEADS + hk) * LANES
        kw_ref[0, hk] = acc[:, c1:c1 + LANES].astype(BF16)
    vt = _dot_nt(wvt_ref[...], xb)
    ones_then_zeros = lambda w: jnp.where(
        lax.broadcasted_iota(jnp.int32, (V_ROWS - HEAD_DIM, w), 0) == 0, 1.0, 0.0).astype(BF16)
    tail, tail_q = ones_then_zeros(tm), ones_then_zeros(Q_BLOCK)
    for hk in range(NSA_KV_HEADS):
        vst_ref[0, hk, 0, 0:HEAD_DIM, :] = vt[hk * HEAD_DIM:(hk + 1) * HEAD_DIM].astype(BF16)
        vst_ref[0, hk, 0, HEAD_DIM:, :] = tail
        r0 = (NSA_KV_HEADS + hk) * HEAD_DIM
        for i in range(tm // Q_BLOCK):
            ls = slice(i * Q_BLOCK, (i + 1) * Q_BLOCK)
            vwt_ref[0, hk, i, 0:HEAD_DIM, :] = vt[r0:r0 + HEAD_DIM, ls].astype(BF16)
            vwt_ref[0, hk, i, HEAD_DIM:, :] = tail_q


def _inproj(x2, w, wvt, batch):
    n, d = x2.shape
    t = n // batch
    tm = ATTN_TK
    steps = t // tm
    nblk = t // SLC_BLOCK
    hk = NSA_KV_HEADS
    return pl.pallas_call(
        _inproj_body,
        grid=(batch, steps),
        in_specs=[pl.BlockSpec((tm, d), lambda b, j: (b * steps + j, 0)),
                  pl.BlockSpec(w.shape, lambda b, j: (0, 0)),
                  pl.BlockSpec(wvt.shape, lambda b, j: (0, 0))],
        out_specs=[pl.BlockSpec((tm, H_COLS), lambda b, j: (b * steps + j, 0)),
                   pl.BlockSpec((1, hk, tm, LANES + nblk), lambda b, j: (b, 0, j, 0)),
                   pl.BlockSpec((1, hk, tm, LANES), lambda b, j: (b, 0, j, 0)),
                   pl.BlockSpec((1, hk, 1, V_ROWS, tm), lambda b, j: (b, 0, j, 0, 0)),
                   pl.BlockSpec((1, hk, tm // Q_BLOCK, V_ROWS, Q_BLOCK), lambda b, j: (b, 0, j, 0, 0))],
        out_shape=[jax.ShapeDtypeStruct((n, H_COLS), F32),
                   jax.ShapeDtypeStruct((batch, hk, t, LANES + nblk), BF16),
                   jax.ShapeDtypeStruct((batch, hk, t, LANES), BF16),
                   jax.ShapeDtypeStruct((batch, hk, steps, V_ROWS, tm), BF16),
                   jax.ShapeDtypeStruct((batch, hk, t // Q_BLOCK, V_ROWS, Q_BLOCK), BF16)],
        compiler_params=_params("parallel", "parallel"),
        name="inproj",
    )(x2, w, wvt)


def _group_mean_matrix(width):
    g = np.arange(width) // HEAD_DIM
    return jnp.asarray((g[:, None] == g[None, :]).astype(np.float32) / HEAD_DIM, dtype=BF16)


def _gmlp_body(gu_ref, gv_ref, gain_ref, bias_ref, ws_ref, bsx_ref, gm_ref, og_ref, o_ref, *, chunks):
    gm = gm_ref[...]
    row = lax.broadcasted_iota(jnp.int32, (GM_CHUNK, GM_CHUNK), 0)
    col = lax.broadcasted_iota(jnp.int32, (GM_CHUNK, GM_CHUNK), 1)
    tril = row >= col
    lane_grp = lax.shift_right_logical(lax.broadcasted_iota(jnp.int32, (GM_CHUNK, GM_WIDTH), 1), 6)
    ws = [jnp.where(tril, ws_ref[g], 0.0).astype(BF16) for g in range(GM_GROUPS)]
    v = _gelu(gv_ref[...])
    vc = v - _dot2(v, gm)
    var = _dot2(vc * vc, gm)
    vn = (vc * lax.rsqrt(var + LN_EPS) * gain_ref[...] + bias_ref[...]).astype(BF16)
    zs = []
    for c in range(chunks):
        vn_c = vn[c * GM_CHUNK:(c + 1) * GM_CHUNK]
        z = bsx_ref[...]
        for g in range(GM_GROUPS):
            z = z + jnp.where(lane_grp == g, _dot(ws[g], vn_c), 0.0)
        zs.append(z)
    y = _gelu(gu_ref[...]) * jnp.concatenate(zs, axis=0)
    ms = _dot2(y * y, gm)
    o_ref[...] = y * lax.rsqrt(ms + RMS_EPS) * og_ref[...]


def _gmlp(h, gain, bias, w_s, b_s, out_gain_a):
    n = h.shape[0]
    chunks = 4
    tm = chunks * GM_CHUNK
    bsx = jnp.repeat(b_s.T, HEAD_DIM, axis=1)
    row = lambda a: a.reshape(1, -1)
    full = lambda shape: pl.BlockSpec(shape, lambda i: (0,) * len(shape))
    return pl.pallas_call(
        functools.partial(_gmlp_body, chunks=chunks),
        grid=(n // tm,),
        in_specs=[
            pl.BlockSpec((tm, GM_WIDTH), lambda i: (i, COL_GU // GM_WIDTH)),
            pl.BlockSpec((tm, GM_WIDTH), lambda i: (i, COL_GV // GM_WIDTH)),
            full((1, GM_WIDTH)), full((1, GM_WIDTH)),
            full((GM_GROUPS, GM_CHUNK, GM_CHUNK)), full((GM_CHUNK, GM_WIDTH)),
            full((GM_WIDTH, GM_WIDTH)), full((1, GM_WIDTH)),
        ],
        out_specs=pl.BlockSpec((tm, GM_WIDTH), lambda i: (i, 0)),
        out_shape=jax.ShapeDtypeStruct((n, GM_WIDTH), F32),
        compiler_params=_params("parallel"),
        name="gmlp",
    )(h, h, row(gain), row(bias), w_s, bsx, _group_mean_matrix(GM_WIDTH), row(out_gain_a))


N_LEVELS = 6


def _hgrn_constants():
    c = HG_CHUNK
    t = np.arange(c)[:, None]
    u = np.arange(c)[None, :]
    mats, masks = [], []
    m = c // 2
    while m >= 1:
        p = (t // (2 * m)) * (2 * m) + m - 1
        mats.append(np.where(t > p, (u > p) & (u <= t), (u > t) & (u <= p)))
        masks.append(((t // (2 * m)) == (u // (2 * m))) & ((t % (2 * m)) >= m) & ((u % (2 * m)) < m))
        m //= 2
    mats.append(u <= t)
    mats.append(u > t)
    masks.append(np.eye(c, dtype=bool))
    a_all = np.concatenate(mats, 0).astype(np.float32)
    mask2 = np.stack([np.concatenate([mk, mk], 0) for mk in masks]).astype(np.float32)
    return jnp.asarray(a_all, dtype=BF16), jnp.asarray(mask2)


def _hgrn_body(q_ref, f_ref, i_ref, loglb_ref, log1m_ref, oml_ref, a_ref, mk_ref, o_ref, st_ref, *, chunks):
    @pl.when(pl.program_id(1) == 0)
    def _():
        st_ref[...] = jnp.zeros_like(st_ref)

    c = HG_CHUNK
    a_all = a_ref[...]
    lane = lax.broadcasted_iota(jnp.int32, (c, LANES), 1)
    lo_half = lane < HEAD_DIM
    r2 = lax.broadcasted_iota(jnp.int32, (LANES, LANES), 0)
    c2 = lax.broadcasted_iota(jnp.int32, (LANES, LANES), 1)
    same_head = (r2 < HEAD_DIM) == (c2 < HEAD_DIM)
    fl = f_ref[...]
    la = loglb_ref[...]
    lb2 = log1m_ref[...] + _log_sigmoid(fl)
    lf_all = jnp.maximum(la, lb2) + jnp.log1p(jnp.exp(-jnp.abs(la - lb2)))
    k_all = oml_ref[...] * (1.0 / (1.0 + jnp.exp(fl)))
    e_alls = []
    for ch in range(chunks):
        d_all = _dot2(lf_all[ch * c:(ch + 1) * c], a_all, lhs=True)
        e_alls.append(jnp.exp(d_all))
    pending = []
    for ch in range(chunks):
        sl = slice(ch * c, (ch + 1) * c)
        q, k, iv, e_all = q_ref[sl, :], k_all[sl], i_ref[sl, :], e_alls[ch]
        for p in range(HG_HEADS // 2):
            ls = slice(p * LANES, (p + 1) * LANES)
            qp, kp, ip = q[:, ls], k[:, ls], iv[:, ls]
            att = jnp.zeros((2 * c, c), F32)
            for lv in range(N_LEVELS + 1):
                if lv < N_LEVELS:
                    el = e_all[lv * c:(lv + 1) * c, ls]
                    ql, kl = qp * el, kp * el
                else:
                    ql, kl = qp, kp
                qstack = jnp.concatenate([jnp.where(lo_half, ql, 0.0), jnp.where(lo_half, 0.0, ql)], axis=0)
                att = att + _dot_nt(qstack.astype(BF16), kl.astype(BF16)) * mk_ref[lv]
            ipb = ip.astype(BF16)
            o_intra = jnp.where(lo_half, _dot(att[:c].astype(BF16), ipb), _dot(att[c:].astype(BF16), ipb))
            eb = e_all[N_LEVELS * c:(N_LEVELS + 1) * c, ls]
            ebl = e_all[(N_LEVELS + 1) * c:(N_LEVELS + 2) * c, ls]
            upd = jnp.where(same_head, _dot_tn(ipb, (kp * ebl).astype(BF16)), 0.0)
            pending.append((sl, ls, p, o_intra, (qp * eb).astype(BF16), eb[c - 1:c, :], upd))
    states = [st_ref[p] for p in range(HG_HEADS // 2)]
    for sl, ls, p, o_intra, qe, decay, upd in pending:
        o_ref[sl, ls] = o_intra + _dot_nt(qe, states[p].astype(BF16))
        states[p] = states[p] * decay + upd
    for p in range(HG_HEADS // 2):
        st_ref[p] = states[p]


def _hgrn(h, lb, batch):
    n = h.shape[0]
    t = n // batch
    chunks = 4
    tm = chunks * HG_CHUNK
    steps = t // tm
    a_all, mask2 = _hgrn_constants()
    lb = lb.reshape(1, HG_WIDTH).astype(F32)
    full = lambda shape: pl.BlockSpec(shape, lambda b, i: (0,) * len(shape))
    col = lambda off: pl.BlockSpec((tm, HG_WIDTH), lambda b, i: (b * steps + i, off // HG_WIDTH))
    return pl.pallas_call(
        functools.partial(_hgrn_body, chunks=chunks),
        grid=(batch, steps),
        in_specs=[col(COL_HQ), col(COL_HF), col(COL_HI),
                  full((1, HG_WIDTH)), full((1, HG_WIDTH)), full((1, HG_WIDTH)),
                  full(a_all.shape), full(mask2.shape)],
        out_specs=pl.BlockSpec((tm, HG_WIDTH), lambda b, i: (b * steps + i, 0)),
        out_shape=jax.ShapeDtypeStruct((n, HG_WIDTH), F32),
        scratch_shapes=[pltpu.VMEM((HG_HEADS // 2, LANES, LANES), F32)],
        compiler_params=_params("parallel", "arbitrary"),
        name="hgrn2",
    )(h, h, h, jnp.log(lb), jnp.log1p(-lb), 1.0 - lb, a_all, mask2)


def _compress_body(x_ref, w1_ref, w2k_ref, w2vt_ref, pos_ref, kc_ref, vct_ref):
    def hidden(kv):
        x = x_ref[0, kv, 0].astype(BF16)
        w1 = w1_ref[kv].astype(BF16)
        half = w1.shape[0] // 2
        a = _dot(x, w1[:half])
        b = _dot(x, w1[half:])
        pos = jnp.broadcast_to(pos_ref[kv], (8, pos_ref.shape[-1])).astype(BF16)
        pw = _dot(pos, w1)[0:1]
        b_next = jnp.concatenate([b[1:], b[:1]], axis=0)
        return _gelu(a + b_next + pw).astype(BF16)

    nrow = x_ref.shape[3]
    kc = _dot(hidden(0), w2k_ref[...].astype(BF16))
    rid = lax.broadcasted_iota(jnp.int32, kc.shape, 0)
    kc_ref[0, 0] = jnp.where(rid < nrow - 1, kc, 0.0).astype(BF16)
    vct = _dot_nt(w2vt_ref[...].astype(BF16), hidden(1))
    cid = lax.broadcasted_iota(jnp.int32, vct.shape, 1)
    vct_ref[0, 0] = jnp.where(cid < nrow - 1, vct, 0.0).astype(BF16)


def _compress(xkv, w1, w2, pos):
    b, _, hk, nr, wd = xkv.shape
    dh = w2.shape[-1]
    w2k = jnp.concatenate([w2[0], w2[0]], axis=-1)
    w2vt = w2[1].T
    full = lambda a: pl.BlockSpec(a.shape, lambda i, k: (0,) * a.ndim)
    return pl.pallas_call(
        _compress_body,
        grid=(b, hk),
        in_specs=[pl.BlockSpec((1, 2, 1, nr, wd), lambda i, k: (i, 0, k, 0, 0)),
                  full(w1), full(w2k), full(w2vt), full(pos)],
        out_specs=[pl.BlockSpec((1, 1, nr, 2 * dh), lambda i, k: (i, k, 0, 0)),
                   pl.BlockSpec((1, 1, dh, nr), lambda i, k: (i, k, 0, 0))],
        out_shape=[jax.ShapeDtypeStruct((b, hk, nr, 2 * dh), BF16),
                   jax.ShapeDtypeStruct((b, hk, dh, nr), BF16)],
        compiler_params=_params("parallel", "parallel"),
        name="nsa_compress",
    )(xkv, w1, w2k, w2vt, pos)


def _emit_heads(o_t, pe_ref, po_ref, out_ref, pair0=0):
    hi, lo = _split2(o_t)
    for j in range(NSA_GROUP // 2):
        a = slice(2 * j * Q_BLOCK, (2 * j + 1) * Q_BLOCK)
        b = slice((2 * j + 1) * Q_BLOCK, (2 * j + 2) * Q_BLOCK)
        out_ref[:, (pair0 + j) * LANES:(pair0 + j + 1) * LANES] = (
            _dot_tn(hi[:, a], pe_ref[...]) + _dot_tn(lo[:, a], pe_ref[...])
            + _dot_tn(hi[:, b], po_ref[...]) + _dot_tn(lo[:, b], po_ref[...]))


def _head_placers():
    eye = np.eye(HEAD_DIM, dtype=np.float32)
    zero = np.zeros_like(eye)
    pe = jnp.asarray(np.concatenate([eye, zero], axis=1), dtype=BF16)
    po = jnp.asarray(np.concatenate([zero, eye], axis=1), dtype=BF16)
    return jnp.asarray(np.eye(LANES, dtype=np.float32), dtype=BF16), pe, po


def _select_body(q_ref, kc_ref, vct_ref, ovt_ref, eye_ref, pe_ref, po_ref, ocmp_ref, bias_ref, *, n_sel):
    c = pl.program_id(1)
    ncmp = kc_ref.shape[-2]
    nblk = ovt_ref.shape[0]
    rows = NSA_GROUP * Q_BLOCK
    q = q_ref[...] * (ATTN_SCALE * LOG2E)
    lane = lax.broadcasted_iota(jnp.int32, (Q_BLOCK, LANES), 1)
    lo_half = lane < HEAD_DIM
    t_lane = c * Q_BLOCK + (lax.broadcasted_iota(jnp.int32, (1, rows), 1) & (Q_BLOCK - 1))
    n_idx = lax.broadcasted_iota(jnp.int32, (ncmp, 1), 0)
    neg = jnp.where((n_idx * CMP_STRIDE + (CMP_BLOCK - 1)) <= t_lane, 0.0, NEG_INF)
    jb = lax.broadcasted_iota(jnp.int32, (nblk, 1), 0)
    t_row = c * Q_BLOCK + lax.broadcasted_iota(jnp.int32, (1, Q_BLOCK), 1)
    cur = lax.shift_right_logical(t_row, 6)
    forced = (jb == 0) | (jb == cur) | (jb == cur - 1)
    causal = jb * SLC_BLOCK <= t_row
    probs, vals, scores = [], [], []
    for hk in range(NSA_KV_HEADS):
        qts = []
        for g in range(NSA_GROUP):
            hq = hk * NSA_GROUP + g
            pair = q[:, (hq // 2) * LANES:(hq // 2 + 1) * LANES]
            qm = jnp.where(lo_half if hq % 2 == 0 else jnp.logical_not(lo_half), pair, 0.0).astype(BF16)
            qts.append(_dot_nt(eye_ref[...], qm).astype(BF16))
        scores.append(_dot(kc_ref[0, hk], jnp.concatenate(qts, axis=1)) + neg)
    s = jnp.concatenate(scores, axis=1)
    m = jnp.max(s, axis=0, keepdims=True)
    e = jnp.exp2(s - m)
    l = jnp.sum(e, axis=0, keepdims=True)
    p_all = e * jnp.where(m > 0.5 * NEG_INF, 1.0 / l, 0.0)
    for hk in range(NSA_KV_HEADS):
        p = p_all[:, hk * rows:(hk + 1) * rows]
        probs.append(p.astype(BF16))
        psum = p[:, 0:Q_BLOCK]
        for g in range(1, NSA_GROUP):
            psum = psum + p[:, g * Q_BLOCK:(g + 1) * Q_BLOCK]
        ph, plo = _split2(psum)
        imp = _dot(ovt_ref[...], ph) + _dot(ovt_ref[...], plo)
        vals.append(jnp.where(causal, imp + jnp.where(forced, FORCE_SELECT, 0.0), NEG_INF))
    val0 = jnp.concatenate(vals, axis=1)
    val = val0
    for _ in range(n_sel):
        m = jnp.max(val, axis=0, keepdims=True)
        idx = jnp.min(jnp.where(val == m, jb, nblk), axis=0, keepdims=True)
        val = jnp.where(jb == idx, TAKEN, val)
    bias = jnp.where((val == TAKEN) & (val0 > 0.5 * NEG_INF), 0.0, NEG_INF).astype(BF16)
    for hk in range(NSA_KV_HEADS):
        bias_ref[0, hk] = bias[:, hk * Q_BLOCK:(hk + 1) * Q_BLOCK]
        o_t = _dot(vct_ref[0, hk], probs[hk])
        _emit_heads(o_t, pe_ref, po_ref, ocmp_ref, hk * (NSA_GROUP // 2))


def _select(h, kcmp, vcmp_t, batch):
    n = h.shape[0]
    t = n // batch
    nq = t // Q_BLOCK
    ncmp = kcmp.shape[-2]
    nblk = t // SLC_BLOCK
    eye_q, pe, po = _head_placers()
    ii = np.arange(ncmp)[None, :]
    jj = np.arange(nblk)[:, None]
    ovt = ((ii * CMP_STRIDE < (jj + 1) * SLC_BLOCK) & (ii * CMP_STRIDE + CMP_BLOCK > jj * SLC_BLOCK)
           & (ii < ncmp - 1))
    ovt = jnp.asarray(ovt.astype(np.float32), dtype=BF16)
    full = lambda a: pl.BlockSpec(a.shape, lambda b, i: (0,) * a.ndim)
    per_batch = lambda a: pl.BlockSpec((1,) + a.shape[1:], lambda b, i: (b,) + (0,) * (a.ndim - 1))
    return pl.pallas_call(
        functools.partial(_select_body, n_sel=min(N_SLC, nblk)),
        grid=(batch, nq),
        in_specs=[pl.BlockSpec((Q_BLOCK, NSA_WIDTH), lambda b, i: (b * nq + i, COL_NQ // NSA_WIDTH)),
                  per_batch(kcmp), per_batch(vcmp_t), full(ovt), full(eye_q), full(pe), full(po)],
        out_specs=[pl.BlockSpec((Q_BLOCK, NSA_WIDTH), lambda b, i: (b * nq + i, 0)),
                   pl.BlockSpec((1, NSA_KV_HEADS, nblk, Q_BLOCK), lambda b, i: (b, 0, 0, i))],
        out_shape=[jax.ShapeDtypeStruct((n, NSA_WIDTH), F32),
                   jax.ShapeDtypeStruct((batch, NSA_KV_HEADS, nblk, t), BF16)],
        compiler_params=_params("parallel", "parallel"),
        name="nsa_select",
    )(h, kcmp, vcmp_t, ovt, eye_q, pe, po)


def _softmax_tile_t(s, m_prev):
    m_new = jnp.maximum(m_prev, jnp.max(s, axis=0, keepdims=True))
    return m_new, jnp.exp2(s - m_new).astype(BF16)


def _attn_body(q_ref, bias_ref, kaug_ref, vst_ref, kw_ref, vwt_ref, eye_ref, pe_ref, po_ref, oslc_ref, owin_ref,
               qaug_sc, acc_sc, sa_sc, sb_sc):
    c = pl.program_id(2)
    rows = NSA_GROUP * Q_BLOCK
    q4 = q_ref[...] * (ATTN_SCALE * LOG2E)
    lane = lax.broadcasted_iota(jnp.int32, (Q_BLOCK, LANES), 1)
    lo_half = lane < HEAD_DIM
    bias = bias_ref[0, 0]
    for g in range(NSA_GROUP):
        pair = q4[:, (g // 2) * LANES:(g // 2 + 1) * LANES]
        qm = jnp.where(lo_half if g % 2 == 0 else jnp.logical_not(lo_half), pair, 0.0).astype(BF16)
        qaug_sc[0:LANES, g * Q_BLOCK:(g + 1) * Q_BLOCK] = _dot_nt(eye_ref[...], qm).astype(BF16)
        qaug_sc[LANES:, g * Q_BLOCK:(g + 1) * Q_BLOCK] = bias
    t_row = c * Q_BLOCK + (lax.broadcasted_iota(jnp.int32, (1, rows), 1) & (Q_BLOCK - 1))

    acc_sc[...] = jnp.zeros(acc_sc.shape, F32)
    last_tile = kaug_ref.shape[2] // ATTN_TK - 1

    def scores(tile, s_ref, masked):
        start = pl.multiple_of(jnp.minimum(tile, last_tile) * ATTN_TK, ATTN_TK)
        s = _dot(kaug_ref[0, 0, pl.ds(start, ATTN_TK), :], qaug_sc[...])
        if masked:
            kpos = tile * ATTN_TK + lax.broadcasted_iota(jnp.int32, (ATTN_TK, 1), 0)
            s = jnp.where(kpos <= t_row, s, NEG_INF)
        s_ref[...] = s
        return jnp.max(s, axis=0, keepdims=True)

    def accumulate(tile, m_prev, m_tile, s_ref):
        p = jnp.exp2(s_ref[...] - m_tile).astype(BF16)
        pv = _dot(vst_ref[0, 0, jnp.minimum(tile, last_tile)], p)
        m_new = jnp.maximum(m_prev, m_tile)
        acc_sc[...] = jnp.exp2(m_prev - m_new) * acc_sc[...] + jnp.exp2(m_tile - m_new) * pv
        return m_new

    def two_tiles(t0, carry, masked):
        m, mt_a = carry
        mt_b = scores(t0 + 1, sb_sc, masked)
        m = accumulate(t0, m, mt_a, sa_sc)
        mt_a = scores(t0 + 2, sa_sc, masked)
        m = accumulate(t0 + 1, m, mt_b, sb_sc)
        return m, mt_a

    n_full = (c * Q_BLOCK) // ATTN_TK
    n_loop = jnp.maximum(n_full - 1, 0) // 2
    span = WINDOW + Q_BLOCK
    wblk = jnp.maximum(c - WINDOW // Q_BLOCK, 0)
    wstart = pl.multiple_of(wblk * Q_BLOCK, Q_BLOCK)
    s = _dot(kw_ref[0, 0, pl.ds(wstart, span), :], qaug_sc[0:LANES, :])
    mt_0 = scores(0, sa_sc, True)
    kpos = wstart + lax.broadcasted_iota(jnp.int32, (span, 1), 0)
    ok = (kpos <= t_row) & (kpos > t_row - WINDOW)
    _, p = _softmax_tile_t(jnp.where(ok, s, NEG_INF), jnp.full((1, rows), TAKEN, F32))
    acc = _dot(vwt_ref[0, 0, wblk], p[0:Q_BLOCK])
    for i in range(1, span // Q_BLOCK):
        acc = acc + _dot(vwt_ref[0, 0, wblk + i], p[i * Q_BLOCK:(i + 1) * Q_BLOCK])
    _emit_heads(acc[0:HEAD_DIM] / acc[HEAD_DIM:HEAD_DIM + 1], pe_ref, po_ref, owin_ref)

    carry = (jnp.full((1, rows), TAKEN, F32), mt_0)
    carry = lax.fori_loop(0, n_loop // 2,
                          lambda i, cr: two_tiles(4 * i + 2, two_tiles(4 * i, cr, False), False), carry)
    carry = lax.fori_loop(n_loop - n_loop % 2, n_loop, lambda i, cr: two_tiles(2 * i, cr, False), carry)
    m, mt_a = two_tiles(2 * n_loop, carry, True)
    accumulate(2 * n_loop + 2, m, mt_a, sa_sc)
    acc = acc_sc[...]
    _emit_heads(acc[0:HEAD_DIM] / acc[HEAD_DIM:HEAD_DIM + 1], pe_ref, po_ref, oslc_ref)


def _attention(h, bias_t, kaug, vst, kw, vwt, batch):
    n = h.shape[0]
    t = n // batch
    nq = t // Q_BLOCK
    gw = NSA_GROUP * HEAD_DIM
    rows = NSA_GROUP * Q_BLOCK
    eye_q, pe, po = _head_placers()
    res = lambda a: pl.BlockSpec((1, 1) + a.shape[2:], lambda b, k, i: (b, k) + (0,) * (a.ndim - 2))
    full = lambda a: pl.BlockSpec(a.shape, lambda b, k, i: (0,) * a.ndim)
    out_spec = pl.BlockSpec((Q_BLOCK, gw), lambda b, k, i: (b * nq + i, k))
    return pl.pallas_call(
        _attn_body,
        grid=(batch, NSA_KV_HEADS, nq),
        in_specs=[pl.BlockSpec((Q_BLOCK, gw), lambda b, k, i: (b * nq + i, COL_NQ // gw + k)),
                  pl.BlockSpec((1, 1, bias_t.shape[2], Q_BLOCK), lambda b, k, i: (b, k, 0, i)),
                  res(kaug), res(vst), res(kw), res(vwt), full(eye_q), full(pe), full(po)],
        out_specs=[out_spec, out_spec],
        out_shape=[jax.ShapeDtypeStruct((n, NSA_WIDTH), F32)] * 2,
        scratch_shapes=[pltpu.VMEM((kaug.shape[-1], rows), BF16), pltpu.VMEM((V_ROWS, rows), F32),
                        pltpu.VMEM((ATTN_TK, rows), F32), pltpu.VMEM((ATTN_TK, rows), F32)],
        compiler_params=_params("parallel", "parallel", "arbitrary"),
        name="nsa_attention",
    )(h, bias_t, kaug, vst, kw, vwt, eye_q, pe, po)


def _branch_expanders():
    e = np.zeros((N_BRANCH, GATE_PAD, NSA_WIDTH), np.float32)
    for hq in range(NSA_HEADS):
        for br in range(N_BRANCH):
            e[br, hq * N_BRANCH + br, hq * HEAD_DIM:(hq + 1) * HEAD_DIM] = 1.0
    return jnp.asarray(e, dtype=BF16)


def _route(logits):
    lane = lax.broadcasted_iota(jnp.int32, logits.shape, 1)
    is_g = lane < N_GROUPS
    gl = jnp.where(is_g, logits, -jnp.inf)
    gmax = jnp.max(gl, axis=-1, keepdims=True)
    gsum = jnp.sum(jnp.where(is_g, jnp.exp(logits - gmax), 0.0), axis=-1, keepdims=True)
    g_val = 1.0 / gsum
    g_idx = jnp.min(jnp.where(is_g & (logits == gmax), lane, ROUTER_PAD), axis=-1, keepdims=True)
    e_lo = EXPERT_LANE0 + g_idx * EXPERTS_PER_GROUP
    in_sel = (lane >= e_lo) & (lane < e_lo + EXPERTS_PER_GROUP)
    el = jnp.where(in_sel, logits, -jnp.inf)
    m1 = jnp.max(el, axis=-1, keepdims=True)
    i1 = jnp.min(jnp.where(in_sel & (logits == m1), lane, ROUTER_PAD), axis=-1, keepdims=True)
    el2 = jnp.where(lane == i1, -jnp.inf, el)
    m2 = jnp.max(el2, axis=-1, keepdims=True)
    i2 = jnp.min(jnp.where(el2 == m2, lane, ROUTER_PAD), axis=-1, keepdims=True)
    r = jnp.exp(m2 - m1)
    w1 = 1.0 / (1.0 + r)
    w2 = r * w1
    return g_val * (jnp.where(lane == i1, w1, 0.0) + jnp.where(lane == i2, w2, 0.0))


def _out_body(ya_ref, ocmp_ref, oslc_ref, owin_ref, gl_ref, yc_ref, hg_ref, x_ref, gb_ref, ex_ref,
              gmb_ref, gmc_ref, og_ref, w_ref, lg_ref, lb_ref, wrh_ref, wrl_ref, br_ref, x1_ref, gate_ref):
    g = _sigmoid(gl_ref[...] + gb_ref[...])
    yb = (_dot2(g, ex_ref[0]) * ocmp_ref[...] + _dot2(g, ex_ref[1]) * oslc_ref[...]
          + _dot2(g, ex_ref[2]) * owin_ref[...])
    n_ab = GM_WIDTH + NSA_WIDTH
    ybn = yb * lax.rsqrt(_dot((yb * yb).astype(BF16), gmb_ref[...]) + RMS_EPS) * og_ref[:, GM_WIDTH:n_ab]
    yc = yc_ref[...]
    hg = hg_ref[...]
    ycn = (yc * lax.rsqrt(_dot((yc * yc).astype(BF16), gmc_ref[...]) + RMS_EPS) * og_ref[:, n_ab:]
           * (hg * _sigmoid(hg)))
    y = (_dot(ya_ref[...].astype(BF16), w_ref[0:GM_WIDTH, :]) + _dot(ybn.astype(BF16), w_ref[GM_WIDTH:n_ab, :])
         + _dot(ycn.astype(BF16), w_ref[n_ab:, :]))
    x1 = _layer_norm(DEEPNORM_ALPHA * x_ref[...] + y, lg_ref[...], lb_ref[...])
    x1_ref[...] = x1
    xh, xl = _split2(x1)
    logits = _dot(xh, wrh_ref[...]) + _dot(xh, wrl_ref[...]) + _dot(xl, wrh_ref[...]) + br_ref[...]
    gate_ref[...] = _route(logits)


def _outproj(ya, ocmp, oslc, owin, h, yc, x2, gate_b, out_gain, w_out, ln_g, ln_b, wr, br):
    n, d = x2.shape
    tm = 512
    row = lambda a: a.reshape(1, -1)
    gb = jnp.pad(gate_b, (0, GATE_PAD - gate_b.shape[0])).reshape(1, GATE_PAD)
    wrh = wr.astype(BF16)
    wrl = (wr - wrh.astype(F32)).astype(BF16)
    ex = _branch_expanders()
    full = lambda shape: pl.BlockSpec(shape, lambda i: (0,) * len(shape))
    tile = lambda w, cb=0: pl.BlockSpec((tm, w), lambda i: (i, cb))
    return pl.pallas_call(
        _out_body,
        grid=(n // tm,),
        in_specs=[tile(GM_WIDTH), tile(NSA_WIDTH), tile(NSA_WIDTH), tile(NSA_WIDTH),
                  tile(GATE_PAD, COL_NG // GATE_PAD), tile(HG_WIDTH), tile(HG_WIDTH, COL_HG // HG_WIDTH), tile(d),
                  full((1, GATE_PAD)), full(ex.shape), full((NSA_WIDTH, NSA_WIDTH)), full((HG_WIDTH, HG_WIDTH)),
                  full((1, d)), full((d, d)), full((1, d)), full((1, d)),
                  full((d, ROUTER_PAD)), full((d, ROUTER_PAD)), full((1, ROUTER_PAD))],
        out_specs=[tile(d), tile(ROUTER_PAD)],
        out_shape=[jax.ShapeDtypeStruct((n, d), F32), jax.ShapeDtypeStruct((n, ROUTER_PAD), F32)],
        compiler_params=_params("parallel"),
        name="outproj_ln_router",
    )(ya, ocmp, oslc, owin, h, yc, h, x2, gb, ex, _group_mean_matrix(NSA_WIDTH), _group_mean_matrix(HG_WIDTH),
      row(out_gain), w_out.astype(BF16), row(ln_g), row(ln_b), wrh, wrl, br)


def _moe_body(x_ref, gate_ref, wg_ref, wu_ref, wd_ref, lg_ref, lb_ref, o_ref,
              xb_sc, acc_sc, gatet_sc, slott_sc, cnt_sc, pstack_sc, ystack_sc):
    step = pl.program_id(1)
    tm = x_ref.shape[0]

    @pl.when(step == 0)
    def _():
        xb_sc[...] = x_ref[...].astype(BF16)
        acc_sc[...] = jnp.zeros_like(acc_sc)
        used = jnp.where(gate_ref[...] > 0.0, 1.0, 0.0)
        r = lax.broadcasted_iota(jnp.int32, (tm, tm), 0)
        q = lax.broadcasted_iota(jnp.int32, (tm, tm), 1)
        after = jnp.where(r < q, 1.0, 0.0).astype(BF16)
        ub = used.astype(BF16)
        ri = lax.broadcasted_iota(jnp.int32, (ROUTER_PAD, ROUTER_PAD), 0)
        qi = lax.broadcasted_iota(jnp.int32, (ROUTER_PAD, ROUTER_PAD), 1)
        eye = jnp.where(ri == qi, 1.0, 0.0).astype(BF16)
        used_t = _dot_nt(eye, ub)
        slott_sc[...] = jnp.where(used_t > 0.0, _dot(used_t.astype(BF16), after), -1.0)
        g_hi, g_mid = _split2(gate_ref[...])
        g_lo = (gate_ref[...] - g_hi.astype(F32) - g_mid.astype(F32)).astype(BF16)
        gatet_sc[...] = _dot_nt(eye, g_hi) + _dot_nt(eye, g_mid) + _dot_nt(eye, g_lo)
        cnt_sc[...] = jnp.broadcast_to(jnp.sum(used, axis=0, keepdims=True), cnt_sc.shape)

    lane = lax.broadcasted_iota(jnp.int32, (1, ROUTER_PAD), 1)
    for sub in range(MOE_EPS):
        e = step * MOE_EPS + sub
        grow = gatet_sc[pl.ds(e + EXPERT_LANE0, 1), :]
        srow = slott_sc[pl.ds(e + EXPERT_LANE0, 1), :]
        n_rows = jnp.sum(jnp.where(lane == e + EXPERT_LANE0, cnt_sc[0:1], 0.0)).astype(jnp.int32)

        def expert_rows(ci, grow=grow, srow=srow, sub=sub):
            base = (ci * MOE_CHUNK).astype(F32)
            r_col = base + lax.broadcasted_iota(jnp.int32, (MOE_CHUNK, 1), 0).astype(F32)
            hit = srow == r_col
            pick = jnp.where(hit, 1.0, 0.0).astype(BF16)
            g_r = jnp.sum(jnp.where(hit, grow, 0.0), axis=-1, keepdims=True)
            xg = _dot(pick, xb_sc[...]).astype(BF16)
            hg = _dot(xg, wg_ref[sub])
            hu = _dot(xg, wu_ref[sub])
            y = (_dot((hg * _sigmoid(hg) * hu).astype(BF16), wd_ref[sub]) * g_r).astype(BF16)
            return pick, y

        off = pl.multiple_of(lax.rem(e, MOE_GROUP) * MOE_CHUNK, MOE_CHUNK)
        pick0, y0 = expert_rows(jnp.int32(0))
        pstack_sc[pl.ds(off, MOE_CHUNK), :] = pick0
        ystack_sc[pl.ds(off, MOE_CHUNK), :] = y0

        def overflow(ci, carry, expert_rows=expert_rows):
            pick, y = expert_rows(ci)
            acc_sc[...] += _dot_tn(pick, y)
            return carry

        lax.fori_loop(1, (n_rows + MOE_CHUNK - 1) // MOE_CHUNK, overflow, 0)

    @pl.when(lax.rem(step, MOE_GROUP // MOE_EPS) == MOE_GROUP // MOE_EPS - 1)
    def _():
        acc_sc[...] += _dot_tn(pstack_sc[...], ystack_sc[...])

    @pl.when(step == pl.num_programs(1) - 1)
    def _():
        o_ref[...] = _layer_norm(DEEPNORM_ALPHA * x_ref[...] + acc_sc[...], lg_ref[...], lb_ref[...])


def _moe(x1, gate, wg, wu, wd, ln_g, ln_b):
    n, d = x1.shape
    ne, _, de = wg.shape
    tm = MOE_TM
    row = lambda a: a.reshape(1, -1)
    return pl.pallas_call(
        _moe_body,
        grid=(n // tm, ne // MOE_EPS),
        in_specs=[pl.BlockSpec((tm, d), lambda i, e: (i, 0)),
                  pl.BlockSpec((tm, ROUTER_PAD), lambda i, e: (i, 0)),
                  pl.BlockSpec((MOE_EPS, d, de), lambda i, e: (e, 0, 0)),
                  pl.BlockSpec((MOE_EPS, d, de), lambda i, e: (e, 0, 0)),
                  pl.BlockSpec((MOE_EPS, de, d), lambda i, e: (e, 0, 0)),
                  pl.BlockSpec((1, d), lambda i, e: (0, 0)),
                  pl.BlockSpec((1, d), lambda i, e: (0, 0))],
        out_specs=pl.BlockSpec((tm, d), lambda i, e: (i, 0)),
        out_shape=jax.ShapeDtypeStruct((n, d), F32),
        scratch_shapes=[pltpu.VMEM((tm, d), BF16), pltpu.VMEM((tm, d), F32),
                        pltpu.VMEM((ROUTER_PAD, tm), F32), pltpu.VMEM((ROUTER_PAD, tm), F32),
                        pltpu.VMEM((8, ROUTER_PAD), F32),
                        pltpu.VMEM((MOE_GROUP * MOE_CHUNK, tm), BF16), pltpu.VMEM((MOE_GROUP * MOE_CHUNK, d), BF16)],
        compiler_params=_params("parallel", "arbitrary"),
        name="moe_ln",
    )(x1, gate, wg, wu, wd, row(ln_g), row(ln_b))


def _prep_w_in(w_in):
    names = ("gu", "gv", "nq", "kc", "vc", "ks", "vs", "kw", "vw", "ng", "hq", "hf", "hi", "hg")
    widths = (GM_WIDTH, GM_WIDTH, NSA_WIDTH) + (NSA_KV_WIDTH,) * 6 + (N_BRANCH * NSA_HEADS,) + (HG_WIDTH,) * 4
    ends = np.cumsum(widths)
    col = {nm: w_in[:, int(e - wd):int(e)] for nm, e, wd in zip(names, ends, widths)}
    pad = jnp.zeros((w_in.shape[0], GATE_PAD - N_BRANCH * NSA_HEADS), w_in.dtype)
    heads = lambda a: [a[:, i * HEAD_DIM:(i + 1) * HEAD_DIM] for i in range(NSA_KV_HEADS)]
    kdup = [a for k in (col["ks"], col["kw"]) for hd in heads(k) for a in (hd, hd)]
    w = jnp.concatenate([col[nm] for nm in ("gu", "gv", "nq", "kc", "vc", "hq", "hf", "hi", "hg", "ng")]
                        + [pad] + kdup, axis=1)
    wvt = jnp.concatenate([col["vs"], col["vw"]], axis=1).T
    return w.astype(BF16), wvt.astype(BF16)


def _layer(x2, batch, w_in, gm_v_gain, gm_v_bias, gm_w_s, gm_b_s, cmp_pos, cmp_w1, cmp_w2, nsa_gate_b, lb,
           out_gain, w_out, ln1_g, ln1_b, wr, br, wg, wu, wd, ln2_g, ln2_b):
    n = x2.shape[0]
    t = n // batch
    h, kaug, kw, vst, vwt = _inproj(x2, *_prep_w_in(w_in), batch)

    ya = _gmlp(h, gm_v_gain, gm_v_bias, gm_w_s, gm_b_s, out_gain[:GM_WIDTH])
    yc = _hgrn(h, lb, batch)

    kvc = h[:, COL_KC:COL_KC + 2 * NSA_KV_WIDTH].reshape(batch, t, 2, NSA_KV_HEADS, HEAD_DIM)
    kvc = kvc.transpose(0, 2, 3, 1, 4).reshape(batch, 2, NSA_KV_HEADS, t // CMP_STRIDE, CMP_STRIDE * HEAD_DIM)
    kcmp, vcmp_t = _compress(kvc, cmp_w1, cmp_w2, cmp_pos.reshape(2, 1, CMP_BLOCK * HEAD_DIM))
    ocmp, bias_t = _select(h, kcmp, vcmp_t, batch)
    oslc, owin = _attention(h, bias_t, kaug, vst, kw, vwt, batch)

    x1, gate = _outproj(ya, ocmp, oslc, owin, h, yc, x2, nsa_gate_b, out_gain, w_out, ln1_g, ln1_b, wr, br)
    return _moe(x1, gate, wg, wu, wd, ln2_g, ln2_b)


def kernel(x, w_in, gm_v_gain, gm_v_bias, gm_w_s, gm_b_s, cmp_pos, cmp_w1, cmp_w2, nsa_gate_b, hg_lower, out_gain, w_out, ln1_g, ln1_b, router_group_w, router_group_b, router_expert_w, router_expert_b, exp_w_gate, exp_w_up, exp_w_down, ln2_g, ln2_b):
    batch, t, d = x.shape
    depth = w_in.shape[0]
    lb_all = jnp.cumsum(jax.nn.softmax(hg_lower.astype(F32), axis=0), axis=0)
    lb_all = lb_all - lb_all[0]
    x2 = x.reshape(batch * t, d)
    for l in range(depth):
        pad = ROUTER_PAD - N_GROUPS - N_EXPERTS
        wr = jnp.concatenate([router_group_w[l], router_expert_w[l], jnp.zeros((d, pad), F32)], axis=1)
        br = jnp.concatenate([router_group_b[l], router_expert_b[l], jnp.zeros((pad,), F32)]).reshape(1, ROUTER_PAD)
        x2 = _layer(x2, batch, w_in[l], gm_v_gain[l], gm_v_bias[l], gm_w_s[l], gm_b_s[l], cmp_pos[l], cmp_w1[l],
                    cmp_w2[l], nsa_gate_b[l], lb_all[l], out_gain[l], w_out[l], ln1_g[l], ln1_b[l], wr, br,
                    exp_w_gate[l].astype(BF16), exp_w_up[l].astype(BF16), exp_w_down[l].astype(BF16),
                    ln2_g[l], ln2_b[l])
    return x2.reshape(batch, t, d)
```

```python
import functools

import numpy as np
import jax
import jax.numpy as jnp
from jax import lax
from jax.experimental import pallas as pl
from jax.experimental.pallas import tpu as pltpu

F32 = jnp.float32
BF16 = jnp.bfloat16

HEAD_DIM = 64
HEAD_SHIFT = HEAD_DIM.bit_length() - 1
GM_GROUPS = 4
GM_WIDTH = GM_GROUPS * HEAD_DIM
GM_CHUNK = 128
NSA_HEADS = 8
NSA_KV_HEADS = 2
NSA_GROUP = NSA_HEADS // NSA_KV_HEADS
NSA_WIDTH = NSA_HEADS * HEAD_DIM
NSA_KV_WIDTH = NSA_KV_HEADS * HEAD_DIM
CMP_BLOCK = 32
CMP_STRIDE = 16
SLC_BLOCK = 64
SLC_SHIFT = SLC_BLOCK.bit_length() - 1
N_SLC = 16
WINDOW = 512
Q_BLOCK = 128
N_BRANCH = 3
HG_HEADS = 4
HG_WIDTH = HG_HEADS * HEAD_DIM
HG_CHUNK = 64
N_GROUPS = 4
EXPERTS_PER_GROUP = 4
N_EXPERTS = N_GROUPS * EXPERTS_PER_GROUP
DEPTH = 2
DEEPNORM_ALPHA = (2.0 * DEPTH) ** 0.25
LN_EPS = 1e-5
RMS_EPS = 1e-6
NEG_INF = -1e30
FORCE_SELECT = 1e4
TAKEN = -(2.0 ** 126)
ATTN_SCALE = HEAD_DIM ** -0.5

LANES = 128
GATE_PAD = LANES
ROUTER_PAD = LANES
EXPERT_LANE0 = N_GROUPS
VMEM_LIMIT = 56 * 1024 * 1024

COL_GU = 0
COL_GV = 256
COL_NQ = 512
COL_KC = 1024
COL_HQ = 1280
COL_HF = 1536
COL_HI = 1792
COL_HG = 2048
COL_NG = 2304
H_COLS = COL_NG + GATE_PAD
KDUP_COLS = 4 * NSA_KV_WIDTH

ATTN_TK = 512
V_ROWS = HEAD_DIM + 16
LOG2E = 1.4426950408889634
MOE_TM = 1024
MOE_CHUNK = 160
MOE_GROUP = 8
MOE_EPS = 2


def _dot(a, b):
    return jnp.dot(a, b, preferred_element_type=F32)


def _dot_nt(a, b):
    return lax.dot_general(a, b, (((1,), (1,)), ((), ())), preferred_element_type=F32)


def _dot_tn(a, b):
    return lax.dot_general(a, b, (((0,), (0,)), ((), ())), preferred_element_type=F32)


def _split2(x):
    hi = x.astype(BF16)
    lo = (x - hi.astype(F32)).astype(BF16)
    return hi, lo


def _dot2(x, m, lhs=False):
    hi, lo = _split2(x)
    return _dot(m, hi) + _dot(m, lo) if lhs else _dot(hi, m) + _dot(lo, m)


def _gelu(x):
    return 0.5 * x * (1.0 + jnp.tanh(0.7978845608028654 * (x + 0.044715 * (x * x * x))))


def _sigmoid(x):
    return 1.0 / (1.0 + jnp.exp(-x))


def _log_sigmoid(x):
    return jnp.minimum(x, 0.0) - jnp.log1p(jnp.exp(-jnp.abs(x)))


def _layer_norm(z, g, b):
    mu = jnp.mean(z, axis=-1, keepdims=True)
    zc = z - mu
    var = jnp.mean(zc * zc, axis=-1, keepdims=True)
    return zc * lax.rsqrt(var + LN_EPS) * g + b


def _params(*sem):
    return pltpu.CompilerParams(dimension_semantics=sem, vmem_limit_bytes=VMEM_LIMIT)


def _inproj_body(x_ref, w_ref, wvt_ref, h_ref, kaug_ref, kw_ref, vst_ref, vwt_ref):
    xb = x_ref[...].astype(BF16)
    tm = xb.shape[0]
    acc = _dot(xb, w_ref[...])
    h_ref[...] = acc[:, :H_COLS]
    nblk = kaug_ref.shape[-1] - LANES
    pos = pl.program_id(1) * tm + lax.broadcasted_iota(jnp.int32, (tm, nblk), 0)
    blk = lax.broadcasted_iota(jnp.int32, (tm, nblk), 1)
    onehot = jnp.where(lax.shift_right_logical(pos, SLC_SHIFT) == blk, 1.0, 0.0).astype(BF16)
    for hk in range(NSA_KV_HEADS):
        c0 = H_COLS + hk * LANES
        kaug_ref[0, hk, :, 0:LANES] = acc[:, c0:c0 + LANES].astype(BF16)
        kaug_ref[0, hk, :, LANES:] = onehot
        c1 = H_COLS + (NSA_KV_HEADS + hk) * LANES
        kw_ref[0, hk] = acc[:, c1:c1 + LANES].astype(BF16)
    vt = _dot_nt(wvt_ref[...], xb)
    ones_then_zeros = lambda w: jnp.where(
        lax.broadcasted_iota(jnp.int32, (V_ROWS - HEAD_DIM, w), 0) == 0, 1.0, 0.0).astype(BF16)
    tail, tail_q = ones_then_zeros(tm), ones_then_zeros(Q_BLOCK)
    for hk in range(NSA_KV_HEADS):
        vst_ref[0, hk, 0, 0:HEAD_DIM, :] = vt[hk * HEAD_DIM:(hk + 1) * HEAD_DIM].astype(BF16)
        vst_ref[0, hk, 0, HEAD_DIM:, :] = tail
        r0 = (NSA_KV_HEADS + hk) * HEAD_DIM
        for i in range(tm // Q_BLOCK):
            ls = slice(i * Q_BLOCK, (i + 1) * Q_BLOCK)
            vwt_ref[0, hk, i, 0:HEAD_DIM, :] = vt[r0:r0 + HEAD_DIM, ls].astype(BF16)
            vwt_ref[0, hk, i, HEAD_DIM:, :] = tail_q


def _inproj(x2, w, wvt, batch):
    n, d = x2.shape
    t = n // batch
    tm = ATTN_TK
    steps = t // tm
    nblk = t // SLC_BLOCK
    hk = NSA_KV_HEADS
    return pl.pallas_call(
        _inproj_body,
        grid=(batch, steps),
        in_specs=[pl.BlockSpec((tm, d), lambda b, j: (b * steps + j, 0)),
                  pl.BlockSpec(w.shape, lambda b, j: (0, 0)),
                  pl.BlockSpec(wvt.shape, lambda b, j: (0, 0))],
        out_specs=[pl.BlockSpec((tm, H_COLS), lambda b, j: (b * steps + j, 0)),
                   pl.BlockSpec((1, hk, tm, LANES + nblk), lambda b, j: (b, 0, j, 0)),
                   pl.BlockSpec((1, hk, tm, LANES), lambda b, j: (b, 0, j, 0)),
                   pl.BlockSpec((1, hk, 1, V_ROWS, tm), lambda b, j: (b, 0, j, 0, 0)),
                   pl.BlockSpec((1, hk, tm // Q_BLOCK, V_ROWS, Q_BLOCK), lambda b, j: (b, 0, j, 0, 0))],
        out_shape=[jax.ShapeDtypeStruct((n, H_COLS), F32),
                   jax.ShapeDtypeStruct((batch, hk, t, LANES + nblk), BF16),
                   jax.ShapeDtypeStruct((batch, hk, t, LANES), BF16),
                   jax.ShapeDtypeStruct((batch, hk, steps, V_ROWS, tm), BF16),
                   jax.ShapeDtypeStruct((batch, hk, t // Q_BLOCK, V_ROWS, Q_BLOCK), BF16)],
        compiler_params=_params("parallel", "parallel"),
        name="inproj",
    )(x2, w, wvt)


def _group_mean_matrix(width):
    g = np.arange(width) // HEAD_DIM
    return jnp.asarray((g[:, None] == g[None, :]).astype(np.float32) / HEAD_DIM, dtype=BF16)


def _gmlp_body(gu_ref, gv_ref, gain_ref, bias_ref, ws_ref, bsx_ref, gm_ref, og_ref, o_ref, *, chunks):
    gm = gm_ref[...]
    row = lax.broadcasted_iota(jnp.int32, (GM_CHUNK, GM_CHUNK), 0)
    col = lax.broadcasted_iota(jnp.int32, (GM_CHUNK, GM_CHUNK), 1)
    tril = row >= col
    lane_grp = lax.shift_right_logical(lax.broadcasted_iota(jnp.int32, (GM_CHUNK, GM_WIDTH), 1), HEAD_SHIFT)
    ws = [jnp.where(tril, ws_ref[g], 0.0).astype(BF16) for g in range(GM_GROUPS)]
    v = _gelu(gv_ref[...])
    vc = v - _dot2(v, gm)
    var = _dot2(vc * vc, gm)
    vn = (vc * lax.rsqrt(var + LN_EPS) * gain_ref[...] + bias_ref[...]).astype(BF16)
    zs = []
    for c in range(chunks):
        vn_c = vn[c * GM_CHUNK:(c + 1) * GM_CHUNK]
        z = bsx_ref[...]
        for g in range(GM_GROUPS):
            z = z + jnp.where(lane_grp == g, _dot(ws[g], vn_c), 0.0)
        zs.append(z)
    y = _gelu(gu_ref[...]) * jnp.concatenate(zs, axis=0)
    ms = _dot2(y * y, gm)
    o_ref[...] = y * lax.rsqrt(ms + RMS_EPS) * og_ref[...]


def _gmlp(h, gain, bias, w_s, b_s, out_gain_a):
    n = h.shape[0]
    chunks = 4
    tm = chunks * GM_CHUNK
    bsx = jnp.repeat(b_s.T, HEAD_DIM, axis=1)
    row = lambda a: a.reshape(1, -1)
    full = lambda shape: pl.BlockSpec(shape, lambda i: (0,) * len(shape))
    return pl.pallas_call(
        functools.partial(_gmlp_body, chunks=chunks),
        grid=(n // tm,),
        in_specs=[
            pl.BlockSpec((tm, GM_WIDTH), lambda i: (i, COL_GU // GM_WIDTH)),
            pl.BlockSpec((tm, GM_WIDTH), lambda i: (i, COL_GV // GM_WIDTH)),
            full((1, GM_WIDTH)), full((1, GM_WIDTH)),
            full((GM_GROUPS, GM_CHUNK, GM_CHUNK)), full((GM_CHUNK, GM_WIDTH)),
            full((GM_WIDTH, GM_WIDTH)), full((1, GM_WIDTH)),
        ],
        out_specs=pl.BlockSpec((tm, GM_WIDTH), lambda i: (i, 0)),
        out_shape=jax.ShapeDtypeStruct((n, GM_WIDTH), F32),
        compiler_params=_params("parallel"),
        name="gmlp",
    )(h, h, row(gain), row(bias), w_s, bsx, _group_mean_matrix(GM_WIDTH), row(out_gain_a))


N_LEVELS = 6


def _hgrn_constants():
    c = HG_CHUNK
    t = np.arange(c)[:, None]
    u = np.arange(c)[None, :]
    mats, masks = [], []
    m = c // 2
    while m >= 1:
        p = (t // (2 * m)) * (2 * m) + m - 1
        mats.append(np.where(t > p, (u > p) & (u <= t), (u > t) & (u <= p)))
        masks.append(((t // (2 * m)) == (u // (2 * m))) & ((t % (2 * m)) >= m) & ((u % (2 * m)) < m))
        m //= 2
    mats.append(u <= t)
    mats.append(u > t)
    masks.append(np.eye(c, dtype=bool))
    a_all = np.concatenate(mats, 0).astype(np.float32)
    mask2 = np.stack([np.concatenate([mk, mk], 0) for mk in masks]).astype(np.float32)
    return jnp.asarray(a_all, dtype=BF16), jnp.asarray(mask2)


def _hgrn_body(q_ref, f_ref, i_ref, loglb_ref, log1m_ref, oml_ref, a_ref, mk_ref, o_ref, st_ref, *, chunks):
    @pl.when(pl.program_id(1) == 0)
    def _():
        st_ref[...] = jnp.zeros_like(st_ref)

    c = HG_CHUNK
    a_all = a_ref[...]
    lane = lax.broadcasted_iota(jnp.int32, (c, LANES), 1)
    lo_half = lane < HEAD_DIM
    r2 = lax.broadcasted_iota(jnp.int32, (LANES, LANES), 0)
    c2 = lax.broadcasted_iota(jnp.int32, (LANES, LANES), 1)
    same_head = (r2 < HEAD_DIM) == (c2 < HEAD_DIM)
    fl = f_ref[...]
    la = loglb_ref[...]
    lb2 = log1m_ref[...] + _log_sigmoid(fl)
    lf_all = jnp.maximum(la, lb2) + jnp.log1p(jnp.exp(-jnp.abs(la - lb2)))
    k_all = oml_ref[...] * (1.0 / (1.0 + jnp.exp(fl)))
    e_alls = []
    for ch in range(chunks):
        d_all = _dot2(lf_all[ch * c:(ch + 1) * c], a_all, lhs=True)
        e_alls.append(jnp.exp(d_all))
    pending = []
    for ch in range(chunks):
        sl = slice(ch * c, (ch + 1) * c)
        q, k, iv, e_all = q_ref[sl, :], k_all[sl], i_ref[sl, :], e_alls[ch]
        for p in range(HG_HEADS // 2):
            ls = slice(p * LANES, (p + 1) * LANES)
            qp, kp, ip = q[:, ls], k[:, ls], iv[:, ls]
            att = jnp.zeros((2 * c, c), F32)
            for lv in range(N_LEVELS + 1):
                if lv < N_LEVELS:
                    el = e_all[lv * c:(lv + 1) * c, ls]
                    ql, kl = qp * el, kp * el
                else:
                    ql, kl = qp, kp
                qstack = jnp.concatenate([jnp.where(lo_half, ql, 0.0), jnp.where(lo_half, 0.0, ql)], axis=0)
                att = att + _dot_nt(qstack.astype(BF16), kl.astype(BF16)) * mk_ref[lv]
            ipb = ip.astype(BF16)
            o_intra = jnp.where(lo_half, _dot(att[:c].astype(BF16), ipb), _dot(att[c:].astype(BF16), ipb))
            eb = e_all[N_LEVELS * c:(N_LEVELS + 1) * c, ls]
            ebl = e_all[(N_LEVELS + 1) * c:(N_LEVELS + 2) * c, ls]
            upd = jnp.where(same_head, _dot_tn(ipb, (kp * ebl).astype(BF16)), 0.0)
            pending.append((sl, ls, p, o_intra, (qp * eb).astype(BF16), eb[c - 1:c, :], upd))
    states = [st_ref[p] for p in range(HG_HEADS // 2)]
    for sl, ls, p, o_intra, qe, decay, upd in pending:
        o_ref[sl, ls] = o_intra + _dot_nt(qe, states[p].astype(BF16))
        states[p] = states[p] * decay + upd
    for p in range(HG_HEADS // 2):
        st_ref[p] = states[p]


def _hgrn(h, lb, batch):
    n = h.shape[0]
    t = n // batch
    chunks = 4
    tm = chunks * HG_CHUNK
    steps = t // tm
    a_all, mask2 = _hgrn_constants()
    lb = lb.reshape(1, HG_WIDTH).astype(F32)
    full = lambda shape: pl.BlockSpec(shape, lambda b, i: (0,) * len(shape))
    col = lambda off: pl.BlockSpec((tm, HG_WIDTH), lambda b, i: (b * steps + i, off // HG_WIDTH))
    return pl.pallas_call(
        functools.partial(_hgrn_body, chunks=chunks),
        grid=(batch, steps),
        in_specs=[col(COL_HQ), col(COL_HF), col(COL_HI),
                  full((1, HG_WIDTH)), full((1, HG_WIDTH)), full((1, HG_WIDTH)),
                  full(a_all.shape), full(mask2.shape)],
        out_specs=pl.BlockSpec((tm, HG_WIDTH), lambda b, i: (b * steps + i, 0)),
        out_shape=jax.ShapeDtypeStruct((n, HG_WIDTH), F32),
        scratch_shapes=[pltpu.VMEM((HG_HEADS // 2, LANES, LANES), F32)],
        compiler_params=_params("parallel", "arbitrary"),
        name="hgrn2",
    )(h, h, h, jnp.log(lb), jnp.log1p(-lb), 1.0 - lb, a_all, mask2)


def _compress_body(x_ref, w1_ref, w2k_ref, w2vt_ref, pos_ref, kc_ref, vct_ref):
    def hidden(kv):
        x = x_ref[0, kv, 0].astype(BF16)
        w1 = w1_ref[kv].astype(BF16)
        half = w1.shape[0] // 2
        a = _dot(x, w1[:half])
        b = _dot(x, w1[half:])
        pos = jnp.broadcast_to(pos_ref[kv], (8, pos_ref.shape[-1])).astype(BF16)
        pw = _dot(pos, w1)[0:1]
        b_next = jnp.concatenate([b[1:], b[:1]], axis=0)
        return _gelu(a + b_next + pw).astype(BF16)

    nrow = x_ref.shape[3]
    kc = _dot(hidden(0), w2k_ref[...].astype(BF16))
    rid = lax.broadcasted_iota(jnp.int32, kc.shape, 0)
    kc_ref[0, 0] = jnp.where(rid < nrow - 1, kc, 0.0).astype(BF16)
    vct = _dot_nt(w2vt_ref[...].astype(BF16), hidden(1))
    cid = lax.broadcasted_iota(jnp.int32, vct.shape, 1)
    vct_ref[0, 0] = jnp.where(cid < nrow - 1, vct, 0.0).astype(BF16)


def _compress(xkv, w1, w2, pos):
    b, _, hk, nr, wd = xkv.shape
    dh = w2.shape[-1]
    w2k = jnp.concatenate([w2[0], w2[0]], axis=-1)
    w2vt = w2[1].T
    full = lambda a: pl.BlockSpec(a.shape, lambda i, k: (0,) * a.ndim)
    return pl.pallas_call(
        _compress_body,
        grid=(b, hk),
        in_specs=[pl.BlockSpec((1, 2, 1, nr, wd), lambda i, k: (i, 0, k, 0, 0)),
                  full(w1), full(w2k), full(w2vt), full(pos)],
        out_specs=[pl.BlockSpec((1, 1, nr, 2 * dh), lambda i, k: (i, k, 0, 0)),
                   pl.BlockSpec((1, 1, dh, nr), lambda i, k: (i, k, 0, 0))],
        out_shape=[jax.ShapeDtypeStruct((b, hk, nr, 2 * dh), BF16),
                   jax.ShapeDtypeStruct((b, hk, dh, nr), BF16)],
        compiler_params=_params("parallel", "parallel"),
        name="nsa_compress",
    )(xkv, w1, w2k, w2vt, pos)


def _emit_heads(o_t, pe_ref, po_ref, out_ref, pair0=0):
    hi, lo = _split2(o_t)
    for j in range(NSA_GROUP // 2):
        a = slice(2 * j * Q_BLOCK, (2 * j + 1) * Q_BLOCK)
        b = slice((2 * j + 1) * Q_BLOCK, (2 * j + 2) * Q_BLOCK)
        out_ref[:, (pair0 + j) * LANES:(pair0 + j + 1) * LANES] = (
            _dot_tn(hi[:, a], pe_ref[...]) + _dot_tn(lo[:, a], pe_ref[...])
            + _dot_tn(hi[:, b], po_ref[...]) + _dot_tn(lo[:, b], po_ref[...]))


def _head_placers():
    eye = np.eye(HEAD_DIM, dtype=np.float32)
    zero = np.zeros_like(eye)
    pe = jnp.asarray(np.concatenate([eye, zero], axis=1), dtype=BF16)
    po = jnp.asarray(np.concatenate([zero, eye], axis=1), dtype=BF16)
    return jnp.asarray(np.eye(LANES, dtype=np.float32), dtype=BF16), pe, po


def _select_body(q_ref, kc_ref, vct_ref, ovt_ref, eye_ref, pe_ref, po_ref, ocmp_ref, bias_ref, *, n_sel):
    c = pl.program_id(1)
    ncmp = kc_ref.shape[-2]
    nblk = ovt_ref.shape[0]
    rows = NSA_GROUP * Q_BLOCK
    q = q_ref[...] * (ATTN_SCALE * LOG2E)
    lane = lax.broadcasted_iota(jnp.int32, (Q_BLOCK, LANES), 1)
    lo_half = lane < HEAD_DIM
    t_lane = c * Q_BLOCK + (lax.broadcasted_iota(jnp.int32, (1, rows), 1) & (Q_BLOCK - 1))
    n_idx = lax.broadcasted_iota(jnp.int32, (ncmp, 1), 0)
    neg = jnp.where((n_idx * CMP_STRIDE + (CMP_BLOCK - 1)) <= t_lane, 0.0, NEG_INF)
    jb = lax.broadcasted_iota(jnp.int32, (nblk, 1), 0)
    t_row = c * Q_BLOCK + lax.broadcasted_iota(jnp.int32, (1, Q_BLOCK), 1)
    cur = lax.shift_right_logical(t_row, SLC_SHIFT)
    forced = (jb == 0) | (jb == cur) | (jb == cur - 1)
    causal = jb * SLC_BLOCK <= t_row
    probs, vals, scores = [], [], []
    for hk in range(NSA_KV_HEADS):
        qts = []
        for g in range(NSA_GROUP):
            hq = hk * NSA_GROUP + g
            pair = q[:, (hq // 2) * LANES:(hq // 2 + 1) * LANES]
            qm = jnp.where(lo_half if hq % 2 == 0 else jnp.logical_not(lo_half), pair, 0.0).astype(BF16)
            qts.append(_dot_nt(eye_ref[...], qm).astype(BF16))
        scores.append(_dot(kc_ref[0, hk], jnp.concatenate(qts, axis=1)) + neg)
    s = jnp.concatenate(scores, axis=1)
    m = jnp.max(s, axis=0, keepdims=True)
    e = jnp.exp2(s - m)
    l = jnp.sum(e, axis=0, keepdims=True)
    p_all = e * jnp.where(m > 0.5 * NEG_INF, 1.0 / l, 0.0)
    for hk in range(NSA_KV_HEADS):
        p = p_all[:, hk * rows:(hk + 1) * rows]
        probs.append(p.astype(BF16))
        psum = p[:, 0:Q_BLOCK]
        for g in range(1, NSA_GROUP):
            psum = psum + p[:, g * Q_BLOCK:(g + 1) * Q_BLOCK]
        ph, plo = _split2(psum)
        imp = _dot(ovt_ref[...], ph) + _dot(ovt_ref[...], plo)
        vals.append(jnp.where(causal, imp + jnp.where(forced, FORCE_SELECT, 0.0), NEG_INF))
    val0 = jnp.concatenate(vals, axis=1)
    val = val0
    for _ in range(n_sel):
        m = jnp.max(val, axis=0, keepdims=True)
        idx = jnp.min(jnp.where(val == m, jb, nblk), axis=0, keepdims=True)
        val = jnp.where(jb == idx, TAKEN, val)
    bias = jnp.where((val == TAKEN) & (val0 > 0.5 * NEG_INF), 0.0, NEG_INF).astype(BF16)
    for hk in range(NSA_KV_HEADS):
        bias_ref[0, hk] = bias[:, hk * Q_BLOCK:(hk + 1) * Q_BLOCK]
        o_t = _dot(vct_ref[0, hk], probs[hk])
        _emit_heads(o_t, pe_ref, po_ref, ocmp_ref, hk * (NSA_GROUP // 2))


def _select(h, kcmp, vcmp_t, batch):
    n = h.shape[0]
    t = n // batch
    nq = t // Q_BLOCK
    ncmp = kcmp.shape[-2]
    nblk = t // SLC_BLOCK
    eye_q, pe, po = _head_placers()
    ii = np.arange(ncmp)[None, :]
    jj = np.arange(nblk)[:, None]
    ovt = ((ii * CMP_STRIDE < (jj + 1) * SLC_BLOCK) & (ii * CMP_STRIDE + CMP_BLOCK > jj * SLC_BLOCK)
           & (ii < ncmp - 1))
    ovt = jnp.asarray(ovt.astype(np.float32), dtype=BF16)
    full = lambda a: pl.BlockSpec(a.shape, lambda b, i: (0,) * a.ndim)
    per_batch = lambda a: pl.BlockSpec((1,) + a.shape[1:], lambda b, i: (b,) + (0,) * (a.ndim - 1))
    return pl.pallas_call(
        functools.partial(_select_body, n_sel=min(N_SLC, nblk)),
        grid=(batch, nq),
        in_specs=[pl.BlockSpec((Q_BLOCK, NSA_WIDTH), lambda b, i: (b * nq + i, COL_NQ // NSA_WIDTH)),
                  per_batch(kcmp), per_batch(vcmp_t), full(ovt), full(eye_q), full(pe), full(po)],
        out_specs=[pl.BlockSpec((Q_BLOCK, NSA_WIDTH), lambda b, i: (b * nq + i, 0)),
                   pl.BlockSpec((1, NSA_KV_HEADS, nblk, Q_BLOCK), lambda b, i: (b, 0, 0, i))],
        out_shape=[jax.ShapeDtypeStruct((n, NSA_WIDTH), F32),
                   jax.ShapeDtypeStruct((batch, NSA_KV_HEADS, nblk, t), BF16)],
        compiler_params=_params("parallel", "parallel"),
        name="nsa_select",
    )(h, kcmp, vcmp_t, ovt, eye_q, pe, po)


def _softmax_tile_t(s, m_prev):
    m_new = jnp.maximum(m_prev, jnp.max(s, axis=0, keepdims=True))
    return m_new, jnp.exp2(s - m_new).astype(BF16)


def _attn_body(q_ref, bias_ref, kaug_ref, vst_ref, kw_ref, vwt_ref, eye_ref, pe_ref, po_ref, oslc_ref, owin_ref,
               qaug_sc, acc_sc, sa_sc, sb_sc):
    c = pl.program_id(2)
    rows = NSA_GROUP * Q_BLOCK
    q4 = q_ref[...] * (ATTN_SCALE * LOG2E)
    lane = lax.broadcasted_iota(jnp.int32, (Q_BLOCK, LANES), 1)
    lo_half = lane < HEAD_DIM
    bias = bias_ref[0, 0]
    for g in range(NSA_GROUP):
        pair = q4[:, (g // 2) * LANES:(g // 2 + 1) * LANES]
        qm = jnp.where(lo_half if g % 2 == 0 else jnp.logical_not(lo_half), pair, 0.0).astype(BF16)
        qaug_sc[0:LANES, g * Q_BLOCK:(g + 1) * Q_BLOCK] = _dot_nt(eye_ref[...], qm).astype(BF16)
        qaug_sc[LANES:, g * Q_BLOCK:(g + 1) * Q_BLOCK] = bias
    t_row = c * Q_BLOCK + (lax.broadcasted_iota(jnp.int32, (1, rows), 1) & (Q_BLOCK - 1))

    acc_sc[...] = jnp.zeros(acc_sc.shape, F32)
    last_tile = kaug_ref.shape[2] // ATTN_TK - 1

    def scores(tile, s_ref, masked):
        start = pl.multiple_of(jnp.minimum(tile, last_tile) * ATTN_TK, ATTN_TK)
        s = _dot(kaug_ref[0, 0, pl.ds(start, ATTN_TK), :], qaug_sc[...])
        if masked:
            kpos = tile * ATTN_TK + lax.broadcasted_iota(jnp.int32, (ATTN_TK, 1), 0)
            s = jnp.where(kpos <= t_row, s, NEG_INF)
        s_ref[...] = s
        return jnp.max(s, axis=0, keepdims=True)

    def accumulate(tile, m_prev, m_tile, s_ref):
        p = jnp.exp2(s_ref[...] - m_tile).astype(BF16)
        pv = _dot(vst_ref[0, 0, jnp.minimum(tile, last_tile)], p)
        m_new = jnp.maximum(m_prev, m_tile)
        acc_sc[...] = jnp.exp2(m_prev - m_new) * acc_sc[...] + jnp.exp2(m_tile - m_new) * pv
        return m_new

    def two_tiles(t0, carry, masked):
        m, mt_a = carry
        mt_b = scores(t0 + 1, sb_sc, masked)
        m = accumulate(t0, m, mt_a, sa_sc)
        mt_a = scores(t0 + 2, sa_sc, masked)
        m = accumulate(t0 + 1, m, mt_b, sb_sc)
        return m, mt_a

    n_full = (c * Q_BLOCK) // ATTN_TK
    n_loop = jnp.maximum(n_full - 1, 0) // 2
    span = WINDOW + Q_BLOCK
    wblk = jnp.maximum(c - WINDOW // Q_BLOCK, 0)
    wstart = pl.multiple_of(wblk * Q_BLOCK, Q_BLOCK)
    s = _dot(kw_ref[0, 0, pl.ds(wstart, span), :], qaug_sc[0:LANES, :])
    mt_0 = scores(0, sa_sc, True)
    kpos = wstart + lax.broadcasted_iota(jnp.int32, (span, 1), 0)
    ok = (kpos <= t_row) & (kpos > t_row - WINDOW)
    _, p = _softmax_tile_t(jnp.where(ok, s, NEG_INF), jnp.full((1, rows), TAKEN, F32))
    acc = _dot(vwt_ref[0, 0, wblk], p[0:Q_BLOCK])
    for i in range(1, span // Q_BLOCK):
        acc = acc + _dot(vwt_ref[0, 0, wblk + i], p[i * Q_BLOCK:(i + 1) * Q_BLOCK])
    _emit_heads(acc[0:HEAD_DIM] / acc[HEAD_DIM:HEAD_DIM + 1], pe_ref, po_ref, owin_ref)

    carry = (jnp.full((1, rows), TAKEN, F32), mt_0)
    carry = lax.fori_loop(0, n_loop // 2,
                          lambda i, cr: two_tiles(4 * i + 2, two_tiles(4 * i, cr, False), False), carry)
    carry = lax.fori_loop(n_loop - n_loop % 2, n_loop, lambda i, cr: two_tiles(2 * i, cr, False), carry)
    m, mt_a = two_tiles(2 * n_loop, carry, True)
    accumulate(2 * n_loop + 2, m, mt_a, sa_sc)
    acc = acc_sc[...]
    _emit_heads(acc[0:HEAD_DIM] / acc[HEAD_DIM:HEAD_DIM + 1], pe_ref, po_ref, oslc_ref)


def _attention(h, bias_t, kaug, vst, kw, vwt, batch):
    n = h.shape[0]
    t = n // batch
    nq = t // Q_BLOCK
    gw = NSA_GROUP * HEAD_DIM
    rows = NSA_GROUP * Q_BLOCK
    eye_q, pe, po = _head_placers()
    res = lambda a: pl.BlockSpec((1, 1) + a.shape[2:], lambda b, k, i: (b, k) + (0,) * (a.ndim - 2))
    full = lambda a: pl.BlockSpec(a.shape, lambda b, k, i: (0,) * a.ndim)
    out_spec = pl.BlockSpec((Q_BLOCK, gw), lambda b, k, i: (b * nq + i, k))
    return pl.pallas_call(
        _attn_body,
        grid=(batch, NSA_KV_HEADS, nq),
        in_specs=[pl.BlockSpec((Q_BLOCK, gw), lambda b, k, i: (b * nq + i, COL_NQ // gw + k)),
                  pl.BlockSpec((1, 1, bias_t.shape[2], Q_BLOCK), lambda b, k, i: (b, k, 0, i)),
                  res(kaug), res(vst), res(kw), res(vwt), full(eye_q), full(pe), full(po)],
        out_specs=[out_spec, out_spec],
        out_shape=[jax.ShapeDtypeStruct((n, NSA_WIDTH), F32)] * 2,
        scratch_shapes=[pltpu.VMEM((kaug.shape[-1], rows), BF16), pltpu.VMEM((V_ROWS, rows), F32),
                        pltpu.VMEM((ATTN_TK, rows), F32), pltpu.VMEM((ATTN_TK, rows), F32)],
        compiler_params=_params("parallel", "parallel", "arbitrary"),
        name="nsa_attention",
    )(h, bias_t, kaug, vst, kw, vwt, eye_q, pe, po)


def _branch_expanders():
    e = np.zeros((N_BRANCH, GATE_PAD, NSA_WIDTH), np.float32)
    for hq in range(NSA_HEADS):
        for br in range(N_BRANCH):
            e[br, hq * N_BRANCH + br, hq * HEAD_DIM:(hq + 1) * HEAD_DIM] = 1.0
    return jnp.asarray(e, dtype=BF16)


def _route(logits):
    lane = lax.broadcasted_iota(jnp.int32, logits.shape, 1)
    is_g = lane < N_GROUPS
    gl = jnp.where(is_g, logits, -jnp.inf)
    gmax = jnp.max(gl, axis=-1, keepdims=True)
    gsum = jnp.sum(jnp.where(is_g, jnp.exp(logits - gmax), 0.0), axis=-1, keepdims=True)
    g_val = 1.0 / gsum
    g_idx = jnp.min(jnp.where(is_g & (logits == gmax), lane, ROUTER_PAD), axis=-1, keepdims=True)
    e_lo = EXPERT_LANE0 + g_idx * EXPERTS_PER_GROUP
    in_sel = (lane >= e_lo) & (lane < e_lo + EXPERTS_PER_GROUP)
    el = jnp.where(in_sel, logits, -jnp.inf)
    m1 = jnp.max(el, axis=-1, keepdims=True)
    i1 = jnp.min(jnp.where(in_sel & (logits == m1), lane, ROUTER_PAD), axis=-1, keepdims=True)
    el2 = jnp.where(lane == i1, -jnp.inf, el)
    m2 = jnp.max(el2, axis=-1, keepdims=True)
    i2 = jnp.min(jnp.where(el2 == m2, lane, ROUTER_PAD), axis=-1, keepdims=True)
    r = jnp.exp(m2 - m1)
    w1 = 1.0 / (1.0 + r)
    w2 = r * w1
    return g_val * (jnp.where(lane == i1, w1, 0.0) + jnp.where(lane == i2, w2, 0.0))


def _out_body(ya_ref, ocmp_ref, oslc_ref, owin_ref, gl_ref, yc_ref, hg_ref, x_ref, gb_ref, ex_ref,
              gmb_ref, gmc_ref, og_ref, w_ref, lg_ref, lb_ref, wrh_ref, wrl_ref, br_ref, x1_ref, gate_ref):
    g = _sigmoid(gl_ref[...] + gb_ref[...])
    yb = (_dot2(g, ex_ref[0]) * ocmp_ref[...] + _dot2(g, ex_ref[1]) * oslc_ref[...]
          + _dot2(g, ex_ref[2]) * owin_ref[...])
    n_ab = GM_WIDTH + NSA_WIDTH
    ybn = yb * lax.rsqrt(_dot((yb * yb).astype(BF16), gmb_ref[...]) + RMS_EPS) * og_ref[:, GM_WIDTH:n_ab]
    yc = yc_ref[...]
    hg = hg_ref[...]
    ycn = (yc * lax.rsqrt(_dot((yc * yc).astype(BF16), gmc_ref[...]) + RMS_EPS) * og_ref[:, n_ab:]
           * (hg * _sigmoid(hg)))
    y = (_dot(ya_ref[...].astype(BF16), w_ref[0:GM_WIDTH, :]) + _dot(ybn.astype(BF16), w_ref[GM_WIDTH:n_ab, :])
         + _dot(ycn.astype(BF16), w_ref[n_ab:, :]))
    x1 = _layer_norm(DEEPNORM_ALPHA * x_ref[...] + y, lg_ref[...], lb_ref[...])
    x1_ref[...] = x1
    xh, xl = _split2(x1)
    logits = _dot(xh, wrh_ref[...]) + _dot(xh, wrl_ref[...]) + _dot(xl, wrh_ref[...]) + br_ref[...]
    gate_ref[...] = _route(logits)


def _outproj(ya, ocmp, oslc, owin, h, yc, x2, gate_b, out_gain, w_out, ln_g, ln_b, wr, br):
    n, d = x2.shape
    tm = 512
    row = lambda a: a.reshape(1, -1)
    gb = jnp.pad(gate_b, (0, GATE_PAD - gate_b.shape[0])).reshape(1, GATE_PAD)
    wrh = wr.astype(BF16)
    wrl = (wr - wrh.astype(F32)).astype(BF16)
    ex = _branch_expanders()
    full = lambda shape: pl.BlockSpec(shape, lambda i: (0,) * len(shape))
    tile = lambda w, cb=0: pl.BlockSpec((tm, w), lambda i: (i, cb))
    return pl.pallas_call(
        _out_body,
        grid=(n // tm,),
        in_specs=[tile(GM_WIDTH), tile(NSA_WIDTH), tile(NSA_WIDTH), tile(NSA_WIDTH),
                  tile(GATE_PAD, COL_NG // GATE_PAD), tile(HG_WIDTH), tile(HG_WIDTH, COL_HG // HG_WIDTH), tile(d),
                  full((1, GATE_PAD)), full(ex.shape), full((NSA_WIDTH, NSA_WIDTH)), full((HG_WIDTH, HG_WIDTH)),
                  full((1, d)), full((d, d)), full((1, d)), full((1, d)),
                  full((d, ROUTER_PAD)), full((d, ROUTER_PAD)), full((1, ROUTER_PAD))],
        out_specs=[tile(d), tile(ROUTER_PAD)],
        out_shape=[jax.ShapeDtypeStruct((n, d), F32), jax.ShapeDtypeStruct((n, ROUTER_PAD), F32)],
        compiler_params=_params("parallel"),
        name="outproj_ln_router",
    )(ya, ocmp, oslc, owin, h, yc, h, x2, gb, ex, _group_mean_matrix(NSA_WIDTH), _group_mean_matrix(HG_WIDTH),
      row(out_gain), w_out.astype(BF16), row(ln_g), row(ln_b), wrh, wrl, br)


def _moe_body(x_ref, gate_ref, wg_ref, wu_ref, wd_ref, lg_ref, lb_ref, o_ref,
              xb_sc, acc_sc, gatet_sc, slott_sc, cnt_sc, pstack_sc, ystack_sc):
    step = pl.program_id(1)
    tm = x_ref.shape[0]

    @pl.when(step == 0)
    def _():
        xb_sc[...] = x_ref[...].astype(BF16)
        acc_sc[...] = jnp.zeros_like(acc_sc)
        used = jnp.where(gate_ref[...] > 0.0, 1.0, 0.0)
        r = lax.broadcasted_iota(jnp.int32, (tm, tm), 0)
        q = lax.broadcasted_iota(jnp.int32, (tm, tm), 1)
        after = jnp.where(r < q, 1.0, 0.0).astype(BF16)
        ub = used.astype(BF16)
        ri = lax.broadcasted_iota(jnp.int32, (ROUTER_PAD, ROUTER_PAD), 0)
        qi = lax.broadcasted_iota(jnp.int32, (ROUTER_PAD, ROUTER_PAD), 1)
        eye = jnp.where(ri == qi, 1.0, 0.0).astype(BF16)
        used_t = _dot_nt(eye, ub)
        slott_sc[...] = jnp.where(used_t > 0.0, _dot(used_t.astype(BF16), after), -1.0)
        g_hi, g_mid = _split2(gate_ref[...])
        g_lo = (gate_ref[...] - g_hi.astype(F32) - g_mid.astype(F32)).astype(BF16)
        gatet_sc[...] = _dot_nt(eye, g_hi) + _dot_nt(eye, g_mid) + _dot_nt(eye, g_lo)
        cnt_sc[...] = jnp.broadcast_to(jnp.sum(used, axis=0, keepdims=True), cnt_sc.shape)

    lane = lax.broadcasted_iota(jnp.int32, (1, ROUTER_PAD), 1)
    pending = []
    for sub in range(MOE_EPS):
        e = step * MOE_EPS + sub
        grow = gatet_sc[pl.ds(e + EXPERT_LANE0, 1), :]
        srow = slott_sc[pl.ds(e + EXPERT_LANE0, 1), :]
        n_rows = jnp.sum(jnp.where(lane == e + EXPERT_LANE0, cnt_sc[0:1], 0.0)).astype(jnp.int32)

        def expert_rows(ci, grow=grow, srow=srow, sub=sub):
            base = (ci * MOE_CHUNK).astype(F32)
            r_col = base + lax.broadcasted_iota(jnp.int32, (MOE_CHUNK, 1), 0).astype(F32)
            hit = srow == r_col
            pick = jnp.where(hit, 1.0, 0.0).astype(BF16)
            g_r = jnp.sum(jnp.where(hit, grow, 0.0), axis=-1, keepdims=True)
            xg = _dot(pick, xb_sc[...]).astype(BF16)
            hg = _dot(xg, wg_ref[sub])
            hu = _dot(xg, wu_ref[sub])
            y = (_dot((hg * _sigmoid(hg) * hu).astype(BF16), wd_ref[sub]) * g_r).astype(BF16)
            return pick, y

        off = pl.multiple_of(lax.rem(e, MOE_GROUP) * MOE_CHUNK, MOE_CHUNK)
        pick0, y0 = expert_rows(jnp.int32(0))
        pstack_sc[pl.ds(off, MOE_CHUNK), :] = pick0
        ystack_sc[pl.ds(off, MOE_CHUNK), :] = y0

        pending.append((expert_rows, n_rows))

    for expert_rows, n_rows in pending:
        def overflow(ci, carry, expert_rows=expert_rows):
            pick, y = expert_rows(ci)
            acc_sc[...] += _dot_tn(pick, y)
            return carry

        lax.fori_loop(1, (n_rows + MOE_CHUNK - 1) // MOE_CHUNK, overflow, 0)

    @pl.when(lax.rem(step, MOE_GROUP // MOE_EPS) == MOE_GROUP // MOE_EPS - 1)
    def _():
        acc_sc[...] += _dot_tn(pstack_sc[...], ystack_sc[...])

    @pl.when(step == pl.num_programs(1) - 1)
    def _():
        o_ref[...] = _layer_norm(DEEPNORM_ALPHA * x_ref[...] + acc_sc[...], lg_ref[...], lb_ref[...])


def _moe(x1, gate, wg, wu, wd, ln_g, ln_b):
    n, d = x1.shape
    ne, _, de = wg.shape
    tm = MOE_TM
    row = lambda a: a.reshape(1, -1)
    return pl.pallas_call(
        _moe_body,
        grid=(n // tm, ne // MOE_EPS),
        in_specs=[pl.BlockSpec((tm, d), lambda i, e: (i, 0)),
                  pl.BlockSpec((tm, ROUTER_PAD), lambda i, e: (i, 0)),
                  pl.BlockSpec((MOE_EPS, d, de), lambda i, e: (e, 0, 0)),
                  pl.BlockSpec((MOE_EPS, d, de), lambda i, e: (e, 0, 0)),
                  pl.BlockSpec((MOE_EPS, de, d), lambda i, e: (e, 0, 0)),
                  pl.BlockSpec((1, d), lambda i, e: (0, 0)),
                  pl.BlockSpec((1, d), lambda i, e: (0, 0))],
        out_specs=pl.BlockSpec((tm, d), lambda i, e: (i, 0)),
        out_shape=jax.ShapeDtypeStruct((n, d), F32),
        scratch_shapes=[pltpu.VMEM((tm, d), BF16), pltpu.VMEM((tm, d), F32),
                        pltpu.VMEM((ROUTER_PAD, tm), F32), pltpu.VMEM((ROUTER_PAD, tm), F32),
                        pltpu.VMEM((8, ROUTER_PAD), F32),
                        pltpu.VMEM((MOE_GROUP * MOE_CHUNK, tm), BF16), pltpu.VMEM((MOE_GROUP * MOE_CHUNK, d), BF16)],
        compiler_params=_params("parallel", "arbitrary"),
        name="moe_ln",
    )(x1, gate, wg, wu, wd, row(ln_g), row(ln_b))


def _prep_w_in(w_in):
    names = ("gu", "gv", "nq", "kc", "vc", "ks", "vs", "kw", "vw", "ng", "hq", "hf", "hi", "hg")
    widths = (GM_WIDTH, GM_WIDTH, NSA_WIDTH) + (NSA_KV_WIDTH,) * 6 + (N_BRANCH * NSA_HEADS,) + (HG_WIDTH,) * 4
    ends = np.cumsum(widths)
    col = {nm: w_in[:, int(e - wd):int(e)] for nm, e, wd in zip(names, ends, widths)}
    pad = jnp.zeros((w_in.shape[0], GATE_PAD - N_BRANCH * NSA_HEADS), w_in.dtype)
    heads = lambda a: [a[:, i * HEAD_DIM:(i + 1) * HEAD_DIM] for i in range(NSA_KV_HEADS)]
    kdup = [a for k in (col["ks"], col["kw"]) for hd in heads(k) for a in (hd, hd)]
    w = jnp.concatenate([col[nm] for nm in ("gu", "gv", "nq", "kc", "vc", "hq", "hf", "hi", "hg", "ng")]
                        + [pad] + kdup, axis=1)
    wvt = jnp.concatenate([col["vs"], col["vw"]], axis=1).T
    return w.astype(BF16), wvt.astype(BF16)


def _layer(x2, batch, w_in, gm_v_gain, gm_v_bias, gm_w_s, gm_b_s, cmp_pos, cmp_w1, cmp_w2, nsa_gate_b, lb,
           out_gain, w_out, ln1_g, ln1_b, wr, br, wg, wu, wd, ln2_g, ln2_b):
    n = x2.shape[0]
    t = n // batch
    h, kaug, kw, vst, vwt = _inproj(x2, *_prep_w_in(w_in), batch)

    ya = _gmlp(h, gm_v_gain, gm_v_bias, gm_w_s, gm_b_s, out_gain[:GM_WIDTH])
    yc = _hgrn(h, lb, batch)

    kvc = h[:, COL_KC:COL_KC + 2 * NSA_KV_WIDTH].reshape(batch, t, 2, NSA_KV_HEADS, HEAD_DIM)
    kvc = kvc.transpose(0, 2, 3, 1, 4).reshape(batch, 2, NSA_KV_HEADS, t // CMP_STRIDE, CMP_STRIDE * HEAD_DIM)
    kcmp, vcmp_t = _compress(kvc, cmp_w1, cmp_w2, cmp_pos.reshape(2, 1, CMP_BLOCK * HEAD_DIM))
    ocmp, bias_t = _select(h, kcmp, vcmp_t, batch)
    oslc, owin = _attention(h, bias_t, kaug, vst, kw, vwt, batch)

    x1, gate = _outproj(ya, ocmp, oslc, owin, h, yc, x2, nsa_gate_b, out_gain, w_out, ln1_g, ln1_b, wr, br)
    return _moe(x1, gate, wg, wu, wd, ln2_g, ln2_b)


def kernel(x, w_in, gm_v_gain, gm_v_bias, gm_w_s, gm_b_s, cmp_pos, cmp_w1, cmp_w2, nsa_gate_b, hg_lower, out_gain, w_out, ln1_g, ln1_b, router_group_w, router_group_b, router_expert_w, router_expert_b, exp_w_gate, exp_w_up, exp_w_down, ln2_g, ln2_b):
    batch, t, d = x.shape
    depth = w_in.shape[0]
    lb_all = jnp.cumsum(jax.nn.softmax(hg_lower.astype(F32), axis=0), axis=0)
    lb_all = lb_all - lb_all[0]
    x2 = x.reshape(batch * t, d)
    for l in range(depth):
        pad = ROUTER_PAD - N_GROUPS - N_EXPERTS
        wr = jnp.concatenate([router_group_w[l], router_expert_w[l], jnp.zeros((d, pad), F32)], axis=1)
        br = jnp.concatenate([router_group_b[l], router_expert_b[l], jnp.zeros((pad,), F32)]).reshape(1, ROUTER_PAD)
        x2 = _layer(x2, batch, w_in[l], gm_v_gain[l], gm_v_bias[l], gm_w_s[l], gm_b_s[l], cmp_pos[l], cmp_w1[l],
                    cmp_w2[l], nsa_gate_b[l], lb_all[l], out_gain[l], w_out[l], ln1_g[l], ln1_b[l], wr, br,
                    exp_w_gate[l].astype(BF16), exp_w_up[l].astype(BF16), exp_w_down[l].astype(BF16),
                    ln2_g[l], ln2_b[l])
    return x2.reshape(batch, t, d)
```

```python
import functools

import numpy as np
import jax
import jax.numpy as jnp
from jax import lax
from jax.experimental import pallas as pl
from jax.experimental.pallas import tpu as pltpu

F32 = jnp.float32
BF16 = jnp.bfloat16

HEAD_DIM = 64
HEAD_SHIFT = HEAD_DIM.bit_length() - 1
GM_GROUPS = 4
GM_WIDTH = GM_GROUPS * HEAD_DIM
GM_CHUNK = 128
NSA_HEADS = 8
NSA_KV_HEADS = 2
NSA_GROUP = NSA_HEADS // NSA_KV_HEADS
NSA_WIDTH = NSA_HEADS * HEAD_DIM
NSA_KV_WIDTH = NSA_KV_HEADS * HEAD_DIM
CMP_BLOCK = 32
CMP_STRIDE = 16
SLC_BLOCK = 64
SLC_SHIFT = SLC_BLOCK.bit_length() - 1
N_SLC = 16
WINDOW = 512
Q_BLOCK = 128
N_BRANCH = 3
HG_HEADS = 4
HG_WIDTH = HG_HEADS * HEAD_DIM
HG_CHUNK = 64
N_GROUPS = 4
EXPERTS_PER_GROUP = 4
N_EXPERTS = N_GROUPS * EXPERTS_PER_GROUP
DEPTH = 2
DEEPNORM_ALPHA = (2.0 * DEPTH) ** 0.25
LN_EPS = 1e-5
RMS_EPS = 1e-6
NEG_INF = -1e30
FORCE_SELECT = 1e4
TAKEN = -(2.0 ** 126)
ATTN_SCALE = HEAD_DIM ** -0.5

LANES = 128
GATE_PAD = LANES
ROUTER_PAD = LANES
EXPERT_LANE0 = N_GROUPS
VMEM_LIMIT = 56 * 1024 * 1024

COL_GU = 0
COL_GV = 256
COL_NQ = 512
COL_KC = 1024
COL_HQ = 1280
COL_HF = 1536
COL_HI = 1792
COL_HG = 2048
COL_NG = 2304
H_COLS = COL_NG + GATE_PAD
KDUP_COLS = 4 * NSA_KV_WIDTH

ATTN_TK = 512
V_ROWS = HEAD_DIM + 16
LOG2E = 1.4426950408889634
MOE_TM = 1024
MOE_CHUNK = 160
MOE_GROUP = 8
MOE_EPS = 2


def _dot(a, b):
    return jnp.dot(a, b, preferred_element_type=F32)


def _dot_nt(a, b):
    return lax.dot_general(a, b, (((1,), (1,)), ((), ())), preferred_element_type=F32)


def _dot_tn(a, b):
    return lax.dot_general(a, b, (((0,), (0,)), ((), ())), preferred_element_type=F32)


def _split2(x):
    hi = x.astype(BF16)
    lo = (x - hi.astype(F32)).astype(BF16)
    return hi, lo


def _dot2(x, m, lhs=False):
    hi, lo = _split2(x)
    return _dot(m, hi) + _dot(m, lo) if lhs else _dot(hi, m) + _dot(lo, m)


def _gelu(x):
    return 0.5 * x * (1.0 + jnp.tanh(0.7978845608028654 * (x + 0.044715 * (x * x * x))))


def _sigmoid(x):
    return 1.0 / (1.0 + jnp.exp(-x))


def _log_sigmoid(x):
    return jnp.minimum(x, 0.0) - jnp.log1p(jnp.exp(-jnp.abs(x)))


def _layer_norm(z, g, b):
    mu = jnp.mean(z, axis=-1, keepdims=True)
    zc = z - mu
    var = jnp.mean(zc * zc, axis=-1, keepdims=True)
    return zc * lax.rsqrt(var + LN_EPS) * g + b


def _params(*sem):
    return pltpu.CompilerParams(dimension_semantics=sem, vmem_limit_bytes=VMEM_LIMIT)


def _inproj_body(x_ref, w_ref, wvt_ref, h_ref, kaug_ref, kw_ref, vst_ref, vwt_ref):
    xb = x_ref[...].astype(BF16)
    tm = xb.shape[0]
    acc = _dot(xb, w_ref[...])
    h_ref[...] = acc[:, :H_COLS]
    nblk = kaug_ref.shape[-1] - LANES
    pos = pl.program_id(1) * tm + lax.broadcasted_iota(jnp.int32, (tm, nblk), 0)
    blk = lax.broadcasted_iota(jnp.int32, (tm, nblk), 1)
    onehot = jnp.where(lax.shift_right_logical(pos, SLC_SHIFT) == blk, 1.0, 0.0).astype(BF16)
    for hk in range(NSA_KV_HEADS):
        c0 = H_COLS + hk * LANES
        kaug_ref[0, hk, :, 0:LANES] = acc[:, c0:c0 + LANES].astype(BF16)
        kaug_ref[0, hk, :, LANES:] = onehot
        c1 = H_COLS + (NSA_KV_HEADS + hk) * LANES
        kw_ref[0, hk] = acc[:, c1:c1 + LANES].astype(BF16)
    vt = _dot_nt(wvt_ref[...], xb)
    ones_then_zeros = lambda w: jnp.where(
        lax.broadcasted_iota(jnp.int32, (V_ROWS - HEAD_DIM, w), 0) == 0, 1.0, 0.0).astype(BF16)
    tail, tail_q = ones_then_zeros(tm), ones_then_zeros(Q_BLOCK)
    for hk in range(NSA_KV_HEADS):
        vst_ref[0, hk, 0, 0:HEAD_DIM, :] = vt[hk * HEAD_DIM:(hk + 1) * HEAD_DIM].astype(BF16)
        vst_ref[0, hk, 0, HEAD_DIM:, :] = tail
        r0 = (NSA_KV_HEADS + hk) * HEAD_DIM
        for i in range(tm // Q_BLOCK):
            ls = slice(i * Q_BLOCK, (i + 1) * Q_BLOCK)
            vwt_ref[0, hk, i, 0:HEAD_DIM, :] = vt[r0:r0 + HEAD_DIM, ls].astype(BF16)
            vwt_ref[0, hk, i, HEAD_DIM:, :] = tail_q


def _inproj(x2, w, wvt, batch):
    n, d = x2.shape
    t = n // batch
    tm = ATTN_TK
    steps = t // tm
    nblk = t // SLC_BLOCK
    hk = NSA_KV_HEADS
    return pl.pallas_call(
        _inproj_body,
        grid=(batch, steps),
        in_specs=[pl.BlockSpec((tm, d), lambda b, j: (b * steps + j, 0)),
                  pl.BlockSpec(w.shape, lambda b, j: (0, 0)),
                  pl.BlockSpec(wvt.shape, lambda b, j: (0, 0))],
        out_specs=[pl.BlockSpec((tm, H_COLS), lambda b, j: (b * steps + j, 0)),
                   pl.BlockSpec((1, hk, tm, LANES + nblk), lambda b, j: (b, 0, j, 0)),
                   pl.BlockSpec((1, hk, tm, LANES), lambda b, j: (b, 0, j, 0)),
                   pl.BlockSpec((1, hk, 1, V_ROWS, tm), lambda b, j: (b, 0, j, 0, 0)),
                   pl.BlockSpec((1, hk, tm // Q_BLOCK, V_ROWS, Q_BLOCK), lambda b, j: (b, 0, j, 0, 0))],
        out_shape=[jax.ShapeDtypeStruct((n, H_COLS), F32),
                   jax.ShapeDtypeStruct((batch, hk, t, LANES + nblk), BF16),
                   jax.ShapeDtypeStruct((batch, hk, t, LANES), BF16),
                   jax.ShapeDtypeStruct((batch, hk, steps, V_ROWS, tm), BF16),
                   jax.ShapeDtypeStruct((batch, hk, t // Q_BLOCK, V_ROWS, Q_BLOCK), BF16)],
        compiler_params=_params("parallel", "parallel"),
        name="inproj",
    )(x2, w, wvt)


def _group_mean_matrix(width):
    g = np.arange(width) // HEAD_DIM
    return jnp.asarray((g[:, None] == g[None, :]).astype(np.float32) / HEAD_DIM, dtype=BF16)


def _gmlp_body(gu_ref, gv_ref, gain_ref, bias_ref, ws_ref, bsx_ref, gm_ref, og_ref, o_ref, *, chunks):
    gm = gm_ref[...]
    row = lax.broadcasted_iota(jnp.int32, (GM_CHUNK, GM_CHUNK), 0)
    col = lax.broadcasted_iota(jnp.int32, (GM_CHUNK, GM_CHUNK), 1)
    tril = row >= col
    lane_grp = lax.shift_right_logical(lax.broadcasted_iota(jnp.int32, (GM_CHUNK, GM_WIDTH), 1), HEAD_SHIFT)
    ws = [jnp.where(tril, ws_ref[g], 0.0).astype(BF16) for g in range(GM_GROUPS)]
    v = _gelu(gv_ref[...])
    vc = v - _dot2(v, gm)
    var = _dot2(vc * vc, gm)
    vn = (vc * lax.rsqrt(var + LN_EPS) * gain_ref[...] + bias_ref[...]).astype(BF16)
    zs = []
    for c in range(chunks):
        vn_c = vn[c * GM_CHUNK:(c + 1) * GM_CHUNK]
        z = bsx_ref[...]
        for g in range(GM_GROUPS):
            z = z + jnp.where(lane_grp == g, _dot(ws[g], vn_c), 0.0)
        zs.append(z)
    y = _gelu(gu_ref[...]) * jnp.concatenate(zs, axis=0)
    ms = _dot2(y * y, gm)
    o_ref[...] = y * lax.rsqrt(ms + RMS_EPS) * og_ref[...]


def _gmlp(h, gain, bias, w_s, b_s, out_gain_a):
    n = h.shape[0]
    chunks = 4
    tm = chunks * GM_CHUNK
    bsx = jnp.repeat(b_s.T, HEAD_DIM, axis=1)
    row = lambda a: a.reshape(1, -1)
    full = lambda shape: pl.BlockSpec(shape, lambda i: (0,) * len(shape))
    return pl.pallas_call(
        functools.partial(_gmlp_body, chunks=chunks),
        grid=(n // tm,),
        in_specs=[
            pl.BlockSpec((tm, GM_WIDTH), lambda i: (i, COL_GU // GM_WIDTH)),
            pl.BlockSpec((tm, GM_WIDTH), lambda i: (i, COL_GV // GM_WIDTH)),
            full((1, GM_WIDTH)), full((1, GM_WIDTH)),
            full((GM_GROUPS, GM_CHUNK, GM_CHUNK)), full((GM_CHUNK, GM_WIDTH)),
            full((GM_WIDTH, GM_WIDTH)), full((1, GM_WIDTH)),
        ],
        out_specs=pl.BlockSpec((tm, GM_WIDTH), lambda i: (i, 0)),
        out_shape=jax.ShapeDtypeStruct((n, GM_WIDTH), F32),
        compiler_params=_params("parallel"),
        name="gmlp",
    )(h, h, row(gain), row(bias), w_s, bsx, _group_mean_matrix(GM_WIDTH), row(out_gain_a))


N_LEVELS = 6


def _hgrn_constants():
    c = HG_CHUNK
    t = np.arange(c)[:, None]
    u = np.arange(c)[None, :]
    mats, masks = [], []
    m = c // 2
    while m >= 1:
        p = (t // (2 * m)) * (2 * m) + m - 1
        mats.append(np.where(t > p, (u > p) & (u <= t), (u > t) & (u <= p)))
        masks.append(((t // (2 * m)) == (u // (2 * m))) & ((t % (2 * m)) >= m) & ((u % (2 * m)) < m))
        m //= 2
    mats.append(u <= t)
    mats.append(u > t)
    masks.append(np.eye(c, dtype=bool))
    a_all = np.concatenate(mats, 0).astype(np.float32)
    mask2 = np.stack([np.concatenate([mk, mk], 0) for mk in masks]).astype(np.float32)
    return jnp.asarray(a_all, dtype=BF16), jnp.asarray(mask2)


def _hgrn_body(q_ref, f_ref, i_ref, loglb_ref, log1m_ref, oml_ref, a_ref, mk_ref, o_ref, st_ref, *, chunks):
    @pl.when(pl.program_id(1) == 0)
    def _():
        st_ref[...] = jnp.zeros_like(st_ref)

    c = HG_CHUNK
    a_all = a_ref[...]
    lane = lax.broadcasted_iota(jnp.int32, (c, LANES), 1)
    lo_half = lane < HEAD_DIM
    r2 = lax.broadcasted_iota(jnp.int32, (LANES, LANES), 0)
    c2 = lax.broadcasted_iota(jnp.int32, (LANES, LANES), 1)
    same_head = (r2 < HEAD_DIM) == (c2 < HEAD_DIM)
    fl = f_ref[...]
    la = loglb_ref[...]
    lb2 = log1m_ref[...] + _log_sigmoid(fl)
    lf_all = jnp.maximum(la, lb2) + jnp.log1p(jnp.exp(-jnp.abs(la - lb2)))
    k_all = oml_ref[...] * (1.0 / (1.0 + jnp.exp(fl)))
    e_alls = []
    for ch in range(chunks):
        d_all = _dot2(lf_all[ch * c:(ch + 1) * c], a_all, lhs=True)
        e_alls.append(jnp.exp(d_all))
    pending = []
    for ch in range(chunks):
        sl = slice(ch * c, (ch + 1) * c)
        q, k, iv, e_all = q_ref[sl, :], k_all[sl], i_ref[sl, :], e_alls[ch]
        for p in range(HG_HEADS // 2):
            ls = slice(p * LANES, (p + 1) * LANES)
            qp, kp, ip = q[:, ls], k[:, ls], iv[:, ls]
            att = jnp.zeros((2 * c, c), F32)
            for lv in range(N_LEVELS + 1):
                if lv < N_LEVELS:
                    el = e_all[lv * c:(lv + 1) * c, ls]
                    ql, kl = qp * el, kp * el
                else:
                    ql, kl = qp, kp
                qstack = jnp.concatenate([jnp.where(lo_half, ql, 0.0), jnp.where(lo_half, 0.0, ql)], axis=0)
                att = att + _dot_nt(qstack.astype(BF16), kl.astype(BF16)) * mk_ref[lv]
            ipb = ip.astype(BF16)
            o_intra = jnp.where(lo_half, _dot(att[:c].astype(BF16), ipb), _dot(att[c:].astype(BF16), ipb))
            eb = e_all[N_LEVELS * c:(N_LEVELS + 1) * c, ls]
            ebl = e_all[(N_LEVELS + 1) * c:(N_LEVELS + 2) * c, ls]
            upd = jnp.where(same_head, _dot_tn(ipb, (kp * ebl).astype(BF16)), 0.0)
            pending.append((sl, ls, p, o_intra, (qp * eb).astype(BF16), eb[c - 1:c, :], upd))
    states = [st_ref[p] for p in range(HG_HEADS // 2)]
    for sl, ls, p, o_intra, qe, decay, upd in pending:
        o_ref[sl, ls] = o_intra + _dot_nt(qe, states[p].astype(BF16))
        states[p] = states[p] * decay + upd
    for p in range(HG_HEADS // 2):
        st_ref[p] = states[p]


def _hgrn(h, lb, batch):
    n = h.shape[0]
    t = n // batch
    chunks = 8
    tm = chunks * HG_CHUNK
    steps = t // tm
    a_all, mask2 = _hgrn_constants()
    lb = lb.reshape(1, HG_WIDTH).astype(F32)
    full = lambda shape: pl.BlockSpec(shape, lambda b, i: (0,) * len(shape))
    col = lambda off: pl.BlockSpec((tm, HG_WIDTH), lambda b, i: (b * steps + i, off // HG_WIDTH))
    return pl.pallas_call(
        functools.partial(_hgrn_body, chunks=chunks),
        grid=(batch, steps),
        in_specs=[col(COL_HQ), col(COL_HF), col(COL_HI),
                  full((1, HG_WIDTH)), full((1, HG_WIDTH)), full((1, HG_WIDTH)),
                  full(a_all.shape), full(mask2.shape)],
        out_specs=pl.BlockSpec((tm, HG_WIDTH), lambda b, i: (b * steps + i, 0)),
        out_shape=jax.ShapeDtypeStruct((n, HG_WIDTH), F32),
        scratch_shapes=[pltpu.VMEM((HG_HEADS // 2, LANES, LANES), F32)],
        compiler_params=_params("parallel", "arbitrary"),
        name="hgrn2",
    )(h, h, h, jnp.log(lb), jnp.log1p(-lb), 1.0 - lb, a_all, mask2)


def _compress_body(x_ref, w1_ref, w2k_ref, w2vt_ref, pos_ref, kc_ref, vct_ref):
    def hidden(kv):
        x = x_ref[0, kv, 0].astype(BF16)
        w1 = w1_ref[kv].astype(BF16)
        half = w1.shape[0] // 2
        a = _dot(x, w1[:half])
        b = _dot(x, w1[half:])
        pos = jnp.broadcast_to(pos_ref[kv], (8, pos_ref.shape[-1])).astype(BF16)
        pw = _dot(pos, w1)[0:1]
        b_next = jnp.concatenate([b[1:], b[:1]], axis=0)
        return _gelu(a + b_next + pw).astype(BF16)

    nrow = x_ref.shape[3]
    kc = _dot(hidden(0), w2k_ref[...].astype(BF16))
    rid = lax.broadcasted_iota(jnp.int32, kc.shape, 0)
    kc_ref[0, 0] = jnp.where(rid < nrow - 1, kc, 0.0).astype(BF16)
    vct = _dot_nt(w2vt_ref[...].astype(BF16), hidden(1))
    cid = lax.broadcasted_iota(jnp.int32, vct.shape, 1)
    vct_ref[0, 0] = jnp.where(cid < nrow - 1, vct, 0.0).astype(BF16)


def _compress(xkv, w1, w2, pos):
    b, _, hk, nr, wd = xkv.shape
    dh = w2.shape[-1]
    w2k = jnp.concatenate([w2[0], w2[0]], axis=-1)
    w2vt = w2[1].T
    full = lambda a: pl.BlockSpec(a.shape, lambda i, k: (0,) * a.ndim)
    return pl.pallas_call(
        _compress_body,
        grid=(b, hk),
        in_specs=[pl.BlockSpec((1, 2, 1, nr, wd), lambda i, k: (i, 0, k, 0, 0)),
                  full(w1), full(w2k), full(w2vt), full(pos)],
        out_specs=[pl.BlockSpec((1, 1, nr, 2 * dh), lambda i, k: (i, k, 0, 0)),
                   pl.BlockSpec((1, 1, dh, nr), lambda i, k: (i, k, 0, 0))],
        out_shape=[jax.ShapeDtypeStruct((b, hk, nr, 2 * dh), BF16),
                   jax.ShapeDtypeStruct((b, hk, dh, nr), BF16)],
        compiler_params=_params("parallel", "parallel"),
        name="nsa_compress",
    )(xkv, w1, w2k, w2vt, pos)


def _emit_heads(o_t, pe_ref, po_ref, out_ref, pair0=0):
    hi, lo = _split2(o_t)
    for j in range(NSA_GROUP // 2):
        a = slice(2 * j * Q_BLOCK, (2 * j + 1) * Q_BLOCK)
        b = slice((2 * j + 1) * Q_BLOCK, (2 * j + 2) * Q_BLOCK)
        out_ref[:, (pair0 + j) * LANES:(pair0 + j + 1) * LANES] = (
            _dot_tn(hi[:, a], pe_ref[...]) + _dot_tn(lo[:, a], pe_ref[...])
            + _dot_tn(hi[:, b], po_ref[...]) + _dot_tn(lo[:, b], po_ref[...]))


def _head_placers():
    eye = np.eye(HEAD_DIM, dtype=np.float32)
    zero = np.zeros_like(eye)
    pe = jnp.asarray(np.concatenate([eye, zero], axis=1), dtype=BF16)
    po = jnp.asarray(np.concatenate([zero, eye], axis=1), dtype=BF16)
    return jnp.asarray(np.eye(LANES, dtype=np.float32), dtype=BF16), pe, po


def _select_body(q_ref, kc_ref, vct_ref, ovt_ref, eye_ref, pe_ref, po_ref, ocmp_ref, bias_ref, *, n_sel):
    c = pl.program_id(1)
    ncmp = kc_ref.shape[-2]
    nblk = ovt_ref.shape[0]
    rows = NSA_GROUP * Q_BLOCK
    q = q_ref[...] * (ATTN_SCALE * LOG2E)
    lane = lax.broadcasted_iota(jnp.int32, (Q_BLOCK, LANES), 1)
    lo_half = lane < HEAD_DIM
    t_lane = c * Q_BLOCK + (lax.broadcasted_iota(jnp.int32, (1, rows), 1) & (Q_BLOCK - 1))
    n_idx = lax.broadcasted_iota(jnp.int32, (ncmp, 1), 0)
    neg = jnp.where((n_idx * CMP_STRIDE + (CMP_BLOCK - 1)) <= t_lane, 0.0, NEG_INF)
    jb = lax.broadcasted_iota(jnp.int32, (nblk, 1), 0)
    t_row = c * Q_BLOCK + lax.broadcasted_iota(jnp.int32, (1, Q_BLOCK), 1)
    cur = lax.shift_right_logical(t_row, SLC_SHIFT)
    forced = (jb == 0) | (jb == cur) | (jb == cur - 1)
    causal = jb * SLC_BLOCK <= t_row
    probs, vals, scores = [], [], []
    for hk in range(NSA_KV_HEADS):
        qts = []
        for g in range(NSA_GROUP):
            hq = hk * NSA_GROUP + g
            pair = q[:, (hq // 2) * LANES:(hq // 2 + 1) * LANES]
            qm = jnp.where(lo_half if hq % 2 == 0 else jnp.logical_not(lo_half), pair, 0.0).astype(BF16)
            qts.append(_dot_nt(eye_ref[...], qm).astype(BF16))
        scores.append(_dot(kc_ref[0, hk], jnp.concatenate(qts, axis=1)) + neg)
    s = jnp.concatenate(scores, axis=1)
    m = jnp.max(s, axis=0, keepdims=True)
    e = jnp.exp2(s - m)
    l = jnp.sum(e, axis=0, keepdims=True)
    p_all = e * jnp.where(m > 0.5 * NEG_INF, 1.0 / l, 0.0)
    for hk in range(NSA_KV_HEADS):
        p = p_all[:, hk * rows:(hk + 1) * rows]
        probs.append(p.astype(BF16))
        psum = p[:, 0:Q_BLOCK]
        for g in range(1, NSA_GROUP):
            psum = psum + p[:, g * Q_BLOCK:(g + 1) * Q_BLOCK]
        ph, plo = _split2(psum)
        imp = _dot(ovt_ref[...], ph) + _dot(ovt_ref[...], plo)
        vals.append(jnp.where(causal, imp + jnp.where(forced, FORCE_SELECT, 0.0), NEG_INF))
    val0 = jnp.concatenate(vals, axis=1)
    val = val0
    for _ in range(n_sel):
        m = jnp.max(val, axis=0, keepdims=True)
        idx = jnp.min(jnp.where(val == m, jb, nblk), axis=0, keepdims=True)
        val = jnp.where(jb == idx, TAKEN, val)
    bias = jnp.where((val == TAKEN) & (val0 > 0.5 * NEG_INF), 0.0, NEG_INF).astype(BF16)
    for hk in range(NSA_KV_HEADS):
        bias_ref[0, hk] = bias[:, hk * Q_BLOCK:(hk + 1) * Q_BLOCK]
        o_t = _dot(vct_ref[0, hk], probs[hk])
        _emit_heads(o_t, pe_ref, po_ref, ocmp_ref, hk * (NSA_GROUP // 2))


def _select(h, kcmp, vcmp_t, batch):
    n = h.shape[0]
    t = n // batch
    nq = t // Q_BLOCK
    ncmp = kcmp.shape[-2]
    nblk = t // SLC_BLOCK
    eye_q, pe, po = _head_placers()
    ii = np.arange(ncmp)[None, :]
    jj = np.arange(nblk)[:, None]
    ovt = ((ii * CMP_STRIDE < (jj + 1) * SLC_BLOCK) & (ii * CMP_STRIDE + CMP_BLOCK > jj * SLC_BLOCK)
           & (ii < ncmp - 1))
    ovt = jnp.asarray(ovt.astype(np.float32), dtype=BF16)
    full = lambda a: pl.BlockSpec(a.shape, lambda b, i: (0,) * a.ndim)
    per_batch = lambda a: pl.BlockSpec((1,) + a.shape[1:], lambda b, i: (b,) + (0,) * (a.ndim - 1))
    return pl.pallas_call(
        functools.partial(_select_body, n_sel=min(N_SLC, nblk)),
        grid=(batch, nq),
        in_specs=[pl.BlockSpec((Q_BLOCK, NSA_WIDTH), lambda b, i: (b * nq + i, COL_NQ // NSA_WIDTH)),
                  per_batch(kcmp), per_batch(vcmp_t), full(ovt), full(eye_q), full(pe), full(po)],
        out_specs=[pl.BlockSpec((Q_BLOCK, NSA_WIDTH), lambda b, i: (b * nq + i, 0)),
                   pl.BlockSpec((1, NSA_KV_HEADS, nblk, Q_BLOCK), lambda b, i: (b, 0, 0, i))],
        out_shape=[jax.ShapeDtypeStruct((n, NSA_WIDTH), F32),
                   jax.ShapeDtypeStruct((batch, NSA_KV_HEADS, nblk, t), BF16)],
        compiler_params=_params("parallel", "parallel"),
        name="nsa_select",
    )(h, kcmp, vcmp_t, ovt, eye_q, pe, po)


def _softmax_tile_t(s, m_prev):
    m_new = jnp.maximum(m_prev, jnp.max(s, axis=0, keepdims=True))
    return m_new, jnp.exp2(s - m_new).astype(BF16)


def _attn_body(q_ref, bias_ref, kaug_ref, vst_ref, kw_ref, vwt_ref, eye_ref, pe_ref, po_ref, oslc_ref, owin_ref,
               qaug_sc, acc_sc, sa_sc, sb_sc, vwin_sc):
    c = pl.program_id(2)
    rows = NSA_GROUP * Q_BLOCK
    q4 = q_ref[...] * (ATTN_SCALE * LOG2E)
    lane = lax.broadcasted_iota(jnp.int32, (Q_BLOCK, LANES), 1)
    lo_half = lane < HEAD_DIM
    bias = bias_ref[0, 0]
    for g in range(NSA_GROUP):
        pair = q4[:, (g // 2) * LANES:(g // 2 + 1) * LANES]
        qm = jnp.where(lo_half if g % 2 == 0 else jnp.logical_not(lo_half), pair, 0.0).astype(BF16)
        qaug_sc[0:LANES, g * Q_BLOCK:(g + 1) * Q_BLOCK] = _dot_nt(eye_ref[...], qm).astype(BF16)
        qaug_sc[LANES:, g * Q_BLOCK:(g + 1) * Q_BLOCK] = bias
    t_row = c * Q_BLOCK + (lax.broadcasted_iota(jnp.int32, (1, rows), 1) & (Q_BLOCK - 1))

    acc_sc[...] = jnp.zeros(acc_sc.shape, F32)
    last_tile = kaug_ref.shape[2] // ATTN_TK - 1

    def scores(tile, s_ref, masked):
        start = pl.multiple_of(jnp.minimum(tile, last_tile) * ATTN_TK, ATTN_TK)
        s = _dot(kaug_ref[0, 0, pl.ds(start, ATTN_TK), :], qaug_sc[...])
        if masked:
            kpos = tile * ATTN_TK + lax.broadcasted_iota(jnp.int32, (ATTN_TK, 1), 0)
            s = jnp.where(kpos <= t_row, s, NEG_INF)
        s_ref[...] = s
        return jnp.max(s, axis=0, keepdims=True)

    def accumulate(tile, m_prev, m_tile, s_ref):
        p = jnp.exp2(s_ref[...] - m_tile).astype(BF16)
        pv = _dot(vst_ref[0, 0, jnp.minimum(tile, last_tile)], p)
        m_new = jnp.maximum(m_prev, m_tile)
        acc_sc[...] = jnp.exp2(m_prev - m_new) * acc_sc[...] + jnp.exp2(m_tile - m_new) * pv
        return m_new

    def two_tiles(t0, carry, masked):
        m, mt_a = carry
        mt_b = scores(t0 + 1, sb_sc, masked)
        m = accumulate(t0, m, mt_a, sa_sc)
        mt_a = scores(t0 + 2, sa_sc, masked)
        m = accumulate(t0 + 1, m, mt_b, sb_sc)
        return m, mt_a

    n_full = (c * Q_BLOCK) // ATTN_TK
    n_loop = jnp.maximum(n_full - 1, 0) // 2
    span = WINDOW + Q_BLOCK
    wblk = jnp.maximum(c - WINDOW // Q_BLOCK, 0)
    wstart = pl.multiple_of(wblk * Q_BLOCK, Q_BLOCK)
    s = _dot(kw_ref[0, 0, pl.ds(wstart, span), :], qaug_sc[0:LANES, :])
    mt_0 = scores(0, sa_sc, True)
    kpos = wstart + lax.broadcasted_iota(jnp.int32, (span, 1), 0)
    ok = (kpos <= t_row) & (kpos > t_row - WINDOW)
    _, p = _softmax_tile_t(jnp.where(ok, s, NEG_INF), jnp.full((1, rows), TAKEN, F32))
    for i in range(span // Q_BLOCK):
        vwin_sc[:, i * Q_BLOCK:(i + 1) * Q_BLOCK] = vwt_ref[0, 0, wblk + i]
    acc = _dot(vwin_sc[...], p)
    _emit_heads(acc[0:HEAD_DIM] / acc[HEAD_DIM:HEAD_DIM + 1], pe_ref, po_ref, owin_ref)

    carry = (jnp.full((1, rows), TAKEN, F32), mt_0)
    carry = lax.fori_loop(0, n_loop // 2,
                          lambda i, cr: two_tiles(4 * i + 2, two_tiles(4 * i, cr, False), False), carry)
    carry = lax.fori_loop(n_loop - n_loop % 2, n_loop, lambda i, cr: two_tiles(2 * i, cr, False), carry)
    m, mt_a = two_tiles(2 * n_loop, carry, True)
    accumulate(2 * n_loop + 2, m, mt_a, sa_sc)
    acc = acc_sc[...]
    _emit_heads(acc[0:HEAD_DIM] / acc[HEAD_DIM:HEAD_DIM + 1], pe_ref, po_ref, oslc_ref)


def _attention(h, bias_t, kaug, vst, kw, vwt, batch):
    n = h.shape[0]
    t = n // batch
    nq = t // Q_BLOCK
    gw = NSA_GROUP * HEAD_DIM
    rows = NSA_GROUP * Q_BLOCK
    eye_q, pe, po = _head_placers()
    res = lambda a: pl.BlockSpec((1, 1) + a.shape[2:], lambda b, k, i: (b, k) + (0,) * (a.ndim - 2))
    full = lambda a: pl.BlockSpec(a.shape, lambda b, k, i: (0,) * a.ndim)
    out_spec = pl.BlockSpec((Q_BLOCK, gw), lambda b, k, i: (b * nq + i, k))
    return pl.pallas_call(
        _attn_body,
        grid=(batch, NSA_KV_HEADS, nq),
        in_specs=[pl.BlockSpec((Q_BLOCK, gw), lambda b, k, i: (b * nq + i, COL_NQ // gw + k)),
                  pl.BlockSpec((1, 1, bias_t.shape[2], Q_BLOCK), lambda b, k, i: (b, k, 0, i)),
                  res(kaug), res(vst), res(kw), res(vwt), full(eye_q), full(pe), full(po)],
        out_specs=[out_spec, out_spec],
        out_shape=[jax.ShapeDtypeStruct((n, NSA_WIDTH), F32)] * 2,
        scratch_shapes=[pltpu.VMEM((kaug.shape[-1], rows), BF16), pltpu.VMEM((V_ROWS, rows), F32),
                        pltpu.VMEM((ATTN_TK, rows), F32), pltpu.VMEM((ATTN_TK, rows), F32),
                        pltpu.VMEM((V_ROWS, WINDOW + Q_BLOCK), BF16)],
        compiler_params=_params("parallel", "parallel", "arbitrary"),
        name="nsa_attention",
    )(h, bias_t, kaug, vst, kw, vwt, eye_q, pe, po)


def _branch_expanders():
    e = np.zeros((N_BRANCH, GATE_PAD, NSA_WIDTH), np.float32)
    for hq in range(NSA_HEADS):
        for br in range(N_BRANCH):
            e[br, hq * N_BRANCH + br, hq * HEAD_DIM:(hq + 1) * HEAD_DIM] = 1.0
    return jnp.asarray(e, dtype=BF16)


def _route(logits):
    lane = lax.broadcasted_iota(jnp.int32, logits.shape, 1)
    is_g = lane < N_GROUPS
    gl = jnp.where(is_g, logits, -jnp.inf)
    gmax = jnp.max(gl, axis=-1, keepdims=True)
    gsum = jnp.sum(jnp.where(is_g, jnp.exp(logits - gmax), 0.0), axis=-1, keepdims=True)
    g_val = 1.0 / gsum
    g_idx = jnp.min(jnp.where(is_g & (logits == gmax), lane, ROUTER_PAD), axis=-1, keepdims=True)
    e_lo = EXPERT_LANE0 + g_idx * EXPERTS_PER_GROUP
    in_sel = (lane >= e_lo) & (lane < e_lo + EXPERTS_PER_GROUP)
    el = jnp.where(in_sel, logits, -jnp.inf)
    m1 = jnp.max(el, axis=-1, keepdims=True)
    i1 = jnp.min(jnp.where(in_sel & (logits == m1), lane, ROUTER_PAD), axis=-1, keepdims=True)
    el2 = jnp.where(lane == i1, -jnp.inf, el)
    m2 = jnp.max(el2, axis=-1, keepdims=True)
    i2 = jnp.min(jnp.where(el2 == m2, lane, ROUTER_PAD), axis=-1, keepdims=True)
    r = jnp.exp(m2 - m1)
    w1 = 1.0 / (1.0 + r)
    w2 = r * w1
    return g_val * (jnp.where(lane == i1, w1, 0.0) + jnp.where(lane == i2, w2, 0.0))


def _out_body(ya_ref, ocmp_ref, oslc_ref, owin_ref, gl_ref, yc_ref, hg_ref, x_ref, gb_ref, ex_ref,
              gmb_ref, gmc_ref, og_ref, w_ref, lg_ref, lb_ref, wrh_ref, wrl_ref, br_ref, x1_ref, gate_ref):
    g = _sigmoid(gl_ref[...] + gb_ref[...])
    yb = (_dot2(g, ex_ref[0]) * ocmp_ref[...] + _dot2(g, ex_ref[1]) * oslc_ref[...]
          + _dot2(g, ex_ref[2]) * owin_ref[...])
    n_ab = GM_WIDTH + NSA_WIDTH
    ybn = yb * lax.rsqrt(_dot((yb * yb).astype(BF16), gmb_ref[...]) + RMS_EPS) * og_ref[:, GM_WIDTH:n_ab]
    yc = yc_ref[...]
    hg = hg_ref[...]
    ycn = (yc * lax.rsqrt(_dot((yc * yc).astype(BF16), gmc_ref[...]) + RMS_EPS) * og_ref[:, n_ab:]
           * (hg * _sigmoid(hg)))
    y = (_dot(ya_ref[...].astype(BF16), w_ref[0:GM_WIDTH, :]) + _dot(ybn.astype(BF16), w_ref[GM_WIDTH:n_ab, :])
         + _dot(ycn.astype(BF16), w_ref[n_ab:, :]))
    x1 = _layer_norm(DEEPNORM_ALPHA * x_ref[...] + y, lg_ref[...], lb_ref[...])
    x1_ref[...] = x1
    xh, xl = _split2(x1)
    logits = _dot(xh, wrh_ref[...]) + _dot(xh, wrl_ref[...]) + _dot(xl, wrh_ref[...]) + br_ref[...]
    gate_ref[...] = _route(logits)


def _outproj(ya, ocmp, oslc, owin, h, yc, x2, gate_b, out_gain, w_out, ln_g, ln_b, wr, br):
    n, d = x2.shape
    tm = 512
    row = lambda a: a.reshape(1, -1)
    gb = jnp.pad(gate_b, (0, GATE_PAD - gate_b.shape[0])).reshape(1, GATE_PAD)
    wrh = wr.astype(BF16)
    wrl = (wr - wrh.astype(F32)).astype(BF16)
    ex = _branch_expanders()
    full = lambda shape: pl.BlockSpec(shape, lambda i: (0,) * len(shape))
    tile = lambda w, cb=0: pl.BlockSpec((tm, w), lambda i: (i, cb))
    return pl.pallas_call(
        _out_body,
        grid=(n // tm,),
        in_specs=[tile(GM_WIDTH), tile(NSA_WIDTH), tile(NSA_WIDTH), tile(NSA_WIDTH),
                  tile(GATE_PAD, COL_NG // GATE_PAD), tile(HG_WIDTH), tile(HG_WIDTH, COL_HG // HG_WIDTH), tile(d),
                  full((1, GATE_PAD)), full(ex.shape), full((NSA_WIDTH, NSA_WIDTH)), full((HG_WIDTH, HG_WIDTH)),
                  full((1, d)), full((d, d)), full((1, d)), full((1, d)),
                  full((d, ROUTER_PAD)), full((d, ROUTER_PAD)), full((1, ROUTER_PAD))],
        out_specs=[tile(d), tile(ROUTER_PAD)],
        out_shape=[jax.ShapeDtypeStruct((n, d), F32), jax.ShapeDtypeStruct((n, ROUTER_PAD), F32)],
        compiler_params=_params("parallel"),
        name="outproj_ln_router",
    )(ya, ocmp, oslc, owin, h, yc, h, x2, gb, ex, _group_mean_matrix(NSA_WIDTH), _group_mean_matrix(HG_WIDTH),
      row(out_gain), w_out.astype(BF16), row(ln_g), row(ln_b), wrh, wrl, br)


def _moe_body(x_ref, gate_ref, wg_ref, wu_ref, wd_ref, lg_ref, lb_ref, o_ref,
              xb_sc, acc_sc, gatet_sc, slott_sc, cnt_sc, pstack_sc, ystack_sc):
    step = pl.program_id(1)
    tm = x_ref.shape[0]

    @pl.when(step == 0)
    def _():
        xb_sc[...] = x_ref[...].astype(BF16)
        acc_sc[...] = jnp.zeros_like(acc_sc)
        used = jnp.where(gate_ref[...] > 0.0, 1.0, 0.0)
        r = lax.broadcasted_iota(jnp.int32, (tm, tm), 0)
        q = lax.broadcasted_iota(jnp.int32, (tm, tm), 1)
        after = jnp.where(r < q, 1.0, 0.0).astype(BF16)
        ub = used.astype(BF16)
        ri = lax.broadcasted_iota(jnp.int32, (ROUTER_PAD, ROUTER_PAD), 0)
        qi = lax.broadcasted_iota(jnp.int32, (ROUTER_PAD, ROUTER_PAD), 1)
        eye = jnp.where(ri == qi, 1.0, 0.0).astype(BF16)
        used_t = _dot_nt(eye, ub)
        slott_sc[...] = jnp.where(used_t > 0.0, _dot(used_t.astype(BF16), after), -1.0)
        g_hi, g_mid = _split2(gate_ref[...])
        g_lo = (gate_ref[...] - g_hi.astype(F32) - g_mid.astype(F32)).astype(BF16)
        gatet_sc[...] = _dot_nt(eye, g_hi) + _dot_nt(eye, g_mid) + _dot_nt(eye, g_lo)
        cnt_sc[...] = jnp.broadcast_to(jnp.sum(used, axis=0, keepdims=True), cnt_sc.shape)

    lane = lax.broadcasted_iota(jnp.int32, (1, ROUTER_PAD), 1)
    pending = []
    for sub in range(MOE_EPS):
        e = step * MOE_EPS + sub
        grow = gatet_sc[pl.ds(e + EXPERT_LANE0, 1), :]
        srow = slott_sc[pl.ds(e + EXPERT_LANE0, 1), :]
        n_rows = jnp.sum(jnp.where(lane == e + EXPERT_LANE0, cnt_sc[0:1], 0.0)).astype(jnp.int32)

        def expert_rows(ci, grow=grow, srow=srow, sub=sub):
            base = (ci * MOE_CHUNK).astype(F32)
            r_col = base + lax.broadcasted_iota(jnp.int32, (MOE_CHUNK, 1), 0).astype(F32)
            hit = srow == r_col
            pick = jnp.where(hit, 1.0, 0.0).astype(BF16)
            g_r = jnp.sum(jnp.where(hit, grow, 0.0), axis=-1, keepdims=True)
            xg = _dot(pick, xb_sc[...]).astype(BF16)
            hg = _dot(xg, wg_ref[sub])
            hu = _dot(xg, wu_ref[sub])
            y = (_dot((hg * _sigmoid(hg) * hu).astype(BF16), wd_ref[sub]) * g_r).astype(BF16)
            return pick, y

        off = pl.multiple_of(lax.rem(e, MOE_GROUP) * MOE_CHUNK, MOE_CHUNK)
        pick0, y0 = expert_rows(jnp.int32(0))
        pstack_sc[pl.ds(off, MOE_CHUNK), :] = pick0
        ystack_sc[pl.ds(off, MOE_CHUNK), :] = y0

        pending.append((expert_rows, n_rows))

    for expert_rows, n_rows in pending:
        def overflow(ci, carry, expert_rows=expert_rows):
            pick, y = expert_rows(ci)
            acc_sc[...] += _dot_tn(pick, y)
            return carry

        lax.fori_loop(1, (n_rows + MOE_CHUNK - 1) // MOE_CHUNK, overflow, 0)

    @pl.when(lax.rem(step, MOE_GROUP // MOE_EPS) == MOE_GROUP // MOE_EPS - 1)
    def _():
        acc_sc[...] += _dot_tn(pstack_sc[...], ystack_sc[...])

    @pl.when(step == pl.num_programs(1) - 1)
    def _():
        o_ref[...] = _layer_norm(DEEPNORM_ALPHA * x_ref[...] + acc_sc[...], lg_ref[...], lb_ref[...])


def _moe(x1, gate, wg, wu, wd, ln_g, ln_b):
    n, d = x1.shape
    ne, _, de = wg.shape
    tm = MOE_TM
    row = lambda a: a.reshape(1, -1)
    return pl.pallas_call(
        _moe_body,
        grid=(n // tm, ne // MOE_EPS),
        in_specs=[pl.BlockSpec((tm, d), lambda i, e: (i, 0)),
                  pl.BlockSpec((tm, ROUTER_PAD), lambda i, e: (i, 0)),
                  pl.BlockSpec((MOE_EPS, d, de), lambda i, e: (e, 0, 0)),
                  pl.BlockSpec((MOE_EPS, d, de), lambda i, e: (e, 0, 0)),
                  pl.BlockSpec((MOE_EPS, de, d), lambda i, e: (e, 0, 0)),
                  pl.BlockSpec((1, d), lambda i, e: (0, 0)),
                  pl.BlockSpec((1, d), lambda i, e: (0, 0))],
        out_specs=pl.BlockSpec((tm, d), lambda i, e: (i, 0)),
        out_shape=jax.ShapeDtypeStruct((n, d), F32),
        scratch_shapes=[pltpu.VMEM((tm, d), BF16), pltpu.VMEM((tm, d), F32),
                        pltpu.VMEM((ROUTER_PAD, tm), F32), pltpu.VMEM((ROUTER_PAD, tm), F32),
                        pltpu.VMEM((8, ROUTER_PAD), F32),
                        pltpu.VMEM((MOE_GROUP * MOE_CHUNK, tm), BF16), pltpu.VMEM((MOE_GROUP * MOE_CHUNK, d), BF16)],
        compiler_params=_params("parallel", "arbitrary"),
        name="moe_ln",
    )(x1, gate, wg, wu, wd, row(ln_g), row(ln_b))


def _prep_w_in(w_in):
    names = ("gu", "gv", "nq", "kc", "vc", "ks", "vs", "kw", "vw", "ng", "hq", "hf", "hi", "hg")
    widths = (GM_WIDTH, GM_WIDTH, NSA_WIDTH) + (NSA_KV_WIDTH,) * 6 + (N_BRANCH * NSA_HEADS,) + (HG_WIDTH,) * 4
    ends = np.cumsum(widths)
    col = {nm: w_in[:, int(e - wd):int(e)] for nm, e, wd in zip(names, ends, widths)}
    pad = jnp.zeros((w_in.shape[0], GATE_PAD - N_BRANCH * NSA_HEADS), w_in.dtype)
    heads = lambda a: [a[:, i * HEAD_DIM:(i + 1) * HEAD_DIM] for i in range(NSA_KV_HEADS)]
    kdup = [a for k in (col["ks"], col["kw"]) for hd in heads(k) for a in (hd, hd)]
    w = jnp.concatenate([col[nm] for nm in ("gu", "gv", "nq", "kc", "vc", "hq", "hf", "hi", "hg", "ng")]
                        + [pad] + kdup, axis=1)
    wvt = jnp.concatenate([col["vs"], col["vw"]], axis=1).T
    return w.astype(BF16), wvt.astype(BF16)


def _layer(x2, batch, w_in, gm_v_gain, gm_v_bias, gm_w_s, gm_b_s, cmp_pos, cmp_w1, cmp_w2, nsa_gate_b, lb,
           out_gain, w_out, ln1_g, ln1_b, wr, br, wg, wu, wd, ln2_g, ln2_b):
    n = x2.shape[0]
    t = n // batch
    h, kaug, kw, vst, vwt = _inproj(x2, *_prep_w_in(w_in), batch)

    ya = _gmlp(h, gm_v_gain, gm_v_bias, gm_w_s, gm_b_s, out_gain[:GM_WIDTH])
    yc = _hgrn(h, lb, batch)

    kvc = h[:, COL_KC:COL_KC + 2 * NSA_KV_WIDTH].reshape(batch, t, 2, NSA_KV_HEADS, HEAD_DIM)
    kvc = kvc.transpose(0, 2, 3, 1, 4).reshape(batch, 2, NSA_KV_HEADS, t // CMP_STRIDE, CMP_STRIDE * HEAD_DIM)
    kcmp, vcmp_t = _compress(kvc, cmp_w1, cmp_w2, cmp_pos.reshape(2, 1, CMP_BLOCK * HEAD_DIM))
    ocmp, bias_t = _select(h, kcmp, vcmp_t, batch)
    oslc, owin = _attention(h, bias_t, kaug, vst, kw, vwt, batch)

    x1, gate = _outproj(ya, ocmp, oslc, owin, h, yc, x2, nsa_gate_b, out_gain, w_out, ln1_g, ln1_b, wr, br)
    return _moe(x1, gate, wg, wu, wd, ln2_g, ln2_b)


def kernel(x, w_in, gm_v_gain, gm_v_bias, gm_w_s, gm_b_s, cmp_pos, cmp_w1, cmp_w2, nsa_gate_b, hg_lower, out_gain, w_out, ln1_g, ln1_b, router_group_w, router_group_b, router_expert_w, router_expert_b, exp_w_gate, exp_w_up, exp_w_down, ln2_g, ln2_b):
    batch, t, d = x.shape
    depth = w_in.shape[0]
    lb_all = jnp.cumsum(jax.nn.softmax(hg_lower.astype(F32), axis=0), axis=0)
    lb_all = lb_all - lb_all[0]
    x2 = x.reshape(batch * t, d)
    for l in range(depth):
        pad = ROUTER_PAD - N_GROUPS - N_EXPERTS
        wr = jnp.concatenate([router_group_w[l], router_expert_w[l], jnp.zeros((d, pad), F32)], axis=1)
        br = jnp.concatenate([router_group_b[l], router_expert_b[l], jnp.zeros((pad,), F32)]).reshape(1, ROUTER_PAD)
        x2 = _layer(x2, batch, w_in[l], gm_v_gain[l], gm_v_bias[l], gm_w_s[l], gm_b_s[l], cmp_pos[l], cmp_w1[l],
                    cmp_w2[l], nsa_gate_b[l], lb_all[l], out_gain[l], w_out[l], ln1_g[l], ln1_b[l], wr, br,
                    exp_w_gate[l].astype(BF16), exp_w_up[l].astype(BF16), exp_w_down[l].astype(BF16),
                    ln2_g[l], ln2_b[l])
    return x2.reshape(batch, t, d)
```
